```python
import math
import jax, jax.numpy as jnp
from jax import lax
import numpy as np

D_MODEL = 2048
BATCH = 4
SEQ = 2048
DEPTH = 2
DEC_BATCH = 8
DEC_SEQ = 32
PAST_LEN = 1024

CHUNK = 64
Q_BLOCK = 128
N_MIXERS = 2
N_MLA = (DEPTH + 1) // 2
N_DIFF = DEPTH // 2
ALPHA = (2 * DEPTH) ** 0.25
BETA = (8 * DEPTH) ** -0.25
LN_EPS = 1e-5
RMS_EPS = 1e-6
NEG = -1e30
MLA_HEADS = 16
MLA_Q_LORA = 512
MLA_KV_LORA = 512
MLA_NOPE = 128
MLA_ROPE = 64
MLA_V = 128
ROPE_THETA = 10000.0
DIFF_HEADS = 8
DIFF_QK = 128
DIFF_V = 256
PEER_HEADS = 8
PEER_TOPK = 16
N_KEYS = 128
N_EXPERTS = N_KEYS * N_KEYS
PEER_DKEY = 256
PEER_BLOCK = 128

kernel_name = "hybrid_mla_diffattn_peer_stream_step"


def layer_norm(x, g, b):
    xf = x.astype(jnp.float32)
    mu = xf.mean(-1, keepdims=True)
    var = jnp.square(xf - mu).mean(-1, keepdims=True)
    return ((xf - mu) * lax.rsqrt(var + LN_EPS)).astype(x.dtype) * g + b


def rms_norm(x, g):
    xf = x.astype(jnp.float32)
    return (xf * lax.rsqrt(jnp.square(xf).mean(-1, keepdims=True) + RMS_EPS)).astype(x.dtype) * g


def rope(x, pos):
    half = MLA_ROPE // 2
    inv = ROPE_THETA ** (-jnp.arange(half, dtype=jnp.float32) / half)
    ang = pos.astype(jnp.float32)[:, None] * inv
    ang = ang.reshape(ang.shape[0], *([1] * (x.ndim - 3)), half)
    cos, sin = jnp.cos(ang).astype(x.dtype), jnp.sin(ang).astype(x.dtype)
    x1, x2 = x[..., :half], x[..., half:]
    return jnp.concatenate([x1 * cos - x2 * sin, x2 * cos + x1 * sin], -1)


def chunk_mask(q_pos, k_pos):
    return (k_pos[None, :] // CHUNK) <= (q_pos[:, None] // CHUNK)


def alibi_slopes(n):
    return 2.0 ** (-8.0 * jnp.arange(1, n + 1, dtype=jnp.float32) / n)


def map_query_blocks(fn, qs, q_pos):
    tq = q_pos.shape[0]
    qb = min(Q_BLOCK, tq)
    nb = tq // qb
    qs_b = tuple(jnp.moveaxis(q.reshape(q.shape[0], nb, qb, *q.shape[2:]), 1, 0) for q in qs)
    pos_b = q_pos.reshape(nb, qb)
    out = lax.map(lambda a: fn(a[0], a[1]), (qs_b, pos_b))
    out = jnp.moveaxis(out, 0, 1)
    return out.reshape(out.shape[0], tq, *out.shape[3:])


def mla_project(x, pos, w_dqkv, g_q, w_uq, g_kv):
    b, t, _ = x.shape
    lat = x @ w_dqkv
    cq, ckv, kr = jnp.split(lat, [MLA_Q_LORA, MLA_Q_LORA + MLA_KV_LORA], -1)
    q = (rms_norm(cq, g_q) @ w_uq).reshape(b, t, MLA_HEADS, MLA_NOPE + MLA_ROPE)
    q_nope, q_rope = q[..., :MLA_NOPE], rope(q[..., MLA_NOPE:], pos)
    return q_nope, q_rope, rms_norm(ckv, g_kv), rope(kr, pos)


def mla_attend(q_nope, q_rope, q_pos, ckv, k_rope, k_pos, w_ukv, w_o):
    b, tk, _ = ckv.shape
    kv = (ckv @ w_ukv).reshape(b, tk, MLA_HEADS, MLA_NOPE + MLA_V)
    k_nope, v = kv[..., :MLA_NOPE], kv[..., MLA_NOPE:]
    scale = (MLA_NOPE + MLA_ROPE) ** -0.5

    def block(qs, qp):
        qn, qr = qs
        s = (jnp.einsum('bqhd,bkhd->bhqk', qn, k_nope)
             + jnp.einsum('bqhr,bkr->bhqk', qr, k_rope)).astype(jnp.float32) * scale
        s = jnp.where(chunk_mask(qp, k_pos), s, NEG)
        p = jax.nn.softmax(s, -1).astype(v.dtype)
        return jnp.einsum('bhqk,bkhd->bqhd', p, v)

    o = map_query_blocks(block, (q_nope, q_rope), q_pos)
    return o.reshape(b, -1, MLA_HEADS * MLA_V) @ w_o


def diff_project(x, w_qkv):
    b, t, _ = x.shape
    nq = DIFF_HEADS * 2 * DIFF_QK
    q, k, v = jnp.split(x @ w_qkv, [nq, 2 * nq], -1)
    return (q.reshape(b, t, DIFF_HEADS, 2 * DIFF_QK), k.reshape(b, t, DIFF_HEADS, 2 * DIFF_QK),
            v.reshape(b, t, DIFF_HEADS, DIFF_V))


def diff_attend(q, q_pos, k, v, k_pos, lam_q1, lam_k1, lam_q2, lam_k2, g_sub, w_o, lam_init):
    b = q.shape[0]
    lam = (jnp.exp(jnp.sum(lam_q1.astype(jnp.float32) * lam_k1.astype(jnp.float32)))
           - jnp.exp(jnp.sum(lam_q2.astype(jnp.float32) * lam_k2.astype(jnp.float32))) + lam_init)
    slopes = alibi_slopes(DIFF_HEADS)
    k1, k2 = k[..., :DIFF_QK], k[..., DIFF_QK:]
    scale = DIFF_QK ** -0.5

    def block(qs, qp):
        (qq,) = qs
        q1, q2 = qq[..., :DIFF_QK], qq[..., DIFF_QK:]
        dist = jnp.abs(qp[:, None] - k_pos[None, :]).astype(jnp.float32)
        bias = jnp.where(chunk_mask(qp, k_pos), -slopes[:, None, None] * dist, NEG)
        a1 = jax.nn.softmax(jnp.einsum('bqhd,bkhd->bhqk', q1, k1).astype(jnp.float32) * scale + bias, -1)
        a2 = jax.nn.softmax(jnp.einsum('bqhd,bkhd->bhqk', q2, k2).astype(jnp.float32) * scale + bias, -1)
        a = (a1 - lam * a2).astype(v.dtype)
        return jnp.einsum('bhqk,bkhd->bqhd', a, v)

    o = map_query_blocks(block, (q,), q_pos)
    o = rms_norm(o, g_sub) * (1.0 - lam_init)
    return o.reshape(b, -1, DIFF_HEADS * DIFF_V) @ w_o


def peer(x, w_query, sub_keys, u_tab, v_tab):
    b, t, d = x.shape
    n = b * t
    nblk = -(-n // PEER_BLOCK)
    xp = jnp.pad(x.reshape(n, d), ((0, nblk * PEER_BLOCK - n), (0, 0))).reshape(nblk, PEER_BLOCK, d)

    def block(xb):
        q = (xb @ w_query).reshape(PEER_BLOCK, PEER_HEADS, 2, PEER_DKEY // 2)
        s = jnp.einsum('nhcd,hckd->nhck', q, sub_keys)
        sv, si = lax.top_k(s, PEER_TOPK)
        cand = (sv[:, :, 0, :, None] + sv[:, :, 1, None, :]).reshape(PEER_BLOCK, PEER_HEADS, -1)
        cidx = (si[:, :, 0, :, None] * N_KEYS + si[:, :, 1, None, :]).reshape(PEER_BLOCK, PEER_HEADS, -1)
        fv, fi = lax.top_k(cand, PEER_TOPK)
        eidx = jnp.take_along_axis(cidx, fi, -1)
        g = jax.nn.softmax(fv.astype(jnp.float32), -1).astype(xb.dtype)
        h = jax.nn.gelu(jnp.einsum('nd,nhkd->nhk', xb, u_tab[eidx]))
        return jnp.einsum('nhk,nhkd->nd', g * h, v_tab[eidx])

    out = lax.map(block, xp).reshape(-1, d)[:n]
    return out.reshape(b, t, d)


def setup_inputs(seed: int = 0) -> dict:
    key = jax.random.key(seed)
    ks = iter(jax.random.split(key, 40))

    def nrm(shape, scale=1.0):
        return jax.random.normal(next(ks), shape, jnp.float32) * scale

    x_prompt = nrm((BATCH, SEQ, D_MODEL))
    x_sample = nrm((DEC_BATCH, DEC_SEQ, D_MODEL))
    cache_mla_ckv = nrm((N_MLA, DEC_BATCH, PAST_LEN, MLA_KV_LORA))
    cache_mla_krope = nrm((N_MLA, DEC_BATCH, PAST_LEN, MLA_ROPE))
    cache_diff_k = nrm((N_DIFF, DEC_BATCH, PAST_LEN, DIFF_HEADS, 2 * DIFF_QK))
    cache_diff_v = nrm((N_DIFF, DEC_BATCH, PAST_LEN, DIFF_HEADS, DIFF_V), BETA)

    mla_w_dqkv = nrm((N_MLA, D_MODEL, MLA_Q_LORA + MLA_KV_LORA + MLA_ROPE), D_MODEL ** -0.5)
    mla_g_q = 1.0 + nrm((N_MLA, MLA_Q_LORA), 0.01)
    mla_w_uq = nrm((N_MLA, MLA_Q_LORA, MLA_HEADS * (MLA_NOPE + MLA_ROPE)), MLA_Q_LORA ** -0.5)
    mla_g_kv = 1.0 + nrm((N_MLA, MLA_KV_LORA), 0.01)
    w_uk = nrm((N_MLA, MLA_KV_LORA, MLA_HEADS, MLA_NOPE), MLA_KV_LORA ** -0.5)
    w_uv = nrm((N_MLA, MLA_KV_LORA, MLA_HEADS, MLA_V), BETA * MLA_KV_LORA ** -0.5)
    mla_w_ukv = jnp.concatenate([w_uk, w_uv], -1).reshape(N_MLA, MLA_KV_LORA, MLA_HEADS * (MLA_NOPE + MLA_V))
    mla_w_o = nrm((N_MLA, MLA_HEADS * MLA_V, D_MODEL), BETA * (MLA_HEADS * MLA_V) ** -0.5)

    w_qk = nrm((N_DIFF, D_MODEL, 2 * DIFF_HEADS * 2 * DIFF_QK), D_MODEL ** -0.5)
    w_v = nrm((N_DIFF, D_MODEL, DIFF_HEADS * DIFF_V), BETA * D_MODEL ** -0.5)
    diff_w_qkv = jnp.concatenate([w_qk, w_v], -1)
    diff_lam_q1 = nrm((N_DIFF, DIFF_QK), 0.1)
    diff_lam_k1 = nrm((N_DIFF, DIFF_QK), 0.1)
    diff_lam_q2 = nrm((N_DIFF, DIFF_QK), 0.1)
    diff_lam_k2 = nrm((N_DIFF, DIFF_QK), 0.1)
    diff_g_sub = 1.0 + nrm((N_DIFF, DIFF_V), 0.01)
    diff_w_o = nrm((N_DIFF, DIFF_HEADS * DIFF_V, D_MODEL), BETA * (DIFF_HEADS * DIFF_V) ** -0.5)

    peer_w_query = nrm((DEPTH, D_MODEL, PEER_HEADS * PEER_DKEY), D_MODEL ** -0.5)
    peer_sub_keys = nrm((DEPTH, PEER_HEADS, 2, N_KEYS, PEER_DKEY // 2), (PEER_DKEY // 2) ** -0.5)
    peer_u = nrm((DEPTH, N_EXPERTS, D_MODEL), D_MODEL ** -0.5)
    peer_v = nrm((DEPTH, N_EXPERTS, D_MODEL), BETA * PEER_HEADS ** -0.5)

    ln_mix_g = 1.0 + nrm((DEPTH, D_MODEL), 0.01)
    ln_mix_b = nrm((DEPTH, D_MODEL), 0.01)
    ln_ffn_g = 1.0 + nrm((DEPTH, D_MODEL), 0.01)
    ln_ffn_b = nrm((DEPTH, D_MODEL), 0.01)

    return {"x_prompt": x_prompt, "x_sample": x_sample,
            "cache_mla_ckv": cache_mla_ckv, "cache_mla_krope": cache_mla_krope,
            "cache_diff_k": cache_diff_k, "cache_diff_v": cache_diff_v,
            "mla_w_dqkv": mla_w_dqkv, "mla_g_q": mla_g_q, "mla_w_uq": mla_w_uq, "mla_g_kv": mla_g_kv,
            "mla_w_ukv": mla_w_ukv, "mla_w_o": mla_w_o,
            "diff_w_qkv": diff_w_qkv, "diff_lam_q1": diff_lam_q1, "diff_lam_k1": diff_lam_k1,
            "diff_lam_q2": diff_lam_q2, "diff_lam_k2": diff_lam_k2, "diff_g_sub": diff_g_sub,
            "diff_w_o": diff_w_o,
            "peer_w_query": peer_w_query, "peer_sub_keys": peer_sub_keys, "peer_u": peer_u, "peer_v": peer_v,
            "ln_mix_g": ln_mix_g, "ln_mix_b": ln_mix_b, "ln_ffn_g": ln_ffn_g, "ln_ffn_b": ln_ffn_b}


def reference(x_prompt, x_sample, cache_mla_ckv, cache_mla_krope, cache_diff_k, cache_diff_v,
              mla_w_dqkv, mla_g_q, mla_w_uq, mla_g_kv, mla_w_ukv, mla_w_o,
              diff_w_qkv, diff_lam_q1, diff_lam_k1, diff_lam_q2, diff_lam_k2, diff_g_sub, diff_w_o,
              peer_w_query, peer_sub_keys, peer_u, peer_v,
              ln_mix_g, ln_mix_b, ln_ffn_g, ln_ffn_b):
    t_p = x_prompt.shape[1]
    t_s = x_sample.shape[1]
    past = cache_mla_ckv.shape[2]
    p_pos = jnp.arange(t_p)
    s_pos = past + jnp.arange(t_s)
    s_kpos = jnp.arange(past + t_s)

    xp, xs = x_prompt, x_sample
    p_ckv, p_kr, p_dk, p_dv = [], [], [], []
    s_ckv, s_kr, s_dk, s_dv = [], [], [], []
    for i in range(DEPTH):
        j = i // N_MIXERS
        if i % N_MIXERS == 0:
            qn, qr, ckv, kr = mla_project(xp, p_pos, mla_w_dqkv[j], mla_g_q[j], mla_w_uq[j], mla_g_kv[j])
            mix_p = mla_attend(qn, qr, p_pos, ckv, kr, p_pos, mla_w_ukv[j], mla_w_o[j])
            p_ckv.append(ckv)
            p_kr.append(kr)
            qn, qr, ckv, kr = mla_project(xs, s_pos, mla_w_dqkv[j], mla_g_q[j], mla_w_uq[j], mla_g_kv[j])
            mix_s = mla_attend(qn, qr, s_pos,
                               jnp.concatenate([cache_mla_ckv[j], ckv], 1),
                               jnp.concatenate([cache_mla_krope[j], kr], 1), s_kpos,
                               mla_w_ukv[j], mla_w_o[j])
            s_ckv.append(ckv)
            s_kr.append(kr)
        else:
            lam_init = 0.8 - 0.6 * math.exp(-0.3 * i)
            lam_args = (diff_lam_q1[j], diff_lam_k1[j], diff_lam_q2[j], diff_lam_k2[j], diff_g_sub[j], diff_w_o[j], lam_init)
            q, k, v = diff_project(xp, diff_w_qkv[j])
            mix_p = diff_attend(q, p_pos, k, v, p_pos, *lam_args)
            p_dk.append(k)
            p_dv.append(v)
            q, k, v = diff_project(xs, diff_w_qkv[j])
            mix_s = diff_attend(q, s_pos, jnp.concatenate([cache_diff_k[j], k], 1),
                                jnp.concatenate([cache_diff_v[j], v], 1), s_kpos, *lam_args)
            s_dk.append(k)
            s_dv.append(v)
        xp = layer_norm(ALPHA * xp + mix_p, ln_mix_g[i], ln_mix_b[i])
        xs = layer_norm(ALPHA * xs + mix_s, ln_mix_g[i], ln_mix_b[i])
        xp = layer_norm(ALPHA * xp + peer(xp, peer_w_query[i], peer_sub_keys[i], peer_u[i], peer_v[i]),
                        ln_ffn_g[i], ln_ffn_b[i])
        xs = layer_norm(ALPHA * xs + peer(xs, peer_w_query[i], peer_sub_keys[i], peer_u[i], peer_v[i]),
                        ln_ffn_g[i], ln_ffn_b[i])

    return (xp, xs, jnp.stack(p_ckv), jnp.stack(p_kr), jnp.stack(p_dk), jnp.stack(p_dv),
            jnp.stack(s_ckv), jnp.stack(s_kr), jnp.stack(s_dk), jnp.stack(s_dv))
```

```python
import functools
import math

import jax
import jax.numpy as jnp
import numpy as np
from jax import lax
from jax.experimental import pallas as pl
from jax.experimental.pallas import tpu as pltpu

F32 = jnp.float32
BF16 = jnp.bfloat16

DEPTH = 2
CHUNK = 64
ALPHA = (2 * DEPTH) ** 0.25
LN_EPS = 1e-5
RMS_EPS = 1e-6
NEG = -1e30
ROPE_THETA = 10000.0
MLA_HEADS = 16
MLA_Q_LORA = 512
MLA_KV_LORA = 512
MLA_NOPE = 128
MLA_ROPE = 64
MLA_V = 128
DIFF_HEADS = 8
DIFF_QK = 128
DIFF_V = 256
PEER_HEADS = 8
PEER_TOPK = 16
N_KEYS = 128
PEER_DKEY = 256

LANE = 128
BF16_SUBLANE = 16
VMEM_LIMIT = 56 << 20


def _tile(n, target, mult=BF16_SUBLANE):
    for t in range(min(n, target), 0, -1):
        if n % t == 0 and t % mult == 0:
            return t
    raise ValueError(f"no tile for {n} (target {target}, multiple of {mult})")


def _params(*sem):
    return pltpu.CompilerParams(dimension_semantics=sem, vmem_limit_bytes=VMEM_LIMIT)


def _layer_norm(y, g, b):
    mu = jnp.mean(y, axis=-1, keepdims=True)
    d = y - mu
    var = jnp.mean(d * d, axis=-1, keepdims=True)
    return d * lax.rsqrt(var + LN_EPS) * g + b


def _rms_norm(y, g):
    return y * lax.rsqrt(jnp.mean(y * y, axis=-1, keepdims=True) + RMS_EPS) * g


def _rope128(r, cos, sin_a, sin_b):
    return r * cos + pltpu.roll(r, 96, 1) * sin_a + pltpu.roll(r, 32, 1) * sin_b


def _mm_kernel(x_ref, w_ref, *refs, nk, n_out):
    outs, acc = refs[:n_out], refs[n_out]
    k = pl.program_id(2)

    @pl.when(k == 0)
    def _():
        acc[...] = jnp.zeros_like(acc)

    acc[...] += jnp.dot(x_ref[...], w_ref[...], preferred_element_type=F32)

    @pl.when(k == nk - 1)
    def _():
        for o in outs:
            o[...] = acc[...].astype(o.dtype)


def _mm(x, w, out_dtypes, *, tm=528, tn=1024, tk=512):
    m, kdim = x.shape
    n = w.shape[1]
    tm, tn, tk = _tile(m, tm), _tile(n, tn, LANE), _tile(kdim, tk, LANE)
    nk = kdim // tk
    outs = pl.pallas_call(
        functools.partial(_mm_kernel, nk=nk, n_out=len(out_dtypes)),
        grid=(m // tm, n // tn, nk),
        in_specs=[pl.BlockSpec((tm, tk), lambda i, j, k: (i, k)),
                  pl.BlockSpec((tk, tn), lambda i, j, k: (k, j))],
        out_specs=[pl.BlockSpec((tm, tn), lambda i, j, k: (i, j)) for _ in out_dtypes],
        out_shape=[jax.ShapeDtypeStruct((m, n), dt) for dt in out_dtypes],
        scratch_shapes=[pltpu.VMEM((tm, tn), F32)],
        compiler_params=_params("parallel", "parallel", "arbitrary"),
        name="mm",
    )(x, w)
    return outs


def _mm_ln_kernel(a_ref, w_ref, r_ref, g_ref, b_ref, of_ref, ob_ref, acc, *, nk):
    k = pl.program_id(1)

    @pl.when(k == 0)
    def _():
        acc[...] = jnp.zeros_like(acc)

    acc[...] += jnp.dot(a_ref[...], w_ref[...], preferred_element_type=F32)

    @pl.when(k == nk - 1)
    def _():
        y = _layer_norm(ALPHA * r_ref[...] + acc[...], g_ref[...], b_ref[...])
        of_ref[...] = y
        ob_ref[...] = y.astype(BF16)


def _mm_ln(a, w, resid, g, b, *, tm=528, tk=512):
    m, kdim = a.shape
    d = w.shape[1]
    tm, tk = _tile(m, tm), _tile(kdim, tk, LANE)
    nk = kdim // tk
    row = lambda i, k: (i, 0)
    return pl.pallas_call(
        functools.partial(_mm_ln_kernel, nk=nk),
        grid=(m // tm, nk),
        in_specs=[pl.BlockSpec((tm, tk), lambda i, k: (i, k)),
                  pl.BlockSpec((tk, d), lambda i, k: (k, 0)),
                  pl.BlockSpec((tm, d), row),
                  pl.BlockSpec((1, d), lambda i, k: (0, 0)),
                  pl.BlockSpec((1, d), lambda i, k: (0, 0))],
        out_specs=[pl.BlockSpec((tm, d), row), pl.BlockSpec((tm, d), row)],
        out_shape=[jax.ShapeDtypeStruct((m, d), F32), jax.ShapeDtypeStruct((m, d), BF16)],
        scratch_shapes=[pltpu.VMEM((tm, d), F32)],
        compiler_params=_params("parallel", "arbitrary"),
        name="mm_ln",
    )(a, w, resid, g.reshape(1, d), b.reshape(1, d))


def _mla_proj_kernel(x_ref, w_ref, gq_ref, gkv_ref, cos_ref, sa_ref, sb_ref,
                     cq_ref, ckvf_ref, ckvb_ref, krf_ref, krb_ref, acc, *, nk, nq, nkv):
    k = pl.program_id(1)

    @pl.when(k == 0)
    def _():
        acc[...] = jnp.zeros_like(acc)

    acc[...] += jnp.dot(x_ref[...], w_ref[...], preferred_element_type=F32)

    @pl.when(k == nk - 1)
    def _():
        cq_ref[...] = _rms_norm(acc[:, :nq], gq_ref[...]).astype(BF16)
        ckv = _rms_norm(acc[:, nq:nq + nkv], gkv_ref[...])
        ckvf_ref[...] = ckv
        ckvb_ref[...] = ckv.astype(BF16)
        kr = _rope128(acc[:, nq + nkv:], cos_ref[...], sa_ref[...], sb_ref[...])
        krf_ref[...] = kr
        krb_ref[...] = kr.astype(BF16)


def _mla_proj(xb, w_pad, g_q, g_kv, cos, sin_a, sin_b, *, tm=528, tk=512):
    m, kdim = xb.shape
    nq, nkv = g_q.shape[0], g_kv.shape[0]
    n = w_pad.shape[1]
    assert n == nq + nkv + LANE
    tm, tk = _tile(m, tm), _tile(kdim, tk, LANE)
    nk = kdim // tk
    row = lambda i, k: (i, 0)
    fixed = lambda i, k: (0, 0)
    return pl.pallas_call(
        functools.partial(_mla_proj_kernel, nk=nk, nq=nq, nkv=nkv),
        grid=(m // tm, nk),
        in_specs=[pl.BlockSpec((tm, tk), lambda i, k: (i, k)),
                  pl.BlockSpec((tk, n), lambda i, k: (k, 0)),
                  pl.BlockSpec((1, nq), fixed), pl.BlockSpec((1, nkv), fixed),
                  pl.BlockSpec((tm, LANE), row), pl.BlockSpec((tm, LANE), row),
                  pl.BlockSpec((tm, LANE), row)],
        out_specs=[pl.BlockSpec((tm, nq), row), pl.BlockSpec((tm, nkv), row),
                   pl.BlockSpec((tm, nkv), row), pl.BlockSpec((tm, LANE), row),
                   pl.BlockSpec((tm, LANE), row)],
        out_shape=[jax.ShapeDtypeStruct((m, nq), BF16), jax.ShapeDtypeStruct((m, nkv), F32),
                   jax.ShapeDtypeStruct((m, nkv), BF16), jax.ShapeDtypeStruct((m, LANE), F32),
                   jax.ShapeDtypeStruct((m, LANE), BF16)],
        scratch_shapes=[pltpu.VMEM((tm, n), F32)],
        compiler_params=_params("parallel", "arbitrary"),
        name="mla_proj",
    )(xb, w_pad, g_q.reshape(1, nq), g_kv.reshape(1, nkv), cos, sin_a, sin_b)


def _q_proj_kernel(c_ref, w_ref, cos_ref, sa_ref, sb_ref, o_ref, *, heads):
    y = jnp.dot(c_ref[...], w_ref[...], preferred_element_type=F32)
    for h in range(heads):
        lo = h * 2 * LANE
        o_ref[:, lo:lo + LANE] = y[:, lo:lo + LANE].astype(BF16)
        r = _rope128(y[:, lo + LANE:lo + 2 * LANE], cos_ref[...], sa_ref[...], sb_ref[...])
        o_ref[:, lo + LANE:lo + 2 * LANE] = r.astype(BF16)


def _q_proj(cq, w_arr, cos, sin_a, sin_b, *, tm=528, heads_per_step=2):
    m, kdim = cq.shape
    n = w_arr.shape[1]
    tm = _tile(m, tm)
    tn = heads_per_step * 2 * LANE
    row = lambda i, j: (i, 0)
    return pl.pallas_call(
        functools.partial(_q_proj_kernel, heads=heads_per_step),
        grid=(m // tm, n // tn),
        in_specs=[pl.BlockSpec((tm, kdim), row),
                  pl.BlockSpec((kdim, tn), lambda i, j: (0, j)),
                  pl.BlockSpec((tm, LANE), row), pl.BlockSpec((tm, LANE), row),
                  pl.BlockSpec((tm, LANE), row)],
        out_specs=pl.BlockSpec((tm, tn), lambda i, j: (i, j)),
        out_shape=jax.ShapeDtypeStruct((m, n), BF16),
        compiler_params=_params("parallel", "parallel"),
        name="q_proj",
    )(cq, w_arr, cos, sin_a, sin_b)


def _last_visible_block(qi, *, q_off, tq, tk, nk):
    last_q = q_off + (qi + 1) * tq - 1
    last_key = (last_q // CHUNK + 1) * CHUNK - 1
    return jnp.minimum(last_key // tk, nk - 1)


def _positions(qi, ki, *, q_off, tq, tk):
    qp = q_off + qi * tq + lax.broadcasted_iota(jnp.int32, (tq, tk), 0)
    kp = ki * tk + lax.broadcasted_iota(jnp.int32, (tq, tk), 1)
    return qp, kp


def _online_softmax_step(s, v, m_ref, l_ref, acc_ref):
    m_prev = m_ref[...]
    m_new = jnp.maximum(m_prev, jnp.max(s, axis=-1, keepdims=True))
    alpha = jnp.exp(m_prev - m_new)
    p = jnp.exp(s - m_new)
    l_ref[...] = alpha * l_ref[...] + jnp.sum(p, axis=-1, keepdims=True)
    acc_ref[...] = alpha * acc_ref[...] + jnp.dot(p.astype(BF16), v, preferred_element_type=F32)
    m_ref[...] = m_new


def _mla_attn_kernel(q_ref, kv_ref, kr_ref, o_ref, m_ref, l_ref, acc_ref, *, geo, scale):
    qi, ki = pl.program_id(2), pl.program_id(3)

    @pl.when(ki == 0)
    def _():
        m_ref[...] = jnp.full_like(m_ref, -jnp.inf)
        l_ref[...] = jnp.zeros_like(l_ref)
        acc_ref[...] = jnp.zeros_like(acc_ref)

    @pl.when(ki <= _last_visible_block(qi, **geo))
    def _():
        kv = kv_ref[...]
        kcat = jnp.concatenate([kv[:, :MLA_NOPE], kr_ref[...]], axis=1)
        s = lax.dot_general(q_ref[...], kcat, (((1,), (1,)), ((), ())),
                            preferred_element_type=F32) * scale
        qp, kp = _positions(qi, ki, q_off=geo["q_off"], tq=geo["tq"], tk=geo["tk"])
        s = jnp.where((kp // CHUNK) <= (qp // CHUNK), s, NEG)
        _online_softmax_step(s, kv[:, MLA_NOPE:], m_ref, l_ref, acc_ref)

    @pl.when(ki == geo["nk"] - 1)
    def _():
        o_ref[...] = (acc_ref[...] / l_ref[...]).astype(BF16)


def _mla_attention(q2d, kv2d, kr2d, *, batch, t_q, t_k, q_row0, q_off, tq, tk):
    nq, nk = t_q // tq, t_k // tk
    assert q_row0 % tq == 0
    geo = dict(q_off=q_off, tq=tq, tk=tk, nk=nk)
    qb0 = q_row0 // tq
    kblk = lambda b, qi, ki: b * nk + jnp.minimum(ki, _last_visible_block(qi, **geo))
    return pl.pallas_call(
        functools.partial(_mla_attn_kernel, geo=geo, scale=(MLA_NOPE + MLA_ROPE) ** -0.5),
        grid=(batch, MLA_HEADS, nq, nk),
        in_specs=[pl.BlockSpec((tq, 2 * LANE), lambda b, h, qi, ki: (qb0 + b * nq + qi, h)),
                  pl.BlockSpec((tk, MLA_NOPE + MLA_V), lambda b, h, qi, ki: (kblk(b, qi, ki), h)),
                  pl.BlockSpec((tk, LANE), lambda b, h, qi, ki: (kblk(b, qi, ki), 0))],
        out_specs=pl.BlockSpec((tq, MLA_V), lambda b, h, qi, ki: (b * nq + qi, h)),
        out_shape=jax.ShapeDtypeStruct((batch * t_q, MLA_HEADS * MLA_V), BF16),
        scratch_shapes=[pltpu.VMEM((tq, 1), F32), pltpu.VMEM((tq, 1), F32),
                        pltpu.VMEM((tq, MLA_V), F32)],
        compiler_params=_params("parallel", "parallel", "parallel", "arbitrary"),
        name="mla_attn",
    )(q2d, kv2d, kr2d)


def _diff_attn_kernel(slope_ref, lam_ref, gsub_ref, q_ref, k_ref, v_ref, o_ref,
                      m1, l1, a1, m2, l2, a2, *, geo, scale, lam_init):
    h, qi, ki = pl.program_id(1), pl.program_id(2), pl.program_id(3)

    @pl.when(ki == 0)
    def _():
        for m, l, a in ((m1, l1, a1), (m2, l2, a2)):
            m[...] = jnp.full_like(m, -jnp.inf)
            l[...] = jnp.zeros_like(l)
            a[...] = jnp.zeros_like(a)

    @pl.when(ki <= _last_visible_block(qi, **geo))
    def _():
        qp, kp = _positions(qi, ki, q_off=geo["q_off"], tq=geo["tq"], tk=geo["tk"])
        dist = jnp.abs(qp - kp).astype(F32)
        bias = jnp.where((kp // CHUNK) <= (qp // CHUNK), -slope_ref[h] * dist, NEG)
        q, k, v = q_ref[...], k_ref[...], v_ref[...]
        nt = (((1,), (1,)), ((), ()))
        s1 = lax.dot_general(q[:, :DIFF_QK], k[:, :DIFF_QK], nt, preferred_element_type=F32)
        _online_softmax_step(s1 * scale + bias, v, m1, l1, a1)
        s2 = lax.dot_general(q[:, DIFF_QK:], k[:, DIFF_QK:], nt, preferred_element_type=F32)
        _online_softmax_step(s2 * scale + bias, v, m2, l2, a2)

    @pl.when(ki == geo["nk"] - 1)
    def _():
        lam_v = lam_ref[...]
        lam = (jnp.exp(jnp.sum(lam_v[0:1] * lam_v[1:2], axis=-1, keepdims=True))
               - jnp.exp(jnp.sum(lam_v[2:3] * lam_v[3:4], axis=-1, keepdims=True)) + lam_init)
        o = a1[...] / l1[...] - lam * (a2[...] / l2[...])
        o_ref[...] = (_rms_norm(o, gsub_ref[...]) * (1.0 - lam_init)).astype(BF16)


def _diff_attention(q2d, k2d, v2d, q_lane0, k_lane0, v_lane0, slopes, lam_vecs, g_sub, *,
                    batch, t_q, t_k, q_row0, q_off, tq, tk, lam_init):
    nq, nk = t_q // tq, t_k // tk
    assert q_row0 % tq == 0
    geo = dict(q_off=q_off, tq=tq, tk=tk, nk=nk)
    qb0 = q_row0 // tq
    hd = 2 * DIFF_QK
    kblk = lambda b, qi, ki: b * nk + jnp.minimum(ki, _last_visible_block(qi, **geo))
    return pl.pallas_call(
        functools.partial(_diff_attn_kernel, geo=geo, scale=DIFF_QK ** -0.5, lam_init=lam_init),
        grid=(batch, DIFF_HEADS, nq, nk),
        in_specs=[pl.BlockSpec(memory_space=pltpu.SMEM),
                  pl.BlockSpec((4, DIFF_QK), lambda b, h, qi, ki: (0, 0)),
                  pl.BlockSpec((1, DIFF_V), lambda b, h, qi, ki: (0, 0)),
                  pl.BlockSpec((tq, hd), lambda b, h, qi, ki: (qb0 + b * nq + qi, q_lane0 + h)),
                  pl.BlockSpec((tk, hd), lambda b, h, qi, ki: (kblk(b, qi, ki), k_lane0 + h)),
                  pl.BlockSpec((tk, DIFF_V), lambda b, h, qi, ki: (kblk(b, qi, ki), v_lane0 + h))],
        out_specs=pl.BlockSpec((tq, DIFF_V), lambda b, h, qi, ki: (b * nq + qi, h)),
        out_shape=jax.ShapeDtypeStruct((batch * t_q, DIFF_HEADS * DIFF_V), BF16),
        scratch_shapes=[pltpu.VMEM((tq, 1), F32), pltpu.VMEM((tq, 1), F32),
                        pltpu.VMEM((tq, DIFF_V), F32),
                        pltpu.VMEM((tq, 1), F32), pltpu.VMEM((tq, 1), F32),
                        pltpu.VMEM((tq, DIFF_V), F32)],
        compiler_params=_params("parallel", "parallel", "parallel", "arbitrary"),
        name="diff_attn",
    )(slopes, lam_vecs, g_sub.reshape(1, DIFF_V), q2d, k2d, v2d)


def _top_rows(s, n_rows, k, payload=None):
    pos = lax.broadcasted_iota(jnp.int32, s.shape, 0).astype(F32)
    vals, idxs, pays = [], [], []
    for _ in range(k):
        m = jnp.max(s, axis=0, keepdims=True)
        sel = jnp.min(jnp.where(s == m, pos, float(n_rows)), axis=0, keepdims=True)
        hit = pos == sel
        vals.append(m)
        idxs.append(sel)
        if payload is not None:
            pays.append(jnp.max(jnp.where(hit, payload, -1.0), axis=0, keepdims=True))
        s = jnp.where(hit, -jnp.inf, s)
    cat = lambda xs: jnp.concatenate(xs, axis=0)
    return cat(vals), cat(idxs), (cat(pays) if payload is not None else None)


def _peer_topk_kernel(q_ref, sk_ref, e_ref, g_ref):
    half = PEER_DKEY // 2
    q = q_ref[...]
    nt = (((1,), (1,)), ((), ()))
    sv, si = [], []
    for c in range(2):
        s = lax.dot_general(sk_ref[c], q[:, c * half:(c + 1) * half], nt,
                            preferred_element_type=F32)
        v, i, _ = _top_rows(s, N_KEYS, PEER_TOPK)
        sv.append(v)
        si.append(i)
    cand = jnp.concatenate([sv[0][a:a + 1] + sv[1] for a in range(PEER_TOPK)], axis=0)
    cidx = jnp.concatenate([si[0][a:a + 1] * float(N_KEYS) + si[1] for a in range(PEER_TOPK)],
                           axis=0)
    fv, _, fe = _top_rows(cand, PEER_TOPK * PEER_TOPK, PEER_TOPK, payload=cidx)
    p = jnp.exp(fv - fv[0:1])
    g_ref[...] = p / jnp.sum(p, axis=0, keepdims=True)
    e_ref[...] = fe.astype(jnp.int32)


def _peer_topk(qb, sub_keys_b, *, tn=128):
    m = qb.shape[0]
    tn = _tile(m, tn, LANE)
    half = PEER_DKEY // 2
    out_spec = pl.BlockSpec((None, PEER_TOPK, tn), lambda i, h: (h, 0, i))
    return pl.pallas_call(
        _peer_topk_kernel,
        grid=(m // tn, PEER_HEADS),
        in_specs=[pl.BlockSpec((tn, PEER_DKEY), lambda i, h: (i, h)),
                  pl.BlockSpec((None, 2, N_KEYS, half), lambda i, h: (h, 0, 0, 0))],
        out_specs=[out_spec, out_spec],
        out_shape=[jax.ShapeDtypeStruct((PEER_HEADS, PEER_TOPK, m), jnp.int32),
                   jax.ShapeDtypeStruct((PEER_HEADS, PEER_TOPK, m), F32)],
        compiler_params=_params("parallel", "parallel"),
        name="peer_topk",
    )(qb, sub_keys_b)


def _peer_gate_kernel(e_ref, g_ref, o_ref, *, rows):
    npick = PEER_HEADS * PEER_TOPK
    key = lax.broadcasted_iota(jnp.int32, (N_KEYS, npick), 0)
    nt = (((1,), (1,)), ((), ()))

    def body(n, carry):
        e = e_ref[pl.ds(n, 1), :]
        g = g_ref[pl.ds(n, 1), :]
        first = jnp.where(key == e // N_KEYS, 1.0, 0.0).astype(BF16)
        second = jnp.where(key == e % N_KEYS, g, 0.0).astype(BF16)
        w = lax.dot_general(first, second, nt, preferred_element_type=F32)
        o_ref[n] = w.astype(BF16)
        return carry

    lax.fori_loop(0, rows, body, 0)


def _peer_gate(e_t, g_t, *, tb=64):
    m, npick = e_t.shape
    tb = _tile(m, tb, 8)
    return pl.pallas_call(
        functools.partial(_peer_gate_kernel, rows=tb),
        grid=(m // tb,),
        in_specs=[pl.BlockSpec((tb, npick), lambda i: (i, 0)),
                  pl.BlockSpec((tb, npick), lambda i: (i, 0))],
        out_specs=pl.BlockSpec((tb, N_KEYS, N_KEYS), lambda i: (i, 0, 0)),
        out_shape=jax.ShapeDtypeStruct((m, N_KEYS, N_KEYS), BF16),
        compiler_params=_params("parallel"),
        name="peer_gate",
    )(e_t, g_t)


def _gelu_tanh(x):
    c = np.sqrt(2 / np.pi).astype(np.float32)
    return x * (0.5 * (1.0 + jnp.tanh(c * (x + 0.044715 * (x * x * x)))))


def _peer_dense_kernel(xb_ref, xf_ref, wd_ref, ut_ref, v_ref, g_ref, b_ref,
                       of_ref, ob_ref, acc, *, ne):
    e = pl.program_id(1)

    @pl.when(e == 0)
    def _():
        acc[...] = jnp.zeros_like(acc)

    h = jnp.dot(xb_ref[...], ut_ref[...], preferred_element_type=F32)
    w = (wd_ref[...].astype(F32) * _gelu_tanh(h)).astype(BF16)
    acc[...] += jnp.dot(w, v_ref[...], preferred_element_type=F32)

    @pl.when(e == ne - 1)
    def _():
        y = _layer_norm(ALPHA * xf_ref[...] + acc[...], g_ref[...], b_ref[...])
        of_ref[...] = y
        ob_ref[...] = y.astype(BF16)


def _peer_dense(xb, xf, wd, u_t, v, g, b, *, tn=528, te=512):
    m, d = xb.shape
    n_exp = v.shape[0]
    tn, te = _tile(m, tn), _tile(n_exp, te, LANE)
    ne = n_exp // te
    row = lambda i, e: (i, 0)
    fixed = lambda i, e: (0, 0)
    return pl.pallas_call(
        functools.partial(_peer_dense_kernel, ne=ne),
        grid=(m // tn, ne),
        in_specs=[pl.BlockSpec((tn, d), row), pl.BlockSpec((tn, d), row),
                  pl.BlockSpec((tn, te), lambda i, e: (i, e)),
                  pl.BlockSpec((d, te), lambda i, e: (0, e)),
                  pl.BlockSpec((te, d), lambda i, e: (e, 0)),
                  pl.BlockSpec((1, d), fixed), pl.BlockSpec((1, d), fixed)],
        out_specs=[pl.BlockSpec((tn, d), row), pl.BlockSpec((tn, d), row)],
        out_shape=[jax.ShapeDtypeStruct((m, d), F32), jax.ShapeDtypeStruct((m, d), BF16)],
        scratch_shapes=[pltpu.VMEM((tn, d), F32)],
        compiler_params=_params("parallel", "arbitrary"),
        name="peer_dense",
    )(xb, xf, wd, u_t, v, g.reshape(1, d), b.reshape(1, d))


def _peer_layer(xf, xb, w_query, sub_keys, u_tab, v_tab, ln_g, ln_b):
    m = xf.shape[0]
    npick = PEER_HEADS * PEER_TOPK
    (qb,) = _mm(xb, w_query.astype(BF16), [BF16])
    eidx, gates = _peer_topk(qb, sub_keys.astype(BF16))
    e_t = eidx.reshape(npick, m).T
    g_t = gates.reshape(npick, m).T
    wd = _peer_gate(e_t, g_t).reshape(m, N_KEYS * N_KEYS)
    return _peer_dense(xb, xf, wd, u_tab.astype(BF16).T, v_tab.astype(BF16), ln_g, ln_b)


def _rope_tables(pos):
    half = MLA_ROPE // 2
    inv = ROPE_THETA ** (-jnp.arange(half, dtype=jnp.float32) / half)
    ang = pos.astype(jnp.float32)[:, None] * inv
    cos, sin = jnp.cos(ang), jnp.sin(ang)
    z = jnp.zeros_like(cos)
    return (jnp.concatenate([cos, cos, z, z], -1), jnp.concatenate([-sin, z, z, z], -1),
            jnp.concatenate([z, sin, z, z], -1))


def kernel(x_prompt, x_sample, cache_mla_ckv, cache_mla_krope, cache_diff_k, cache_diff_v,
           mla_w_dqkv, mla_g_q, mla_w_uq, mla_g_kv, mla_w_ukv, mla_w_o,
           diff_w_qkv, diff_lam_q1, diff_lam_k1, diff_lam_q2, diff_lam_k2, diff_g_sub, diff_w_o,
           peer_w_query, peer_sub_keys, peer_u, peer_v,
           ln_mix_g, ln_mix_b, ln_ffn_g, ln_ffn_b):
    bp, tp, d = x_prompt.shape
    bs, ts, _ = x_sample.shape
    past = cache_mla_ckv.shape[2]
    mp, ms = bp * tp, bs * ts
    tks = past + ts
    tq_p = _tile(tp, 256, CHUNK)

    xf = jnp.concatenate([x_prompt.reshape(mp, d), x_sample.reshape(ms, d)], 0)
    xb = xf.astype(BF16)
    pos = jnp.concatenate([jnp.tile(jnp.arange(tp), bp), jnp.tile(past + jnp.arange(ts), bs)])
    cos, sin_a, sin_b = _rope_tables(pos)

    j = 0
    w_dqkv = jnp.pad(mla_w_dqkv[j], ((0, 0), (0, LANE - MLA_ROPE))).astype(BF16)
    cq, ckv_f, ckv_b, kr_f, kr_b = _mla_proj(xb, w_dqkv, mla_g_q[j], mla_g_kv[j], cos, sin_a, sin_b)
    hq = MLA_NOPE + MLA_ROPE
    w_uq = jnp.pad(mla_w_uq[j].reshape(MLA_Q_LORA, MLA_HEADS, hq),
                   ((0, 0), (0, 0), (0, 2 * LANE - hq))).reshape(MLA_Q_LORA, MLA_HEADS * 2 * LANE)
    q = _q_proj(cq, w_uq.astype(BF16), cos, sin_a, sin_b)
    w_ukv = mla_w_ukv[j].astype(BF16)
    (kv_p,) = _mm(ckv_b[:mp], w_ukv, [BF16])
    o_p = _mla_attention(q, kv_p, kr_b, batch=bp, t_q=tp, t_k=tp, q_row0=0, q_off=0,
                         tq=tq_p, tk=tq_p)
    ckv_cat = jnp.concatenate([cache_mla_ckv[j].astype(BF16),
                               ckv_b[mp:].reshape(bs, ts, MLA_KV_LORA)], 1).reshape(bs * tks, -1)
    kr_cache = jnp.pad(cache_mla_krope[j], ((0, 0), (0, 0), (0, LANE - MLA_ROPE))).astype(BF16)
    kr_cat = jnp.concatenate([kr_cache, kr_b[mp:].reshape(bs, ts, LANE)], 1).reshape(bs * tks, LANE)
    (kv_s,) = _mm(ckv_cat, w_ukv, [BF16])
    o_s = _mla_attention(q, kv_s, kr_cat, batch=bs, t_q=ts, t_k=tks, q_row0=mp, q_off=past,
                         tq=ts, tk=tks)
    o = jnp.concatenate([o_p, o_s], 0)
    xf, xb = _mm_ln(o, mla_w_o[j].astype(BF16), xf, ln_mix_g[0], ln_mix_b[0])
    xf, xb = _peer_layer(xf, xb, peer_w_query[0], peer_sub_keys[0], peer_u[0], peer_v[0],
                         ln_ffn_g[0], ln_ffn_b[0])

    i = 1
    lam_init = 0.8 - 0.6 * math.exp(-0.3 * i)
    nqk = DIFF_HEADS * 2 * DIFF_QK
    w_qkv = diff_w_qkv[j].astype(BF16)
    (dq,) = _mm(xb, w_qkv[:, :nqk], [BF16])
    dk_f, dk_b = _mm(xb, w_qkv[:, nqk:2 * nqk], [F32, BF16])
    dv_f, dv_b = _mm(xb, w_qkv[:, 2 * nqk:], [F32, BF16])
    slopes = 2.0 ** (-8.0 * jnp.arange(1, DIFF_HEADS + 1, dtype=jnp.float32) / DIFF_HEADS)
    lam_vecs = jnp.stack([diff_lam_q1[j], diff_lam_k1[j], diff_lam_q2[j], diff_lam_k2[j]])
    diff_args = (slopes, lam_vecs, diff_g_sub[j])
    o_p = _diff_attention(dq, dk_b, dv_b, 0, 0, 0, *diff_args, batch=bp, t_q=tp, t_k=tp,
                          q_row0=0, q_off=0, tq=tq_p, tk=tq_p, lam_init=lam_init)
    k_cat = jnp.concatenate([cache_diff_k[j].reshape(bs, past, nqk).astype(BF16),
                             dk_b[mp:].reshape(bs, ts, nqk)], 1).reshape(bs * tks, nqk)
    v_cat = jnp.concatenate([cache_diff_v[j].reshape(bs, past, -1).astype(BF16),
                             dv_b[mp:].reshape(bs, ts, -1)], 1).reshape(bs * tks, -1)
    o_s = _diff_attention(dq, k_cat, v_cat, 0, 0, 0, *diff_args, batch=bs, t_q=ts, t_k=tks,
                          q_row0=mp, q_off=past, tq=ts, tk=tks, lam_init=lam_init)
    o = jnp.concatenate([o_p, o_s], 0)
    xf, xb = _mm_ln(o, diff_w_o[j].astype(BF16), xf, ln_mix_g[1], ln_mix_b[1])
    xf, xb = _peer_layer(xf, xb, peer_w_query[1], peer_sub_keys[1], peer_u[1], peer_v[1],
                         ln_ffn_g[1], ln_ffn_b[1])

    kr_f = kr_f[:, :MLA_ROPE]
    return (xf[:mp].reshape(bp, tp, d), xf[mp:].reshape(bs, ts, d),
            ckv_f[:mp].reshape(1, bp, tp, -1), kr_f[:mp].reshape(1, bp, tp, -1),
            dk_f[:mp].reshape(1, bp, tp, DIFF_HEADS, -1), dv_f[:mp].reshape(1, bp, tp, DIFF_HEADS, -1),
            ckv_f[mp:].reshape(1, bs, ts, -1), kr_f[mp:].reshape(1, bs, ts, -1),
            dk_f[mp:].reshape(1, bs, ts, DIFF_HEADS, -1), dv_f[mp:].reshape(1, bs, ts, DIFF_HEADS, -1))
```

```python
import functools
import math

import jax
import jax.numpy as jnp
import numpy as np
from jax import lax
from jax.experimental import pallas as pl
from jax.experimental.pallas import tpu as pltpu

F32 = jnp.float32
BF16 = jnp.bfloat16

DEPTH = 2
CHUNK = 64
ALPHA = (2 * DEPTH) ** 0.25
LN_EPS = 1e-5
RMS_EPS = 1e-6
NEG = -1e30
ROPE_THETA = 10000.0
MLA_HEADS = 16
MLA_Q_LORA = 512
MLA_KV_LORA = 512
MLA_NOPE = 128
MLA_ROPE = 64
MLA_V = 128
DIFF_HEADS = 8
DIFF_QK = 128
DIFF_V = 256
PEER_HEADS = 8
PEER_TOPK = 16
N_KEYS = 128
PEER_DKEY = 256

LANE = 128
BF16_SUBLANE = 16
VMEM_LIMIT = 56 << 20


def _tile(n, target, mult=BF16_SUBLANE):
    for t in range(min(n, target), 0, -1):
        if n % t == 0 and t % mult == 0:
            return t
    raise ValueError(f"no tile for {n} (target {target}, multiple of {mult})")


def _params(*sem):
    return pltpu.CompilerParams(dimension_semantics=sem, vmem_limit_bytes=VMEM_LIMIT)


def _layer_norm(y, g, b):
    mu = jnp.mean(y, axis=-1, keepdims=True)
    d = y - mu
    var = jnp.mean(d * d, axis=-1, keepdims=True)
    return d * lax.rsqrt(var + LN_EPS) * g + b


def _rms_norm(y, g):
    return y * lax.rsqrt(jnp.mean(y * y, axis=-1, keepdims=True) + RMS_EPS) * g


def _rope128(r, cos, sin_a, sin_b):
    return r * cos + pltpu.roll(r, 96, 1) * sin_a + pltpu.roll(r, 32, 1) * sin_b


def _mm_kernel(x_ref, w_ref, *refs, nk, n_out):
    outs, acc = refs[:n_out], refs[n_out]
    k = pl.program_id(2)

    @pl.when(k == 0)
    def _():
        acc[...] = jnp.zeros_like(acc)

    acc[...] += jnp.dot(x_ref[...], w_ref[...], preferred_element_type=F32)

    @pl.when(k == nk - 1)
    def _():
        for o in outs:
            o[...] = acc[...].astype(o.dtype)


def _mm(x, w, out_dtypes, *, tm=528, tn=1024, tk=512):
    m, kdim = x.shape
    n = w.shape[1]
    tm, tn, tk = _tile(m, tm), _tile(n, tn, LANE), _tile(kdim, tk, LANE)
    nk = kdim // tk
    outs = pl.pallas_call(
        functools.partial(_mm_kernel, nk=nk, n_out=len(out_dtypes)),
        grid=(m // tm, n // tn, nk),
        in_specs=[pl.BlockSpec((tm, tk), lambda i, j, k: (i, k)),
                  pl.BlockSpec((tk, tn), lambda i, j, k: (k, j))],
        out_specs=[pl.BlockSpec((tm, tn), lambda i, j, k: (i, j)) for _ in out_dtypes],
        out_shape=[jax.ShapeDtypeStruct((m, n), dt) for dt in out_dtypes],
        scratch_shapes=[pltpu.VMEM((tm, tn), F32)],
        compiler_params=_params("parallel", "parallel", "arbitrary"),
        name="mm",
    )(x, w)
    return outs


def _mm_ln_kernel(a_ref, w_ref, r_ref, g_ref, b_ref, of_ref, ob_ref, acc, *, nk):
    k = pl.program_id(1)

    @pl.when(k == 0)
    def _():
        acc[...] = jnp.zeros_like(acc)

    acc[...] += jnp.dot(a_ref[...], w_ref[...], preferred_element_type=F32)

    @pl.when(k == nk - 1)
    def _():
        y = _layer_norm(ALPHA * r_ref[...] + acc[...], g_ref[...], b_ref[...])
        of_ref[...] = y
        ob_ref[...] = y.astype(BF16)


def _mm_ln(a, w, resid, g, b, *, tm=528, tk=512):
    m, kdim = a.shape
    d = w.shape[1]
    tm, tk = _tile(m, tm), _tile(kdim, tk, LANE)
    nk = kdim // tk
    row = lambda i, k: (i, 0)
    return pl.pallas_call(
        functools.partial(_mm_ln_kernel, nk=nk),
        grid=(m // tm, nk),
        in_specs=[pl.BlockSpec((tm, tk), lambda i, k: (i, k)),
                  pl.BlockSpec((tk, d), lambda i, k: (k, 0)),
                  pl.BlockSpec((tm, d), row),
                  pl.BlockSpec((1, d), lambda i, k: (0, 0)),
                  pl.BlockSpec((1, d), lambda i, k: (0, 0))],
        out_specs=[pl.BlockSpec((tm, d), row), pl.BlockSpec((tm, d), row)],
        out_shape=[jax.ShapeDtypeStruct((m, d), F32), jax.ShapeDtypeStruct((m, d), BF16)],
        scratch_shapes=[pltpu.VMEM((tm, d), F32)],
        compiler_params=_params("parallel", "arbitrary"),
        name="mm_ln",
    )(a, w, resid, g.reshape(1, d), b.reshape(1, d))


def _mla_proj_kernel(x_ref, w_ref, gq_ref, gkv_ref, cos_ref, sa_ref, sb_ref,
                     cq_ref, ckvf_ref, ckvb_ref, krf_ref, krb_ref, acc, *, nk, nq, nkv):
    k = pl.program_id(1)

    @pl.when(k == 0)
    def _():
        acc[...] = jnp.zeros_like(acc)

    acc[...] += jnp.dot(x_ref[...], w_ref[...], preferred_element_type=F32)

    @pl.when(k == nk - 1)
    def _():
        cq_ref[...] = _rms_norm(acc[:, :nq], gq_ref[...]).astype(BF16)
        ckv = _rms_norm(acc[:, nq:nq + nkv], gkv_ref[...])
        ckvf_ref[...] = ckv
        ckvb_ref[...] = ckv.astype(BF16)
        kr = _rope128(acc[:, nq + nkv:], cos_ref[...], sa_ref[...], sb_ref[...])
        krf_ref[...] = kr
        krb_ref[...] = kr.astype(BF16)


def _mla_proj(xb, w_pad, g_q, g_kv, cos, sin_a, sin_b, *, tm=528, tk=512):
    m, kdim = xb.shape
    nq, nkv = g_q.shape[0], g_kv.shape[0]
    n = w_pad.shape[1]
    assert n == nq + nkv + LANE
    tm, tk = _tile(m, tm), _tile(kdim, tk, LANE)
    nk = kdim // tk
    row = lambda i, k: (i, 0)
    fixed = lambda i, k: (0, 0)
    return pl.pallas_call(
        functools.partial(_mla_proj_kernel, nk=nk, nq=nq, nkv=nkv),
        grid=(m // tm, nk),
        in_specs=[pl.BlockSpec((tm, tk), lambda i, k: (i, k)),
                  pl.BlockSpec((tk, n), lambda i, k: (k, 0)),
                  pl.BlockSpec((1, nq), fixed), pl.BlockSpec((1, nkv), fixed),
                  pl.BlockSpec((tm, LANE), row), pl.BlockSpec((tm, LANE), row),
                  pl.BlockSpec((tm, LANE), row)],
        out_specs=[pl.BlockSpec((tm, nq), row), pl.BlockSpec((tm, nkv), row),
                   pl.BlockSpec((tm, nkv), row), pl.BlockSpec((tm, LANE), row),
                   pl.BlockSpec((tm, LANE), row)],
        out_shape=[jax.ShapeDtypeStruct((m, nq), BF16), jax.ShapeDtypeStruct((m, nkv), F32),
                   jax.ShapeDtypeStruct((m, nkv), BF16), jax.ShapeDtypeStruct((m, LANE), F32),
                   jax.ShapeDtypeStruct((m, LANE), BF16)],
        scratch_shapes=[pltpu.VMEM((tm, n), F32)],
        compiler_params=_params("parallel", "arbitrary"),
        name="mla_proj",
    )(xb, w_pad, g_q.reshape(1, nq), g_kv.reshape(1, nkv), cos, sin_a, sin_b)


def _q_proj_kernel(c_ref, w_ref, cos_ref, sa_ref, sb_ref, o_ref, *, heads):
    y = jnp.dot(c_ref[...], w_ref[...], preferred_element_type=F32)
    for h in range(heads):
        lo = h * 2 * LANE
        o_ref[:, lo:lo + LANE] = y[:, lo:lo + LANE].astype(BF16)
        r = _rope128(y[:, lo + LANE:lo + 2 * LANE], cos_ref[...], sa_ref[...], sb_ref[...])
        o_ref[:, lo + LANE:lo + 2 * LANE] = r.astype(BF16)


def _q_proj(cq, w_arr, cos, sin_a, sin_b, *, tm=528, heads_per_step=2):
    m, kdim = cq.shape
    n = w_arr.shape[1]
    tm = _tile(m, tm)
    tn = heads_per_step * 2 * LANE
    row = lambda i, j: (i, 0)
    return pl.pallas_call(
        functools.partial(_q_proj_kernel, heads=heads_per_step),
        grid=(m // tm, n // tn),
        in_specs=[pl.BlockSpec((tm, kdim), row),
                  pl.BlockSpec((kdim, tn), lambda i, j: (0, j)),
                  pl.BlockSpec((tm, LANE), row), pl.BlockSpec((tm, LANE), row),
                  pl.BlockSpec((tm, LANE), row)],
        out_specs=pl.BlockSpec((tm, tn), lambda i, j: (i, j)),
        out_shape=jax.ShapeDtypeStruct((m, n), BF16),
        compiler_params=_params("parallel", "parallel"),
        name="q_proj",
    )(cq, w_arr, cos, sin_a, sin_b)


def _visible_blocks(qi, *, q_off, tq, tk, nk):
    q_first = q_off + qi * tq
    q_last = q_first + tq - 1
    n_full = jnp.minimum(((q_first // CHUNK + 1) * CHUNK) // tk, nk)
    n_vis = jnp.minimum(((q_last // CHUNK + 1) * CHUNK + tk - 1) // tk, nk)
    return n_full, n_vis


def _positions(qi, ki, *, q_off, tq, tk):
    qp = q_off + qi * tq + lax.broadcasted_iota(jnp.int32, (tq, tk), 0)
    kp = ki * tk + lax.broadcasted_iota(jnp.int32, (tq, tk), 1)
    return qp, kp


def _for_blocks(lo, hi, step):
    lax.fori_loop(lo, hi, lambda ki, carry: (step(ki), carry)[1], 0)


def _online_softmax_step(s, v, m_ref, l_ref, acc_ref):
    m_prev = m_ref[...]
    m_new = jnp.maximum(m_prev, jnp.max(s, axis=-1, keepdims=True))
    alpha = jnp.exp(m_prev - m_new)
    p = jnp.exp(s - m_new)
    l_ref[...] = alpha * l_ref[...] + jnp.sum(p, axis=-1, keepdims=True)
    acc_ref[...] = alpha * acc_ref[...] + jnp.dot(p.astype(BF16), v, preferred_element_type=F32)
    m_ref[...] = m_new


def _key_mask(qi, ki, geo):
    qp, kp = _positions(qi, ki, q_off=geo["q_off"], tq=geo["tq"], tk=geo["tk"])
    k_chunk = kp // CHUNK
    if geo["t_valid"] < geo["nk"] * geo["tk"]:
        k_chunk = jnp.where(kp < geo["t_valid"], k_chunk, jnp.iinfo(jnp.int32).max)
    return k_chunk <= (qp // CHUNK), qp, kp


def _mla_attn_kernel(q_ref, knt_ref, krt_ref, kv_ref, o_ref, kcat_t, m_ref, l_ref, acc_ref,
                     *, geo, scale):
    qi = pl.program_id(2)
    tk = geo["tk"]

    @pl.when(qi == 0)
    def _():
        kcat_t[:MLA_NOPE, :] = knt_ref[...]
        kcat_t[MLA_NOPE:, :] = krt_ref[...]

    m_ref[...] = jnp.full_like(m_ref, -jnp.inf)
    l_ref[...] = jnp.zeros_like(l_ref)
    acc_ref[...] = jnp.zeros_like(acc_ref)
    q = q_ref[...]

    def step(ki, masked):
        r0 = pl.multiple_of(ki * tk, tk)
        s = jnp.dot(q, kcat_t[:, pl.ds(r0, tk)], preferred_element_type=F32) * scale
        if masked:
            s = jnp.where(_key_mask(qi, ki, geo)[0], s, NEG)
        _online_softmax_step(s, kv_ref[pl.ds(r0, tk), MLA_NOPE:], m_ref, l_ref, acc_ref)

    n_full, n_vis = _visible_blocks(qi, q_off=geo["q_off"], tq=geo["tq"], tk=tk, nk=geo["nk"])
    _for_blocks(0, n_full, functools.partial(step, masked=False))
    _for_blocks(n_full, n_vis, functools.partial(step, masked=True))
    o_ref[...] = (acc_ref[...] / l_ref[...]).astype(BF16)


def _mla_attention(q2d, knt, krt, kv2d, *, batch, t_q, t_k, t_valid, q_row0, q_off, tq, tk):
    nq, nk = t_q // tq, t_k // tk
    assert q_row0 % tq == 0 and t_k % tk == 0 and tk % LANE == 0
    geo = dict(q_off=q_off, tq=tq, tk=tk, nk=nk, t_valid=t_valid)
    qb0 = q_row0 // tq
    return pl.pallas_call(
        functools.partial(_mla_attn_kernel, geo=geo, scale=(MLA_NOPE + MLA_ROPE) ** -0.5),
        grid=(batch, MLA_HEADS, nq),
        in_specs=[pl.BlockSpec((tq, 2 * LANE), lambda b, h, qi: (qb0 + b * nq + qi, h)),
                  pl.BlockSpec((MLA_NOPE, t_k), lambda b, h, qi: (h, b)),
                  pl.BlockSpec((LANE, t_k), lambda b, h, qi: (0, b)),
                  pl.BlockSpec((t_k, MLA_NOPE + MLA_V), lambda b, h, qi: (b, h))],
        out_specs=pl.BlockSpec((tq, MLA_V), lambda b, h, qi: (b * nq + qi, h)),
        out_shape=jax.ShapeDtypeStruct((batch * t_q, MLA_HEADS * MLA_V), BF16),
        scratch_shapes=[pltpu.VMEM((2 * LANE, t_k), BF16),
                        pltpu.VMEM((tq, 1), F32), pltpu.VMEM((tq, 1), F32),
                        pltpu.VMEM((tq, MLA_V), F32)],
        compiler_params=_params("parallel", "parallel", "arbitrary"),
        name="mla_attn",
    )(q2d, knt, krt, kv2d)


def _diff_attn_kernel(slope_ref, lam_ref, gsub_ref, q_ref, kt_ref, v_ref, o_ref,
                      m1, l1, a1, m2, l2, a2, *, geo, scale, lam_init):
    h, qi = pl.program_id(1), pl.program_id(2)
    tk = geo["tk"]
    for m, l, a in ((m1, l1, a1), (m2, l2, a2)):
        m[...] = jnp.full_like(m, -jnp.inf)
        l[...] = jnp.zeros_like(l)
        a[...] = jnp.zeros_like(a)
    q = q_ref[...]
    slope = slope_ref[h]

    def step(ki, masked):
        r0 = pl.multiple_of(ki * tk, tk)
        if masked:
            mask, qp, kp = _key_mask(qi, ki, geo)
            bias = jnp.where(mask, -slope * jnp.abs(qp - kp).astype(F32), NEG)
        else:
            qp, kp = _positions(qi, ki, q_off=geo["q_off"], tq=geo["tq"], tk=tk)
            bias = -slope * jnp.abs(qp - kp).astype(F32)
        v = v_ref[pl.ds(r0, tk), :]
        s1 = jnp.dot(q[:, :DIFF_QK], kt_ref[:DIFF_QK, pl.ds(r0, tk)], preferred_element_type=F32)
        _online_softmax_step(s1 * scale + bias, v, m1, l1, a1)
        s2 = jnp.dot(q[:, DIFF_QK:], kt_ref[DIFF_QK:, pl.ds(r0, tk)], preferred_element_type=F32)
        _online_softmax_step(s2 * scale + bias, v, m2, l2, a2)

    n_full, n_vis = _visible_blocks(qi, q_off=geo["q_off"], tq=geo["tq"], tk=tk, nk=geo["nk"])
    _for_blocks(0, n_full, functools.partial(step, masked=False))
    _for_blocks(n_full, n_vis, functools.partial(step, masked=True))

    lam_v = lam_ref[...]
    lam = (jnp.exp(jnp.sum(lam_v[0:1] * lam_v[1:2], axis=-1, keepdims=True))
           - jnp.exp(jnp.sum(lam_v[2:3] * lam_v[3:4], axis=-1, keepdims=True)) + lam_init)
    o = a1[...] / l1[...] - lam * (a2[...] / l2[...])
    o_ref[...] = (_rms_norm(o, gsub_ref[...]) * (1.0 - lam_init)).astype(BF16)


def _diff_attention(q2d, kt, v2d, slopes, lam_vecs, g_sub, *,
                    batch, t_q, t_k, t_valid, q_row0, q_off, tq, tk, lam_init):
    nq, nk = t_q // tq, t_k // tk
    assert q_row0 % tq == 0 and t_k % tk == 0 and tk % LANE == 0
    geo = dict(q_off=q_off, tq=tq, tk=tk, nk=nk, t_valid=t_valid)
    qb0 = q_row0 // tq
    hd = 2 * DIFF_QK
    return pl.pallas_call(
        functools.partial(_diff_attn_kernel, geo=geo, scale=DIFF_QK ** -0.5, lam_init=lam_init),
        grid=(batch, DIFF_HEADS, nq),
        in_specs=[pl.BlockSpec(memory_space=pltpu.SMEM),
                  pl.BlockSpec((4, DIFF_QK), lambda b, h, qi: (0, 0)),
                  pl.BlockSpec((1, DIFF_V), lambda b, h, qi: (0, 0)),
                  pl.BlockSpec((tq, hd), lambda b, h, qi: (qb0 + b * nq + qi, h)),
                  pl.BlockSpec((hd, t_k), lambda b, h, qi: (h, b)),
                  pl.BlockSpec((t_k, DIFF_V), lambda b, h, qi: (b, h))],
        out_specs=pl.BlockSpec((tq, DIFF_V), lambda b, h, qi: (b * nq + qi, h)),
        out_shape=jax.ShapeDtypeStruct((batch * t_q, DIFF_HEADS * DIFF_V), BF16),
        scratch_shapes=[pltpu.VMEM((tq, 1), F32), pltpu.VMEM((tq, 1), F32),
                        pltpu.VMEM((tq, DIFF_V), F32),
                        pltpu.VMEM((tq, 1), F32), pltpu.VMEM((tq, 1), F32),
                        pltpu.VMEM((tq, DIFF_V), F32)],
        compiler_params=_params("parallel", "parallel", "parallel"),
        name="diff_attn",
    )(slopes, lam_vecs, g_sub.reshape(1, DIFF_V), q2d, kt, v2d)


def _top_rows(s, n_rows, k, payload=None):
    pos = lax.broadcasted_iota(jnp.int32, s.shape, 0).astype(F32)
    vals, idxs, pays = [], [], []
    for _ in range(k):
        m = jnp.max(s, axis=0, keepdims=True)
        sel = jnp.min(jnp.where(s == m, pos, float(n_rows)), axis=0, keepdims=True)
        hit = pos == sel
        vals.append(m)
        idxs.append(sel)
        if payload is not None:
            pays.append(jnp.max(jnp.where(hit, payload, -1.0), axis=0, keepdims=True))
        s = jnp.where(hit, -jnp.inf, s)
    cat = lambda xs: jnp.concatenate(xs, axis=0)
    return cat(vals), cat(idxs), (cat(pays) if payload is not None else None)


def _peer_topk_kernel(q_ref, sk_ref, e_ref, g_ref):
    half = PEER_DKEY // 2
    q = q_ref[...]
    nt = (((1,), (1,)), ((), ()))
    sv, si = [], []
    for c in range(2):
        s = lax.dot_general(sk_ref[c], q[:, c * half:(c + 1) * half], nt,
                            preferred_element_type=F32)
        v, i, _ = _top_rows(s, N_KEYS, PEER_TOPK)
        sv.append(v)
        si.append(i)
    width = [PEER_TOPK // (a + 1) for a in range(PEER_TOPK)]
    n_cand = sum(width)
    pad = -n_cand % 8
    cand = jnp.concatenate([sv[0][a:a + 1] + sv[1][:width[a]] for a in range(PEER_TOPK)]
                           + [jnp.full((pad, q.shape[0]), -jnp.inf, F32)], axis=0)
    cidx = jnp.concatenate([si[0][a:a + 1] * float(N_KEYS) + si[1][:width[a]]
                            for a in range(PEER_TOPK)]
                           + [jnp.full((pad, q.shape[0]), -1.0, F32)], axis=0)
    fv, _, fe = _top_rows(cand, n_cand + pad, PEER_TOPK, payload=cidx)
    p = jnp.exp(fv - fv[0:1])
    g_ref[...] = p / jnp.sum(p, axis=0, keepdims=True)
    e_ref[...] = fe.astype(jnp.int32)


def _peer_topk(qb, sub_keys_b, *, tn=128):
    m = qb.shape[0]
    tn = _tile(m, tn, LANE)
    half = PEER_DKEY // 2
    out_spec = pl.BlockSpec((None, PEER_TOPK, tn), lambda i, h: (h, 0, i))
    return pl.pallas_call(
        _peer_topk_kernel,
        grid=(m // tn, PEER_HEADS),
        in_specs=[pl.BlockSpec((tn, PEER_DKEY), lambda i, h: (i, h)),
                  pl.BlockSpec((None, 2, N_KEYS, half), lambda i, h: (h, 0, 0, 0))],
        out_specs=[out_spec, out_spec],
        out_shape=[jax.ShapeDtypeStruct((PEER_HEADS, PEER_TOPK, m), jnp.int32),
                   jax.ShapeDtypeStruct((PEER_HEADS, PEER_TOPK, m), F32)],
        compiler_params=_params("parallel", "parallel"),
        name="peer_topk",
    )(qb, sub_keys_b)


GATE_ROWS = 8
GATE_UNROLL = 8


def _peer_gate_kernel(e_ref, g_ref, o_ref, *, rows):
    npick = PEER_HEADS * PEER_TOPK
    key = lax.broadcasted_iota(jnp.int32, (N_KEYS, npick), 0)
    nt = (((1,), (1,)), ((), ()))
    shift = N_KEYS.bit_length() - 1

    def body(blk, carry):
        for u in range(GATE_UNROLL):
            n = blk * GATE_UNROLL + u
            e = e_ref[pl.ds(n, 1), :]
            g = g_ref[pl.ds(n, 1), :]
            first = jnp.where(key == (e >> shift), 1.0, 0.0).astype(BF16)
            second = jnp.where(key == (e & (N_KEYS - 1)), g, 0.0).astype(BF16)
            w = lax.dot_general(first, second, nt, preferred_element_type=F32)
            o_ref[:, n] = w.reshape(N_KEYS // GATE_ROWS, GATE_ROWS, N_KEYS)
        return carry

    lax.fori_loop(0, rows // GATE_UNROLL, body, 0)


def _peer_gate(e_t, g_t, *, tb=64):
    m, npick = e_t.shape
    assert N_KEYS & (N_KEYS - 1) == 0
    tb = _tile(m, tb, GATE_UNROLL)
    nib = N_KEYS // GATE_ROWS
    return pl.pallas_call(
        functools.partial(_peer_gate_kernel, rows=tb),
        grid=(m // tb,),
        in_specs=[pl.BlockSpec((tb, npick), lambda i: (i, 0)),
                  pl.BlockSpec((tb, npick), lambda i: (i, 0))],
        out_specs=pl.BlockSpec((nib, tb, GATE_ROWS, N_KEYS), lambda i: (0, i, 0, 0)),
        out_shape=jax.ShapeDtypeStruct((nib, m, GATE_ROWS, N_KEYS), F32),
        compiler_params=_params("parallel"),
        name="peer_gate",
    )(e_t, g_t)


def _gelu_tanh(x):
    c = np.sqrt(2 / np.pi).astype(np.float32)
    return x * (0.5 * (1.0 + jnp.tanh(c * (x + 0.044715 * (x * x * x)))))


def _peer_dense_kernel(xb_ref, xf_ref, wd_ref, ut_ref, v_ref, g_ref, b_ref,
                       of_ref, ob_ref, w_ref, *, ne, tn):
    e = pl.program_id(1)

    @pl.when(e == 0)
    def _():
        of_ref[...] = jnp.zeros_like(of_ref)

    h = jnp.dot(xb_ref[...], ut_ref[...], preferred_element_type=F32)
    for r in range(GATE_ROWS):
        lanes = slice(r * N_KEYS, (r + 1) * N_KEYS)
        gate = wd_ref[pl.ds(r, tn, stride=GATE_ROWS), :]
        w_ref[:, lanes] = (gate * _gelu_tanh(h[:, lanes])).astype(BF16)
    of_ref[...] += jnp.dot(w_ref[...], v_ref[...], preferred_element_type=F32)

    @pl.when(e == ne - 1)
    def _():
        y = _layer_norm(ALPHA * xf_ref[...] + of_ref[...], g_ref[...], b_ref[...])
        of_ref[...] = y
        ob_ref[...] = y.astype(BF16)


def _peer_dense(xb, xf, wd, u_t, v, g, b, *, tn=528):
    m, d = xb.shape
    ne = wd.shape[0]
    te = GATE_ROWS * N_KEYS
    assert v.shape[0] == ne * te
    tn = _tile(m, tn)
    row = lambda i, e: (i, 0)
    fixed = lambda i, e: (0, 0)
    return pl.pallas_call(
        functools.partial(_peer_dense_kernel, ne=ne, tn=tn),
        grid=(m // tn, ne),
        in_specs=[pl.BlockSpec((tn, d), row), pl.BlockSpec((tn, d), row),
                  pl.BlockSpec((None, tn * GATE_ROWS, N_KEYS), lambda i, e: (e, i, 0)),
                  pl.BlockSpec((d, te), lambda i, e: (0, e)),
                  pl.BlockSpec((te, d), lambda i, e: (e, 0)),
                  pl.BlockSpec((1, d), fixed), pl.BlockSpec((1, d), fixed)],
        out_specs=[pl.BlockSpec((tn, d), row), pl.BlockSpec((tn, d), row)],
        out_shape=[jax.ShapeDtypeStruct((m, d), F32), jax.ShapeDtypeStruct((m, d), BF16)],
        scratch_shapes=[pltpu.VMEM((tn, te), BF16)],
        compiler_params=_params("parallel", "arbitrary"),
        name="peer_dense",
    )(xb, xf, wd, u_t, v, g.reshape(1, d), b.reshape(1, d))


def _peer_layer(xf, xb, w_query, sub_keys, u_tab, v_tab, ln_g, ln_b):
    m = xf.shape[0]
    npick = PEER_HEADS * PEER_TOPK
    (qb,) = _mm(xb, w_query.astype(BF16), [BF16])
    eidx, gates = _peer_topk(qb, sub_keys.astype(BF16))
    e_t = eidx.reshape(npick, m).T
    g_t = gates.reshape(npick, m).T
    wd = _peer_gate(e_t, g_t).reshape(N_KEYS // GATE_ROWS, m * GATE_ROWS, N_KEYS)
    return _peer_dense(xb, xf, wd, u_tab.astype(BF16).T, v_tab.astype(BF16), ln_g, ln_b)


def _rope_tables(pos):
    half = MLA_ROPE // 2
    inv = ROPE_THETA ** (-jnp.arange(half, dtype=jnp.float32) / half)
    ang = pos.astype(jnp.float32)[:, None] * inv
    cos, sin = jnp.cos(ang), jnp.sin(ang)
    z = jnp.zeros_like(cos)
    return (jnp.concatenate([cos, cos, z, z], -1), jnp.concatenate([-sin, z, z, z], -1),
            jnp.concatenate([z, sin, z, z], -1))


def kernel(x_prompt, x_sample, cache_mla_ckv, cache_mla_krope, cache_diff_k, cache_diff_v,
           mla_w_dqkv, mla_g_q, mla_w_uq, mla_g_kv, mla_w_ukv, mla_w_o,
           diff_w_qkv, diff_lam_q1, diff_lam_k1, diff_lam_q2, diff_lam_k2, diff_g_sub, diff_w_o,
           peer_w_query, peer_sub_keys, peer_u, peer_v,
           ln_mix_g, ln_mix_b, ln_ffn_g, ln_ffn_b):
    bp, tp, d = x_prompt.shape
    bs, ts, _ = x_sample.shape
    past = cache_mla_ckv.shape[2]
    mp, ms = bp * tp, bs * ts
    tks = past + ts
    tkp = -(-tks // LANE) * LANE
    tq_p = _tile(tp, 512, LANE)

    def pad_keys(a):
        return jnp.pad(a, ((0, 0), (0, tkp - tks), (0, 0))).reshape(bs * tkp, a.shape[-1])

    def k_nope_t(kv):
        rows = kv.shape[0]
        kn = kv.reshape(rows, MLA_HEADS, MLA_NOPE + MLA_V)[:, :, :MLA_NOPE]
        return kn.transpose(1, 2, 0).reshape(MLA_HEADS * MLA_NOPE, rows)

    xf = jnp.concatenate([x_prompt.reshape(mp, d), x_sample.reshape(ms, d)], 0)
    xb = xf.astype(BF16)
    pos = jnp.concatenate([jnp.tile(jnp.arange(tp), bp), jnp.tile(past + jnp.arange(ts), bs)])
    cos, sin_a, sin_b = _rope_tables(pos)

    j = 0
    w_dqkv = jnp.pad(mla_w_dqkv[j], ((0, 0), (0, LANE - MLA_ROPE))).astype(BF16)
    cq, ckv_f, ckv_b, kr_f, kr_b = _mla_proj(xb, w_dqkv, mla_g_q[j], mla_g_kv[j], cos, sin_a, sin_b)
    hq = MLA_NOPE + MLA_ROPE
    w_uq = jnp.pad(mla_w_uq[j].reshape(MLA_Q_LORA, MLA_HEADS, hq),
                   ((0, 0), (0, 0), (0, 2 * LANE - hq))).reshape(MLA_Q_LORA, MLA_HEADS * 2 * LANE)
    q = _q_proj(cq, w_uq.astype(BF16), cos, sin_a, sin_b)
    w_ukv = mla_w_ukv[j].astype(BF16)
    (kv_p,) = _mm(ckv_b[:mp], w_ukv, [BF16])
    o_p = _mla_attention(q, k_nope_t(kv_p), kr_b[:mp].T, kv_p, batch=bp, t_q=tp, t_k=tp,
                         t_valid=tp, q_row0=0, q_off=0, tq=tq_p, tk=tq_p)
    ckv_cat = pad_keys(jnp.concatenate([cache_mla_ckv[j].astype(BF16),
                                        ckv_b[mp:].reshape(bs, ts, MLA_KV_LORA)], 1))
    kr_cache = jnp.pad(cache_mla_krope[j], ((0, 0), (0, 0), (0, LANE - MLA_ROPE))).astype(BF16)
    kr_cat = pad_keys(jnp.concatenate([kr_cache, kr_b[mp:].reshape(bs, ts, LANE)], 1))
    (kv_s,) = _mm(ckv_cat, w_ukv, [BF16])
    o_s = _mla_attention(q, k_nope_t(kv_s), kr_cat.T, kv_s, batch=bs, t_q=ts, t_k=tkp,
                         t_valid=tks, q_row0=mp, q_off=past, tq=ts, tk=tkp)
    o = jnp.concatenate([o_p, o_s], 0)
    xf, xb = _mm_ln(o, mla_w_o[j].astype(BF16), xf, ln_mix_g[0], ln_mix_b[0])
    xf, xb = _peer_layer(xf, xb, peer_w_query[0], peer_sub_keys[0], peer_u[0], peer_v[0],
                         ln_ffn_g[0], ln_ffn_b[0])

    i = 1
    lam_init = 0.8 - 0.6 * math.exp(-0.3 * i)
    nqk = DIFF_HEADS * 2 * DIFF_QK
    w_qkv = diff_w_qkv[j].astype(BF16)
    (dq,) = _mm(xb, w_qkv[:, :nqk], [BF16])
    dk_f, dk_b = _mm(xb, w_qkv[:, nqk:2 * nqk], [F32, BF16])
    dv_f, dv_b = _mm(xb, w_qkv[:, 2 * nqk:], [F32, BF16])
    slopes = 2.0 ** (-8.0 * jnp.arange(1, DIFF_HEADS + 1, dtype=jnp.float32) / DIFF_HEADS)
    lam_vecs = jnp.stack([diff_lam_q1[j], diff_lam_k1[j], diff_lam_q2[j], diff_lam_k2[j]])
    diff_args = (slopes, lam_vecs, diff_g_sub[j])
    o_p = _diff_attention(dq, dk_b[:mp].T, dv_b, *diff_args, batch=bp, t_q=tp, t_k=tp,
                          t_valid=tp, q_row0=0, q_off=0, tq=tq_p, tk=tq_p, lam_init=lam_init)
    k_cat = pad_keys(jnp.concatenate([cache_diff_k[j].reshape(bs, past, nqk).astype(BF16),
                                      dk_b[mp:].reshape(bs, ts, nqk)], 1))
    v_cat = pad_keys(jnp.concatenate([cache_diff_v[j].reshape(bs, past, -1).astype(BF16),
                                      dv_b[mp:].reshape(bs, ts, -1)], 1))
    o_s = _diff_attention(dq, k_cat.T, v_cat, *diff_args, batch=bs, t_q=ts, t_k=tkp,
                          t_valid=tks, q_row0=mp, q_off=past, tq=ts, tk=tkp, lam_init=lam_init)
    o = jnp.concatenate([o_p, o_s], 0)
    xf, xb = _mm_ln(o, diff_w_o[j].astype(BF16), xf, ln_mix_g[1], ln_mix_b[1])
    xf, xb = _peer_layer(xf, xb, peer_w_query[1], peer_sub_keys[1], peer_u[1], peer_v[1],
                         ln_ffn_g[1], ln_ffn_b[1])

    kr_f = kr_f[:, :MLA_ROPE]
    return (xf[:mp].reshape(bp, tp, d), xf[mp:].reshape(bs, ts, d),
            ckv_f[:mp].reshape(1, bp, tp, -1), kr_f[:mp].reshape(1, bp, tp, -1),
            dk_f[:mp].reshape(1, bp, tp, DIFF_HEADS, -1), dv_f[:mp].reshape(1, bp, tp, DIFF_HEADS, -1),
            ckv_f[mp:].reshape(1, bs, ts, -1), kr_f[mp:].reshape(1, bs, ts, -1),
            dk_f[mp:].reshape(1, bs, ts, DIFF_HEADS, -1), dv_f[mp:].reshape(1, bs, ts, DIFF_HEADS, -1))
```

```python
import functools
import math

import jax
import jax.numpy as jnp
import numpy as np
from jax import lax
from jax.experimental import pallas as pl
from jax.experimental.pallas import tpu as pltpu

F32 = jnp.float32
BF16 = jnp.bfloat16

DEPTH = 2
CHUNK = 64
ALPHA = (2 * DEPTH) ** 0.25
LN_EPS = 1e-5
RMS_EPS = 1e-6
NEG = -1e30
ROPE_THETA = 10000.0
MLA_HEADS = 16
MLA_Q_LORA = 512
MLA_KV_LORA = 512
MLA_NOPE = 128
MLA_ROPE = 64
MLA_V = 128
DIFF_HEADS = 8
DIFF_QK = 128
DIFF_V = 256
PEER_HEADS = 8
PEER_TOPK = 16
N_KEYS = 128
PEER_DKEY = 256

LANE = 128
BF16_SUBLANE = 16
VMEM_LIMIT = 56 << 20


def _tile(n, target, mult=BF16_SUBLANE):
    for t in range(min(n, target), 0, -1):
        if n % t == 0 and t % mult == 0:
            return t
    raise ValueError(f"no tile for {n} (target {target}, multiple of {mult})")


def _params(*sem):
    return pltpu.CompilerParams(dimension_semantics=sem, vmem_limit_bytes=VMEM_LIMIT)


def _layer_norm(y, g, b):
    mu = jnp.mean(y, axis=-1, keepdims=True)
    d = y - mu
    var = jnp.mean(d * d, axis=-1, keepdims=True)
    return d * lax.rsqrt(var + LN_EPS) * g + b


def _rms_norm(y, g):
    return y * lax.rsqrt(jnp.mean(y * y, axis=-1, keepdims=True) + RMS_EPS) * g


def _rope128(r, cos, sin_a, sin_b):
    return r * cos + pltpu.roll(r, 96, 1) * sin_a + pltpu.roll(r, 32, 1) * sin_b


def _mm_kernel(x_ref, w_ref, *outs):
    y = jnp.dot(x_ref[...], w_ref[...], preferred_element_type=F32)
    for o in outs:
        o[...] = y.astype(o.dtype)


def _mm(x, w, out_dtypes, *, tm=528, tn=1024):
    m, kdim = x.shape
    n = w.shape[1]
    tm, tn = _tile(m, tm), _tile(n, tn, LANE)
    outs = pl.pallas_call(
        _mm_kernel,
        grid=(m // tm, n // tn),
        in_specs=[pl.BlockSpec((tm, kdim), lambda i, j: (i, 0)),
                  pl.BlockSpec((kdim, tn), lambda i, j: (0, j))],
        out_specs=[pl.BlockSpec((tm, tn), lambda i, j: (i, j)) for _ in out_dtypes],
        out_shape=[jax.ShapeDtypeStruct((m, n), dt) for dt in out_dtypes],
        compiler_params=_params("parallel", "parallel"),
        name="mm",
    )(x, w)
    return outs


def _resident(shape):
    return pl.BlockSpec(shape, lambda *_: (0,) * len(shape), pipeline_mode=pl.Buffered(1))


def _write_ln(y, g_ref, b_ref, of_ref, ob_ref):
    y = _layer_norm(y, g_ref[...], b_ref[...])
    of_ref[...] = y
    ob_ref[...] = y.astype(BF16)


def _mm_ln_kernel(a_ref, w_ref, r_ref, g_ref, b_ref, of_ref, ob_ref):
    mix = jnp.dot(a_ref[...], w_ref[...], preferred_element_type=F32)
    _write_ln(ALPHA * r_ref[...] + mix, g_ref, b_ref, of_ref, ob_ref)


def _mm_ln(a, w, resid, g, b, *, tm=528):
    m, kdim = a.shape
    d = w.shape[1]
    tm = _tile(m, tm)
    row = lambda i: (i, 0)
    return pl.pallas_call(
        _mm_ln_kernel,
        grid=(m // tm,),
        in_specs=[pl.BlockSpec((tm, kdim), row), _resident((kdim, d)),
                  pl.BlockSpec((tm, d), row), _resident((1, d)), _resident((1, d))],
        out_specs=[pl.BlockSpec((tm, d), row), pl.BlockSpec((tm, d), row)],
        out_shape=[jax.ShapeDtypeStruct((m, d), F32), jax.ShapeDtypeStruct((m, d), BF16)],
        compiler_params=_params("parallel"),
        name="mm_ln",
    )(a, w, resid, g.reshape(1, d), b.reshape(1, d))


def _res_ln_kernel(r_ref, mix_ref, g_ref, b_ref, of_ref, ob_ref):
    _write_ln(ALPHA * r_ref[...] + mix_ref[...], g_ref, b_ref, of_ref, ob_ref)


def _res_ln(resid, mix, g, b, *, tm=528):
    m, d = resid.shape
    tm = _tile(m, tm)
    row = lambda i: (i, 0)
    return pl.pallas_call(
        _res_ln_kernel,
        grid=(m // tm,),
        in_specs=[pl.BlockSpec((tm, d), row), pl.BlockSpec((tm, d), row),
                  _resident((1, d)), _resident((1, d))],
        out_specs=[pl.BlockSpec((tm, d), row), pl.BlockSpec((tm, d), row)],
        out_shape=[jax.ShapeDtypeStruct((m, d), F32), jax.ShapeDtypeStruct((m, d), BF16)],
        compiler_params=_params("parallel"),
        name="res_ln",
    )(resid, mix, g.reshape(1, d), b.reshape(1, d))


def _mla_proj_kernel(x_ref, w_ref, gq_ref, gkv_ref, cos_ref, sa_ref, sb_ref,
                     cq_ref, ckvf_ref, ckvb_ref, krf_ref, krb_ref, *, nq, nkv):
    lat = jnp.dot(x_ref[...], w_ref[...], preferred_element_type=F32)
    cq_ref[...] = _rms_norm(lat[:, :nq], gq_ref[...]).astype(BF16)
    ckv = _rms_norm(lat[:, nq:nq + nkv], gkv_ref[...])
    ckvf_ref[...] = ckv
    ckvb_ref[...] = ckv.astype(BF16)
    kr = _rope128(lat[:, nq + nkv:], cos_ref[...], sa_ref[...], sb_ref[...])
    krf_ref[...] = kr
    krb_ref[...] = kr.astype(BF16)


def _mla_proj(xb, w_pad, g_q, g_kv, cos, sin_a, sin_b, *, tm=528):
    m, kdim = xb.shape
    nq, nkv = g_q.shape[0], g_kv.shape[0]
    n = w_pad.shape[1]
    assert n == nq + nkv + LANE
    tm = _tile(m, tm)
    row = lambda i: (i, 0)
    return pl.pallas_call(
        functools.partial(_mla_proj_kernel, nq=nq, nkv=nkv),
        grid=(m // tm,),
        in_specs=[pl.BlockSpec((tm, kdim), row), _resident((kdim, n)),
                  _resident((1, nq)), _resident((1, nkv)),
                  pl.BlockSpec((tm, LANE), row), pl.BlockSpec((tm, LANE), row),
                  pl.BlockSpec((tm, LANE), row)],
        out_specs=[pl.BlockSpec((tm, nq), row), pl.BlockSpec((tm, nkv), row),
                   pl.BlockSpec((tm, nkv), row), pl.BlockSpec((tm, LANE), row),
                   pl.BlockSpec((tm, LANE), row)],
        out_shape=[jax.ShapeDtypeStruct((m, nq), BF16), jax.ShapeDtypeStruct((m, nkv), F32),
                   jax.ShapeDtypeStruct((m, nkv), BF16), jax.ShapeDtypeStruct((m, LANE), F32),
                   jax.ShapeDtypeStruct((m, LANE), BF16)],
        compiler_params=_params("parallel"),
        name="mla_proj",
    )(xb, w_pad, g_q.reshape(1, nq), g_kv.reshape(1, nkv), cos, sin_a, sin_b)


def _q_proj_kernel(c_ref, w_ref, cos_ref, sa_ref, sb_ref, o_ref, *, heads):
    y = jnp.dot(c_ref[...], w_ref[...], preferred_element_type=F32)
    for h in range(heads):
        lo = h * 2 * LANE
        o_ref[:, lo:lo + LANE] = y[:, lo:lo + LANE].astype(BF16)
        r = _rope128(y[:, lo + LANE:lo + 2 * LANE], cos_ref[...], sa_ref[...], sb_ref[...])
        o_ref[:, lo + LANE:lo + 2 * LANE] = r.astype(BF16)


def _q_proj(cq, w_arr, cos, sin_a, sin_b, *, tm=528, heads_per_step=2):
    m, kdim = cq.shape
    n = w_arr.shape[1]
    tm = _tile(m, tm)
    tn = heads_per_step * 2 * LANE
    row = lambda i, j: (i, 0)
    return pl.pallas_call(
        functools.partial(_q_proj_kernel, heads=heads_per_step),
        grid=(m // tm, n // tn),
        in_specs=[pl.BlockSpec((tm, kdim), row),
                  pl.BlockSpec((kdim, tn), lambda i, j: (0, j)),
                  pl.BlockSpec((tm, LANE), row), pl.BlockSpec((tm, LANE), row),
                  pl.BlockSpec((tm, LANE), row)],
        out_specs=pl.BlockSpec((tm, tn), lambda i, j: (i, j)),
        out_shape=jax.ShapeDtypeStruct((m, n), BF16),
        compiler_params=_params("parallel", "parallel"),
        name="q_proj",
    )(cq, w_arr, cos, sin_a, sin_b)


Q_CHAIN_ROWS = 256


def _row_parts(tq):
    n = max(1, tq // Q_CHAIN_ROWS)
    assert tq % n == 0
    return [slice(p * (tq // n), (p + 1) * (tq // n)) for p in range(n)]


def _visible_blocks(qi, *, q_off, tq, tk, nk):
    q_first = q_off + qi * tq
    q_last = q_first + tq - 1
    n_before = jnp.minimum(q_first // tk, nk)
    n_vis = jnp.minimum(((q_last // CHUNK + 1) * CHUNK + tk - 1) // tk, nk)
    return n_before, n_vis


def _positions(qi, ki, rows, *, q_off, tq, tk):
    shape = (rows.stop - rows.start, tk)
    qp = q_off + qi * tq + rows.start + lax.broadcasted_iota(jnp.int32, shape, 0)
    kp = ki * tk + lax.broadcasted_iota(jnp.int32, shape, 1)
    return qp, kp


def _key_mask(qi, ki, rows, geo):
    qp, kp = _positions(qi, ki, rows, q_off=geo["q_off"], tq=geo["tq"], tk=geo["tk"])
    k_chunk = kp // CHUNK
    if geo["t_valid"] < geo["nk"] * geo["tk"]:
        k_chunk = jnp.where(kp < geo["t_valid"], k_chunk, jnp.iinfo(jnp.int32).max)
    return k_chunk <= (qp // CHUNK), qp, kp


def _for_blocks(lo, hi, step):
    lax.fori_loop(lo, hi, lambda ki, carry: (step(ki), carry)[1], 0)


def _online_softmax_step(s, v, m_ref, l_ref, acc_ref, rows):
    m_prev = m_ref[rows, :]
    m_new = jnp.maximum(m_prev, jnp.max(s, axis=-1, keepdims=True))
    alpha = jnp.exp(m_prev - m_new)
    p = jnp.exp(s - m_new)
    l_ref[rows, :] = alpha * l_ref[rows, :] + jnp.sum(p, axis=-1, keepdims=True)
    acc_ref[rows, :] = (alpha * acc_ref[rows, :]
                        + jnp.dot(p.astype(BF16), v, preferred_element_type=F32))
    m_ref[rows, :] = m_new


def _mla_attn_kernel(q_ref, knt_ref, krt_ref, v_ref, o_ref, kcat_t, m_ref, l_ref, acc_ref,
                     *, geo, scale):
    qi = pl.program_id(2)
    tk = geo["tk"]

    @pl.when(qi == 0)
    def _():
        kcat_t[:MLA_NOPE, :] = knt_ref[...]
        kcat_t[MLA_NOPE:, :] = krt_ref[...]

    m_ref[...] = jnp.full_like(m_ref, -jnp.inf)
    l_ref[...] = jnp.zeros_like(l_ref)
    acc_ref[...] = jnp.zeros_like(acc_ref)

    def step(ki, masked):
        r0 = pl.multiple_of(ki * tk, tk)
        k_t = kcat_t[:, pl.ds(r0, tk)]
        v = v_ref[pl.ds(r0, tk), :]
        for rows in _row_parts(geo["tq"]):
            s = jnp.dot(q_ref[rows, :], k_t, preferred_element_type=F32) * scale
            if masked:
                s = jnp.where(_key_mask(qi, ki, rows, geo)[0], s, NEG)
            _online_softmax_step(s, v, m_ref, l_ref, acc_ref, rows)

    n_before, n_vis = _visible_blocks(qi, q_off=geo["q_off"], tq=geo["tq"], tk=tk, nk=geo["nk"])
    _for_blocks(0, n_before, functools.partial(step, masked=False))
    _for_blocks(n_before, n_vis, functools.partial(step, masked=True))
    o_ref[...] = (acc_ref[...] / l_ref[...]).astype(BF16)


def _mla_attention(q2d, knt, krt, v2d, *, batch, t_q, t_k, t_valid, q_row0, q_off, tq, tk):
    nq, nk = t_q // tq, t_k // tk
    assert q_row0 % tq == 0 and t_k % tk == 0 and tk % LANE == 0
    geo = dict(q_off=q_off, tq=tq, tk=tk, nk=nk, t_valid=t_valid)
    qb0 = q_row0 // tq
    return pl.pallas_call(
        functools.partial(_mla_attn_kernel, geo=geo, scale=(MLA_NOPE + MLA_ROPE) ** -0.5),
        grid=(batch, MLA_HEADS, nq),
        in_specs=[pl.BlockSpec((tq, 2 * LANE), lambda b, h, qi: (qb0 + b * nq + qi, h)),
                  pl.BlockSpec((MLA_NOPE, t_k), lambda b, h, qi: (h, b)),
                  pl.BlockSpec((LANE, t_k), lambda b, h, qi: (0, b)),
                  pl.BlockSpec((t_k, MLA_V), lambda b, h, qi: (b, h))],
        out_specs=pl.BlockSpec((tq, MLA_V), lambda b, h, qi: (b * nq + qi, h)),
        out_shape=jax.ShapeDtypeStruct((batch * t_q, MLA_HEADS * MLA_V), BF16),
        scratch_shapes=[pltpu.VMEM((2 * LANE, t_k), BF16),
                        pltpu.VMEM((tq, 1), F32), pltpu.VMEM((tq, 1), F32),
                        pltpu.VMEM((tq, MLA_V), F32)],
        compiler_params=_params("parallel", "parallel", "arbitrary"),
        name="mla_attn",
    )(q2d, knt, krt, v2d)


def _diff_attn_kernel(slope_ref, lam_ref, gsub_ref, q_ref, k_ref, v_ref, o_ref,
                      b0_ref, m1, l1, a1, m2, l2, a2, *, geo, scale, lam_init, k_transposed):
    h, qi = pl.program_id(1), pl.program_id(2)
    tq, tk = geo["tq"], geo["tk"]
    for m, l, a in ((m1, l1, a1), (m2, l2, a2)):
        m[...] = jnp.full_like(m, -jnp.inf)
        l[...] = jnp.zeros_like(l)
        a[...] = jnp.zeros_like(a)
    slope = slope_ref[h]
    q_first = geo["q_off"] + qi * tq
    b0_ref[...] = -slope * (lax.broadcasted_iota(jnp.int32, (tq, tk), 0)
                            - lax.broadcasted_iota(jnp.int32, (tq, tk), 1)).astype(F32)

    def scores(q_half, r0, lo):
        if k_transposed:
            return jnp.dot(q_half, k_ref[lo:lo + DIFF_QK, pl.ds(r0, tk)],
                           preferred_element_type=F32)
        return lax.dot_general(q_half, k_ref[pl.ds(r0, tk), lo:lo + DIFF_QK],
                               (((1,), (1,)), ((), ())), preferred_element_type=F32)

    def step(ki, masked):
        r0 = pl.multiple_of(ki * tk, tk)
        v = v_ref[pl.ds(r0, tk), :]
        for rows in _row_parts(tq):
            if masked:
                mask, qp, kp = _key_mask(qi, ki, rows, geo)
                bias = jnp.where(mask, -slope * jnp.abs(qp - kp).astype(F32), NEG)
            else:
                shift = -slope * jnp.full((1, tk), q_first - r0, jnp.int32).astype(F32)
                bias = b0_ref[rows, :] + shift
            s1 = scores(q_ref[rows, :DIFF_QK], r0, 0)
            _online_softmax_step(s1 * scale + bias, v, m1, l1, a1, rows)
            s2 = scores(q_ref[rows, DIFF_QK:], r0, DIFF_QK)
            _online_softmax_step(s2 * scale + bias, v, m2, l2, a2, rows)

    n_before, n_vis = _visible_blocks(qi, q_off=geo["q_off"], tq=tq, tk=tk, nk=geo["nk"])
    _for_blocks(0, n_before, functools.partial(step, masked=False))
    _for_blocks(n_before, n_vis, functools.partial(step, masked=True))

    lam_v = lam_ref[...]
    lam = (jnp.exp(jnp.sum(lam_v[0:1] * lam_v[1:2], axis=-1, keepdims=True))
           - jnp.exp(jnp.sum(lam_v[2:3] * lam_v[3:4], axis=-1, keepdims=True)) + lam_init)
    o = a1[...] / l1[...] - lam * (a2[...] / l2[...])
    o_ref[...] = (_rms_norm(o, gsub_ref[...]) * (1.0 - lam_init)).astype(BF16)


def _diff_attention(q2d, k, v2d, slopes, lam_vecs, g_sub, *, k_transposed,
                    batch, t_q, t_k, q_row0, q_off, tq, tk, lam_init):
    nq, nk = t_q // tq, t_k // tk
    assert q_row0 % tq == 0 and t_k % tk == 0 and (tk % LANE == 0 or not k_transposed)
    geo = dict(q_off=q_off, tq=tq, tk=tk, nk=nk, t_valid=t_k)
    qb0 = q_row0 // tq
    hd = 2 * DIFF_QK
    k_spec = (pl.BlockSpec((hd, t_k), lambda b, h, qi: (h, b)) if k_transposed
              else pl.BlockSpec((t_k, hd), lambda b, h, qi: (b, h)))
    return pl.pallas_call(
        functools.partial(_diff_attn_kernel, geo=geo, scale=DIFF_QK ** -0.5, lam_init=lam_init,
                          k_transposed=k_transposed),
        grid=(batch, DIFF_HEADS, nq),
        in_specs=[pl.BlockSpec(memory_space=pltpu.SMEM),
                  pl.BlockSpec((4, DIFF_QK), lambda b, h, qi: (0, 0)),
                  pl.BlockSpec((1, DIFF_V), lambda b, h, qi: (0, 0)),
                  pl.BlockSpec((tq, hd), lambda b, h, qi: (qb0 + b * nq + qi, h)),
                  k_spec,
                  pl.BlockSpec((t_k, DIFF_V), lambda b, h, qi: (b, h))],
        out_specs=pl.BlockSpec((tq, DIFF_V), lambda b, h, qi: (b * nq + qi, h)),
        out_shape=jax.ShapeDtypeStruct((batch * t_q, DIFF_HEADS * DIFF_V), BF16),
        scratch_shapes=[pltpu.VMEM((tq, tk), F32),
                        pltpu.VMEM((tq, 1), F32), pltpu.VMEM((tq, 1), F32),
                        pltpu.VMEM((tq, DIFF_V), F32),
                        pltpu.VMEM((tq, 1), F32), pltpu.VMEM((tq, 1), F32),
                        pltpu.VMEM((tq, DIFF_V), F32)],
        compiler_params=_params("parallel", "parallel", "parallel"),
        name="diff_attn",
    )(slopes, lam_vecs, g_sub.reshape(1, DIFF_V), q2d, k, v2d)


def _top_rows(s, n_rows, k, payload=None):
    pos = lax.broadcasted_iota(jnp.int32, s.shape, 0).astype(F32)
    vals, idxs, pays = [], [], []
    for _ in range(k):
        m = jnp.max(s, axis=0, keepdims=True)
        sel = jnp.min(jnp.where(s == m, pos, float(n_rows)), axis=0, keepdims=True)
        hit = pos == sel
        vals.append(m)
        idxs.append(sel)
        if payload is not None:
            pays.append(jnp.max(jnp.where(hit, payload, -1.0), axis=0, keepdims=True))
        s = jnp.where(hit, -jnp.inf, s)
    cat = lambda xs: jnp.concatenate(xs, axis=0)
    return cat(vals), cat(idxs), (cat(pays) if payload is not None else None)


def _peer_topk_kernel(q_ref, sk_ref, e_ref, g_ref):
    half = PEER_DKEY // 2
    q = q_ref[...]
    nt = (((1,), (1,)), ((), ()))
    sv, si = [], []
    for c in range(2):
        s = lax.dot_general(sk_ref[c], q[:, c * half:(c + 1) * half], nt,
                            preferred_element_type=F32)
        v, i, _ = _top_rows(s, N_KEYS, PEER_TOPK)
        sv.append(v)
        si.append(i)
    width = [PEER_TOPK // (a + 1) for a in range(PEER_TOPK)]
    n_cand = sum(width)
    pad = -n_cand % 8
    cand = jnp.concatenate([sv[0][a:a + 1] + sv[1][:width[a]] for a in range(PEER_TOPK)]
                           + [jnp.full((pad, q.shape[0]), -jnp.inf, F32)], axis=0)
    cidx = jnp.concatenate([si[0][a:a + 1] * float(N_KEYS) + si[1][:width[a]]
                            for a in range(PEER_TOPK)]
                           + [jnp.full((pad, q.shape[0]), -1.0, F32)], axis=0)
    fv, _, fe = _top_rows(cand, n_cand + pad, PEER_TOPK, payload=cidx)
    p = jnp.exp(fv - fv[0:1])
    g_ref[...] = p / jnp.sum(p, axis=0, keepdims=True)
    e_ref[...] = fe.astype(jnp.int32)


def _peer_topk(qb, sub_keys_b, *, tn=256):
    m = qb.shape[0]
    tn = _tile(m, tn, LANE)
    half = PEER_DKEY // 2
    out_spec = pl.BlockSpec((None, PEER_TOPK, tn), lambda i, h: (h, 0, i))
    return pl.pallas_call(
        _peer_topk_kernel,
        grid=(m // tn, PEER_HEADS),
        in_specs=[pl.BlockSpec((tn, PEER_DKEY), lambda i, h: (i, h)),
                  pl.BlockSpec((None, 2, N_KEYS, half), lambda i, h: (h, 0, 0, 0))],
        out_specs=[out_spec, out_spec],
        out_shape=[jax.ShapeDtypeStruct((PEER_HEADS, PEER_TOPK, m), jnp.int32),
                   jax.ShapeDtypeStruct((PEER_HEADS, PEER_TOPK, m), F32)],
        compiler_params=_params("parallel", "parallel"),
        name="peer_topk",
    )(qb, sub_keys_b)


GATE_ROWS = 8
GATE_UNROLL = 8


def _peer_gate_kernel(e_ref, g_ref, o_ref, *, rows):
    npick = PEER_HEADS * PEER_TOPK
    key = lax.broadcasted_iota(jnp.int32, (N_KEYS, npick), 0)
    nt = (((1,), (1,)), ((), ()))
    shift = N_KEYS.bit_length() - 1

    def body(blk, carry):
        for u in range(GATE_UNROLL):
            n = blk * GATE_UNROLL + u
            e = e_ref[pl.ds(n, 1), :]
            g = g_ref[pl.ds(n, 1), :]
            first = jnp.where(key == (e >> shift), 1.0, 0.0).astype(BF16)
            second = jnp.where(key == (e & (N_KEYS - 1)), g, 0.0).astype(BF16)
            w = lax.dot_general(first, second, nt, preferred_element_type=F32)
            o_ref[:, n] = w.reshape(N_KEYS // GATE_ROWS, GATE_ROWS, N_KEYS)
        return carry

    lax.fori_loop(0, rows // GATE_UNROLL, body, 0)


def _peer_gate(e_t, g_t, *, tb=64):
    m, npick = e_t.shape
    assert N_KEYS & (N_KEYS - 1) == 0
    tb = _tile(m, tb, GATE_UNROLL)
    nib = N_KEYS // GATE_ROWS
    return pl.pallas_call(
        functools.partial(_peer_gate_kernel, rows=tb),
        grid=(m // tb,),
        in_specs=[pl.BlockSpec((tb, npick), lambda i: (i, 0)),
                  pl.BlockSpec((tb, npick), lambda i: (i, 0))],
        out_specs=pl.BlockSpec((nib, tb, GATE_ROWS, N_KEYS), lambda i: (0, i, 0, 0)),
        out_shape=jax.ShapeDtypeStruct((nib, m, GATE_ROWS, N_KEYS), F32),
        compiler_params=_params("parallel"),
        name="peer_gate",
    )(e_t, g_t)


def _gelu_tanh(x):
    c = np.sqrt(2 / np.pi).astype(np.float32)
    return x * (0.5 * (1.0 + jnp.tanh(c * (x + 0.044715 * (x * x * x)))))


DENSE_PARTS = 2


def _peer_dense_kernel(xb_ref, wd_ref, ut_ref, v_ref, o_ref, w_ref, *, tn):
    @pl.when(pl.program_id(1) == 0)
    def _():
        o_ref[...] = jnp.zeros_like(o_ref)

    rows_per_part = tn // DENSE_PARTS
    for part in range(DENSE_PARTS):
        rows = slice(part * rows_per_part, (part + 1) * rows_per_part)
        h = jnp.dot(xb_ref[rows, :], ut_ref[...], preferred_element_type=F32)
        for r in range(GATE_ROWS):
            lanes = slice(r * N_KEYS, (r + 1) * N_KEYS)
            gate = wd_ref[pl.ds(rows.start * GATE_ROWS + r, rows_per_part, stride=GATE_ROWS), :]
            w_ref[rows, lanes] = (gate * _gelu_tanh(h[:, lanes])).astype(BF16)
        o_ref[rows, :] += jnp.dot(w_ref[rows, :], v_ref[...], preferred_element_type=F32)


def _peer_dense(xb, wd, u_t, v, layer, *, tn=1056):
    m, d = xb.shape
    ne = wd.shape[0]
    te = GATE_ROWS * N_KEYS
    assert v.shape[1] == ne * te
    tn = _tile(m, tn, BF16_SUBLANE * DENSE_PARTS)
    row = lambda i, e: (i, 0)
    return pl.pallas_call(
        functools.partial(_peer_dense_kernel, tn=tn),
        grid=(m // tn, ne),
        in_specs=[pl.BlockSpec((tn, d), row, pipeline_mode=pl.Buffered(1)),
                  pl.BlockSpec((None, tn * GATE_ROWS, N_KEYS), lambda i, e: (e, i, 0)),
                  pl.BlockSpec((None, d, te), lambda i, e: (layer, 0, e)),
                  pl.BlockSpec((None, te, d), lambda i, e: (layer, e, 0))],
        out_specs=pl.BlockSpec((tn, d), row),
        out_shape=jax.ShapeDtypeStruct((m, d), F32),
        scratch_shapes=[pltpu.VMEM((tn, te), BF16)],
        compiler_params=_params("parallel", "arbitrary"),
        name="peer_dense",
    )(xb, wd, u_t, v)


def _peer_layer(xf, xb, w_query, sub_keys, u_t_all, v_all, layer, ln_g, ln_b):
    m = xf.shape[0]
    npick = PEER_HEADS * PEER_TOPK
    (qb,) = _mm(xb, w_query.astype(BF16), [BF16])
    eidx, gates = _peer_topk(qb, sub_keys.astype(BF16))
    e_t = eidx.reshape(npick, m).T
    g_t = gates.reshape(npick, m).T
    wd = _peer_gate(e_t, g_t).reshape(N_KEYS // GATE_ROWS, m * GATE_ROWS, N_KEYS)
    mix = _peer_dense(xb, wd, u_t_all, v_all, layer)
    return _res_ln(xf, mix, ln_g, ln_b)


def _rope_tables(pos):
    half = MLA_ROPE // 2
    inv = ROPE_THETA ** (-jnp.arange(half, dtype=jnp.float32) / half)
    ang = pos.astype(jnp.float32)[:, None] * inv
    cos, sin = jnp.cos(ang), jnp.sin(ang)
    z = jnp.zeros_like(cos)
    return (jnp.concatenate([cos, cos, z, z], -1), jnp.concatenate([-sin, z, z, z], -1),
            jnp.concatenate([z, sin, z, z], -1))


def kernel(x_prompt, x_sample, cache_mla_ckv, cache_mla_krope, cache_diff_k, cache_diff_v,
           mla_w_dqkv, mla_g_q, mla_w_uq, mla_g_kv, mla_w_ukv, mla_w_o,
           diff_w_qkv, diff_lam_q1, diff_lam_k1, diff_lam_q2, diff_lam_k2, diff_g_sub, diff_w_o,
           peer_w_query, peer_sub_keys, peer_u, peer_v,
           ln_mix_g, ln_mix_b, ln_ffn_g, ln_ffn_b):
    bp, tp, d = x_prompt.shape
    bs, ts, _ = x_sample.shape
    past = cache_mla_ckv.shape[2]
    mp, ms = bp * tp, bs * ts
    tks = past + ts
    tkp = -(-tks // LANE) * LANE
    tq_p = _tile(tp, 512, LANE)

    def pad_keys(a):
        return jnp.pad(a, ((0, 0), (0, tkp - tks), (0, 0))).reshape(bs * tkp, a.shape[-1])

    u_t_all = peer_u.astype(BF16).swapaxes(1, 2)
    v_all = peer_v.astype(BF16)
    xf = jnp.concatenate([x_prompt.reshape(mp, d), x_sample.reshape(ms, d)], 0)
    xb = xf.astype(BF16)
    pos = jnp.concatenate([jnp.tile(jnp.arange(tp), bp), jnp.tile(past + jnp.arange(ts), bs)])
    cos, sin_a, sin_b = _rope_tables(pos)

    j = 0
    w_dqkv = jnp.pad(mla_w_dqkv[j], ((0, 0), (0, LANE - MLA_ROPE))).astype(BF16)
    cq, ckv_f, ckv_b, kr_f, kr_b = _mla_proj(xb, w_dqkv, mla_g_q[j], mla_g_kv[j], cos, sin_a, sin_b)
    hq = MLA_NOPE + MLA_ROPE
    w_uq = jnp.pad(mla_w_uq[j].reshape(MLA_Q_LORA, MLA_HEADS, hq),
                   ((0, 0), (0, 0), (0, 2 * LANE - hq))).reshape(MLA_Q_LORA, MLA_HEADS * 2 * LANE)
    q = _q_proj(cq, w_uq.astype(BF16), cos, sin_a, sin_b)
    w_ukv = mla_w_ukv[j].reshape(MLA_KV_LORA, MLA_HEADS, MLA_NOPE + MLA_V)
    w_uk_t = w_ukv[:, :, :MLA_NOPE].reshape(MLA_KV_LORA, -1).T.astype(BF16)
    w_uv = w_ukv[:, :, MLA_NOPE:].reshape(MLA_KV_LORA, -1).astype(BF16)
    (knt_p,) = _mm(w_uk_t, ckv_b[:mp].T, [BF16])
    (v_p,) = _mm(ckv_b[:mp], w_uv, [BF16])
    o_p = _mla_attention(q, knt_p, kr_b[:mp].T, v_p, batch=bp, t_q=tp, t_k=tp,
                         t_valid=tp, q_row0=0, q_off=0, tq=tq_p, tk=tq_p)
    ckv_cat = pad_keys(jnp.concatenate([cache_mla_ckv[j].astype(BF16),
                                        ckv_b[mp:].reshape(bs, ts, MLA_KV_LORA)], 1))
    kr_cache = jnp.pad(cache_mla_krope[j], ((0, 0), (0, 0), (0, LANE - MLA_ROPE))).astype(BF16)
    kr_cat = pad_keys(jnp.concatenate([kr_cache, kr_b[mp:].reshape(bs, ts, LANE)], 1))
    (knt_s,) = _mm(w_uk_t, ckv_cat.T, [BF16])
    (v_s,) = _mm(ckv_cat, w_uv, [BF16])
    o_s = _mla_attention(q, knt_s, kr_cat.T, v_s, batch=bs, t_q=ts, t_k=tkp,
                         t_valid=tks, q_row0=mp, q_off=past, tq=ts, tk=tkp)
    o = jnp.concatenate([o_p, o_s], 0)
    xf, xb = _mm_ln(o, mla_w_o[j].astype(BF16), xf, ln_mix_g[0], ln_mix_b[0])
    xf, xb = _peer_layer(xf, xb, peer_w_query[0], peer_sub_keys[0], u_t_all, v_all, 0,
                         ln_ffn_g[0], ln_ffn_b[0])

    i = 1
    lam_init = 0.8 - 0.6 * math.exp(-0.3 * i)
    nqk = DIFF_HEADS * 2 * DIFF_QK
    w_qkv = diff_w_qkv[j].astype(BF16)
    (dq,) = _mm(xb, w_qkv[:, :nqk], [BF16])
    dk_f, dk_b = _mm(xb, w_qkv[:, nqk:2 * nqk], [F32, BF16])
    dv_f, dv_b = _mm(xb, w_qkv[:, 2 * nqk:], [F32, BF16])
    slopes = 2.0 ** (-8.0 * jnp.arange(1, DIFF_HEADS + 1, dtype=jnp.float32) / DIFF_HEADS)
    lam_vecs = jnp.stack([diff_lam_q1[j], diff_lam_k1[j], diff_lam_q2[j], diff_lam_k2[j]])
    diff_args = (slopes, lam_vecs, diff_g_sub[j])
    o_p = _diff_attention(dq, dk_b[:mp].T, dv_b, *diff_args, k_transposed=True, batch=bp,
                          t_q=tp, t_k=tp, q_row0=0, q_off=0, tq=tq_p, tk=tq_p, lam_init=lam_init)
    k_cat = jnp.concatenate([cache_diff_k[j].reshape(bs, past, nqk).astype(BF16),
                             dk_b[mp:].reshape(bs, ts, nqk)], 1).reshape(bs * tks, nqk)
    v_cat = jnp.concatenate([cache_diff_v[j].reshape(bs, past, -1).astype(BF16),
                             dv_b[mp:].reshape(bs, ts, -1)], 1).reshape(bs * tks, -1)
    o_s = _diff_attention(dq, k_cat, v_cat, *diff_args, k_transposed=False, batch=bs,
                          t_q=ts, t_k=tks, q_row0=mp, q_off=past, tq=ts, tk=tks, lam_init=lam_init)
    o = jnp.concatenate([o_p, o_s], 0)
    xf, xb = _mm_ln(o, diff_w_o[j].astype(BF16), xf, ln_mix_g[1], ln_mix_b[1])
    xf, xb = _peer_layer(xf, xb, peer_w_query[1], peer_sub_keys[1], u_t_all, v_all, 1,
                         ln_ffn_g[1], ln_ffn_b[1])

    kr_f = kr_f[:, :MLA_ROPE]
    return (xf[:mp].reshape(bp, tp, d), xf[mp:].reshape(bs, ts, d),
            ckv_f[:mp].reshape(1, bp, tp, -1), kr_f[:mp].reshape(1, bp, tp, -1),
            dk_f[:mp].reshape(1, bp, tp, DIFF_HEADS, -1), dv_f[:mp].reshape(1, bp, tp, DIFF_HEADS, -1),
            ckv_f[mp:].reshape(1, bs, ts, -1), kr_f[mp:].reshape(1, bs, ts, -1),
            dk_f[mp:].reshape(1, bs, ts, DIFF_HEADS, -1), dv_f[mp:].reshape(1, bs, ts, DIFF_HEADS, -1))
```

```python
import functools
import math

import jax
import jax.numpy as jnp
import numpy as np
from jax import lax
from jax.experimental import pallas as pl
from jax.experimental.pallas import tpu as pltpu

F32 = jnp.float32
BF16 = jnp.bfloat16

DEPTH = 2
CHUNK = 64
ALPHA = (2 * DEPTH) ** 0.25
LN_EPS = 1e-5
RMS_EPS = 1e-6
NEG = -1e30
ROPE_THETA = 10000.0
MLA_HEADS = 16
MLA_Q_LORA = 512
MLA_KV_LORA = 512
MLA_NOPE = 128
MLA_ROPE = 64
MLA_V = 128
DIFF_HEADS = 8
DIFF_QK = 128
DIFF_V = 256
PEER_HEADS = 8
PEER_TOPK = 16
N_KEYS = 128
PEER_DKEY = 256

LANE = 128
BF16_SUBLANE = 16
VMEM_LIMIT = 56 << 20


def _tile(n, target, mult=BF16_SUBLANE):
    for t in range(min(n, target), 0, -1):
        if n % t == 0 and t % mult == 0:
            return t
    raise ValueError(f"no tile for {n} (target {target}, multiple of {mult})")


def _params(*sem):
    return pltpu.CompilerParams(dimension_semantics=sem, vmem_limit_bytes=VMEM_LIMIT)


def _layer_norm(y, g, b):
    mu = jnp.mean(y, axis=-1, keepdims=True)
    d = y - mu
    var = jnp.mean(d * d, axis=-1, keepdims=True)
    return d * lax.rsqrt(var + LN_EPS) * g + b


def _rms_norm(y, g):
    return y * lax.rsqrt(jnp.mean(y * y, axis=-1, keepdims=True) + RMS_EPS) * g


def _rope128(r, cos, sin_a, sin_b):
    return r * cos + pltpu.roll(r, 96, 1) * sin_a + pltpu.roll(r, 32, 1) * sin_b


def _mm_kernel(x_ref, w_ref, *outs):
    y = jnp.dot(x_ref[...], w_ref[...], preferred_element_type=F32)
    for o in outs:
        o[...] = y.astype(o.dtype)


def _mm(x, w, out_dtypes, *, tm=528, tn=1024):
    m, kdim = x.shape
    n = w.shape[1]
    tm, tn = _tile(m, tm), _tile(n, tn, LANE)
    outs = pl.pallas_call(
        _mm_kernel,
        grid=(m // tm, n // tn),
        in_specs=[pl.BlockSpec((tm, kdim), lambda i, j: (i, 0)),
                  pl.BlockSpec((kdim, tn), lambda i, j: (0, j))],
        out_specs=[pl.BlockSpec((tm, tn), lambda i, j: (i, j)) for _ in out_dtypes],
        out_shape=[jax.ShapeDtypeStruct((m, n), dt) for dt in out_dtypes],
        compiler_params=_params("parallel", "parallel"),
        name="mm",
    )(x, w)
    return outs


def _resident(shape):
    return pl.BlockSpec(shape, lambda *_: (0,) * len(shape), pipeline_mode=pl.Buffered(1))


def _write_ln(y, g_ref, b_ref, of_ref, ob_ref):
    y = _layer_norm(y, g_ref[...], b_ref[...])
    of_ref[...] = y
    ob_ref[...] = y.astype(BF16)


def _mm_ln_kernel(a_ref, w_ref, r_ref, g_ref, b_ref, of_ref, ob_ref):
    mix = jnp.dot(a_ref[...], w_ref[...], preferred_element_type=F32)
    _write_ln(ALPHA * r_ref[...] + mix, g_ref, b_ref, of_ref, ob_ref)


def _mm_ln(a, w, resid, g, b, *, tm=528):
    m, kdim = a.shape
    d = w.shape[1]
    tm = _tile(m, tm)
    row = lambda i: (i, 0)
    return pl.pallas_call(
        _mm_ln_kernel,
        grid=(m // tm,),
        in_specs=[pl.BlockSpec((tm, kdim), row), _resident((kdim, d)),
                  pl.BlockSpec((tm, d), row), _resident((1, d)), _resident((1, d))],
        out_specs=[pl.BlockSpec((tm, d), row), pl.BlockSpec((tm, d), row)],
        out_shape=[jax.ShapeDtypeStruct((m, d), F32), jax.ShapeDtypeStruct((m, d), BF16)],
        compiler_params=_params("parallel"),
        name="mm_ln",
    )(a, w, resid, g.reshape(1, d), b.reshape(1, d))


def _mla_proj_kernel(x_ref, w_ref, gq_ref, gkv_ref, cos_ref, sa_ref, sb_ref,
                     cq_ref, ckvf_ref, ckvb_ref, krf_ref, krb_ref, *, nq, nkv):
    lat = jnp.dot(x_ref[...], w_ref[...], preferred_element_type=F32)
    cq_ref[...] = _rms_norm(lat[:, :nq], gq_ref[...]).astype(BF16)
    ckv = _rms_norm(lat[:, nq:nq + nkv], gkv_ref[...])
    ckvf_ref[...] = ckv
    ckvb_ref[...] = ckv.astype(BF16)
    kr = _rope128(lat[:, nq + nkv:], cos_ref[...], sa_ref[...], sb_ref[...])
    krf_ref[...] = kr
    krb_ref[...] = kr.astype(BF16)


def _mla_proj(xb, w_pad, g_q, g_kv, cos, sin_a, sin_b, *, tm=528):
    m, kdim = xb.shape
    nq, nkv = g_q.shape[0], g_kv.shape[0]
    n = w_pad.shape[1]
    assert n == nq + nkv + LANE
    tm = _tile(m, tm)
    row = lambda i: (i, 0)
    return pl.pallas_call(
        functools.partial(_mla_proj_kernel, nq=nq, nkv=nkv),
        grid=(m // tm,),
        in_specs=[pl.BlockSpec((tm, kdim), row), _resident((kdim, n)),
                  _resident((1, nq)), _resident((1, nkv)),
                  pl.BlockSpec((tm, LANE), row), pl.BlockSpec((tm, LANE), row),
                  pl.BlockSpec((tm, LANE), row)],
        out_specs=[pl.BlockSpec((tm, nq), row), pl.BlockSpec((tm, nkv), row),
                   pl.BlockSpec((tm, nkv), row), pl.BlockSpec((tm, LANE), row),
                   pl.BlockSpec((tm, LANE), row)],
        out_shape=[jax.ShapeDtypeStruct((m, nq), BF16), jax.ShapeDtypeStruct((m, nkv), F32),
                   jax.ShapeDtypeStruct((m, nkv), BF16), jax.ShapeDtypeStruct((m, LANE), F32),
                   jax.ShapeDtypeStruct((m, LANE), BF16)],
        compiler_params=_params("parallel"),
        name="mla_proj",
    )(xb, w_pad, g_q.reshape(1, nq), g_kv.reshape(1, nkv), cos, sin_a, sin_b)


def _q_proj_kernel(c_ref, w_ref, cos_ref, sa_ref, sb_ref, o_ref, *, heads):
    y = jnp.dot(c_ref[...], w_ref[...], preferred_element_type=F32)
    for h in range(heads):
        lo = h * 2 * LANE
        o_ref[:, lo:lo + LANE] = y[:, lo:lo + LANE].astype(BF16)
        r = _rope128(y[:, lo + LANE:lo + 2 * LANE], cos_ref[...], sa_ref[...], sb_ref[...])
        o_ref[:, lo + LANE:lo + 2 * LANE] = r.astype(BF16)


def _q_proj(cq, w_arr, cos, sin_a, sin_b, *, tm=528, heads_per_step=8):
    m, kdim = cq.shape
    n = w_arr.shape[1]
    tm = _tile(m, tm)
    tn = heads_per_step * 2 * LANE
    row = lambda i, j: (i, 0)
    return pl.pallas_call(
        functools.partial(_q_proj_kernel, heads=heads_per_step),
        grid=(m // tm, n // tn),
        in_specs=[pl.BlockSpec((tm, kdim), row),
                  pl.BlockSpec((kdim, tn), lambda i, j: (0, j)),
                  pl.BlockSpec((tm, LANE), row), pl.BlockSpec((tm, LANE), row),
                  pl.BlockSpec((tm, LANE), row)],
        out_specs=pl.BlockSpec((tm, tn), lambda i, j: (i, j)),
        out_shape=jax.ShapeDtypeStruct((m, n), BF16),
        compiler_params=_params("parallel", "parallel"),
        name="q_proj",
    )(cq, w_arr, cos, sin_a, sin_b)


Q_CHAIN_ROWS = 256


def _row_parts(tq):
    n = max(1, tq // Q_CHAIN_ROWS)
    assert tq % n == 0
    return [slice(p * (tq // n), (p + 1) * (tq // n)) for p in range(n)]


def _visible_blocks(qi, *, q_off, tq, tk, nk):
    q_first = q_off + qi * tq
    q_last = q_first + tq - 1
    n_before = jnp.minimum(q_first // tk, nk)
    n_vis = jnp.minimum(((q_last // CHUNK + 1) * CHUNK + tk - 1) // tk, nk)
    return n_before, n_vis


def _positions(qi, ki, rows, *, q_off, tq, tk):
    shape = (rows.stop - rows.start, tk)
    qp = q_off + qi * tq + rows.start + lax.broadcasted_iota(jnp.int32, shape, 0)
    kp = ki * tk + lax.broadcasted_iota(jnp.int32, shape, 1)
    return qp, kp


def _key_mask(qi, ki, rows, geo):
    qp, kp = _positions(qi, ki, rows, q_off=geo["q_off"], tq=geo["tq"], tk=geo["tk"])
    k_chunk = kp // CHUNK
    if geo["t_valid"] < geo["nk"] * geo["tk"]:
        k_chunk = jnp.where(kp < geo["t_valid"], k_chunk, jnp.iinfo(jnp.int32).max)
    return k_chunk <= (qp // CHUNK), qp, kp


def _for_blocks(lo, hi, step):
    lax.fori_loop(lo, hi, lambda ki, carry: (step(ki), carry)[1], 0)


def _online_softmax_step(s, v, m_ref, l_ref, acc_ref, rows):
    m_prev = m_ref[rows, :]
    m_new = jnp.maximum(m_prev, jnp.max(s, axis=-1, keepdims=True))
    alpha = jnp.exp(m_prev - m_new)
    p = jnp.exp(s - m_new)
    l_ref[rows, :] = alpha * l_ref[rows, :] + jnp.sum(p, axis=-1, keepdims=True)
    acc_ref[rows, :] = (alpha * acc_ref[rows, :]
                        + jnp.dot(p.astype(BF16), v, preferred_element_type=F32))
    m_ref[rows, :] = m_new


def _mla_attn_kernel(q_ref, knt_ref, krt_ref, v_ref, o_ref, kcat_t, m_ref, l_ref, acc_ref,
                     *, geo, scale):
    qi = pl.program_id(2)
    tk = geo["tk"]

    @pl.when(qi == 0)
    def _():
        kcat_t[:MLA_NOPE, :] = knt_ref[...]
        kcat_t[MLA_NOPE:, :] = krt_ref[...]

    m_ref[...] = jnp.full_like(m_ref, -jnp.inf)
    l_ref[...] = jnp.zeros_like(l_ref)
    acc_ref[...] = jnp.zeros_like(acc_ref)

    def step(ki, masked):
        r0 = pl.multiple_of(ki * tk, tk)
        k_t = kcat_t[:, pl.ds(r0, tk)]
        v = v_ref[pl.ds(r0, tk), :]
        for rows in _row_parts(geo["tq"]):
            s = jnp.dot(q_ref[rows, :], k_t, preferred_element_type=F32) * scale
            if masked:
                s = jnp.where(_key_mask(qi, ki, rows, geo)[0], s, NEG)
            _online_softmax_step(s, v, m_ref, l_ref, acc_ref, rows)

    n_before, n_vis = _visible_blocks(qi, q_off=geo["q_off"], tq=geo["tq"], tk=tk, nk=geo["nk"])
    _for_blocks(0, n_before, functools.partial(step, masked=False))
    _for_blocks(n_before, n_vis, functools.partial(step, masked=True))
    o_ref[...] = (acc_ref[...] / l_ref[...]).astype(BF16)


def _mla_attention(q2d, knt, krt, v2d, *, batch, t_q, t_k, t_valid, q_row0, q_off, tq, tk):
    nq, nk = t_q // tq, t_k // tk
    assert q_row0 % tq == 0 and t_k % tk == 0 and tk % LANE == 0
    geo = dict(q_off=q_off, tq=tq, tk=tk, nk=nk, t_valid=t_valid)
    qb0 = q_row0 // tq
    return pl.pallas_call(
        functools.partial(_mla_attn_kernel, geo=geo, scale=(MLA_NOPE + MLA_ROPE) ** -0.5),
        grid=(batch, MLA_HEADS, nq),
        in_specs=[pl.BlockSpec((tq, 2 * LANE), lambda b, h, qi: (qb0 + b * nq + qi, h)),
                  pl.BlockSpec((MLA_NOPE, t_k), lambda b, h, qi: (h, b)),
                  pl.BlockSpec((LANE, t_k), lambda b, h, qi: (0, b)),
                  pl.BlockSpec((t_k, MLA_V), lambda b, h, qi: (b, h))],
        out_specs=pl.BlockSpec((tq, MLA_V), lambda b, h, qi: (b * nq + qi, h)),
        out_shape=jax.ShapeDtypeStruct((batch * t_q, MLA_HEADS * MLA_V), BF16),
        scratch_shapes=[pltpu.VMEM((2 * LANE, t_k), BF16),
                        pltpu.VMEM((tq, 1), F32), pltpu.VMEM((tq, 1), F32),
                        pltpu.VMEM((tq, MLA_V), F32)],
        compiler_params=_params("parallel", "parallel", "arbitrary"),
        name="mla_attn",
    )(q2d, knt, krt, v2d)


def _diff_attn_kernel(slope_ref, lam_ref, gsub_ref, q_ref, k_ref, v_ref, o_ref,
                      b0_ref, m1, l1, a1, m2, l2, a2, *, geo, scale, lam_init, k_transposed):
    h, qi = pl.program_id(1), pl.program_id(2)
    tq, tk = geo["tq"], geo["tk"]
    for m, l, a in ((m1, l1, a1), (m2, l2, a2)):
        m[...] = jnp.full_like(m, -jnp.inf)
        l[...] = jnp.zeros_like(l)
        a[...] = jnp.zeros_like(a)
    slope = slope_ref[h]
    q_first = geo["q_off"] + qi * tq
    b0_ref[...] = -slope * (lax.broadcasted_iota(jnp.int32, (tq, tk), 0)
                            - lax.broadcasted_iota(jnp.int32, (tq, tk), 1)).astype(F32)

    def scores(q_half, r0, lo):
        if k_transposed:
            return jnp.dot(q_half, k_ref[lo:lo + DIFF_QK, pl.ds(r0, tk)],
                           preferred_element_type=F32)
        return lax.dot_general(q_half, k_ref[pl.ds(r0, tk), lo:lo + DIFF_QK],
                               (((1,), (1,)), ((), ())), preferred_element_type=F32)

    def step(ki, masked):
        r0 = pl.multiple_of(ki * tk, tk)
        v = v_ref[pl.ds(r0, tk), :]
        for rows in _row_parts(tq):
            if masked:
                mask, qp, kp = _key_mask(qi, ki, rows, geo)
                bias = jnp.where(mask, -slope * jnp.abs(qp - kp).astype(F32), NEG)
            else:
                shift = -slope * jnp.full((1, tk), q_first - r0, jnp.int32).astype(F32)
                bias = b0_ref[rows, :] + shift
            s1 = scores(q_ref[rows, :DIFF_QK], r0, 0)
            _online_softmax_step(s1 * scale + bias, v, m1, l1, a1, rows)
            s2 = scores(q_ref[rows, DIFF_QK:], r0, DIFF_QK)
            _online_softmax_step(s2 * scale + bias, v, m2, l2, a2, rows)

    n_before, n_vis = _visible_blocks(qi, q_off=geo["q_off"], tq=tq, tk=tk, nk=geo["nk"])
    _for_blocks(0, n_before, functools.partial(step, masked=False))
    _for_blocks(n_before, n_vis, functools.partial(step, masked=True))

    lam_v = lam_ref[...]
    lam = (jnp.exp(jnp.sum(lam_v[0:1] * lam_v[1:2], axis=-1, keepdims=True))
           - jnp.exp(jnp.sum(lam_v[2:3] * lam_v[3:4], axis=-1, keepdims=True)) + lam_init)
    o = a1[...] / l1[...] - lam * (a2[...] / l2[...])
    o_ref[...] = (_rms_norm(o, gsub_ref[...]) * (1.0 - lam_init)).astype(BF16)


def _diff_attention(q2d, k, v2d, slopes, lam_vecs, g_sub, *, k_transposed,
                    batch, t_q, t_k, q_row0, q_off, tq, tk, lam_init):
    nq, nk = t_q // tq, t_k // tk
    assert q_row0 % tq == 0 and t_k % tk == 0 and (tk % LANE == 0 or not k_transposed)
    geo = dict(q_off=q_off, tq=tq, tk=tk, nk=nk, t_valid=t_k)
    qb0 = q_row0 // tq
    hd = 2 * DIFF_QK
    k_spec = (pl.BlockSpec((hd, t_k), lambda b, h, qi: (h, b)) if k_transposed
              else pl.BlockSpec((t_k, hd), lambda b, h, qi: (b, h)))
    return pl.pallas_call(
        functools.partial(_diff_attn_kernel, geo=geo, scale=DIFF_QK ** -0.5, lam_init=lam_init,
                          k_transposed=k_transposed),
        grid=(batch, DIFF_HEADS, nq),
        in_specs=[pl.BlockSpec(memory_space=pltpu.SMEM),
                  pl.BlockSpec((4, DIFF_QK), lambda b, h, qi: (0, 0)),
                  pl.BlockSpec((1, DIFF_V), lambda b, h, qi: (0, 0)),
                  pl.BlockSpec((tq, hd), lambda b, h, qi: (qb0 + b * nq + qi, h)),
                  k_spec,
                  pl.BlockSpec((t_k, DIFF_V), lambda b, h, qi: (b, h))],
        out_specs=pl.BlockSpec((tq, DIFF_V), lambda b, h, qi: (b * nq + qi, h)),
        out_shape=jax.ShapeDtypeStruct((batch * t_q, DIFF_HEADS * DIFF_V), BF16),
        scratch_shapes=[pltpu.VMEM((tq, tk), F32),
                        pltpu.VMEM((tq, 1), F32), pltpu.VMEM((tq, 1), F32),
                        pltpu.VMEM((tq, DIFF_V), F32),
                        pltpu.VMEM((tq, 1), F32), pltpu.VMEM((tq, 1), F32),
                        pltpu.VMEM((tq, DIFF_V), F32)],
        compiler_params=_params("parallel", "parallel", "parallel"),
        name="diff_attn",
    )(slopes, lam_vecs, g_sub.reshape(1, DIFF_V), q2d, k, v2d)


def _top_rows(s, n_rows, k, payload=None):
    pos = lax.broadcasted_iota(jnp.int32, s.shape, 0).astype(F32)
    vals, idxs, pays = [], [], []
    for _ in range(k):
        m = jnp.max(s, axis=0, keepdims=True)
        sel = jnp.min(jnp.where(s == m, pos, float(n_rows)), axis=0, keepdims=True)
        hit = pos == sel
        vals.append(m)
        idxs.append(sel)
        if payload is not None:
            pays.append(jnp.max(jnp.where(hit, payload, -1.0), axis=0, keepdims=True))
        s = jnp.where(hit, -jnp.inf, s)
    cat = lambda xs: jnp.concatenate(xs, axis=0)
    return cat(vals), cat(idxs), (cat(pays) if payload is not None else None)


def _peer_topk_kernel(q_ref, sk_ref, e_ref, g_ref):
    half = PEER_DKEY // 2
    q = q_ref[...]
    nt = (((1,), (1,)), ((), ()))
    sv, si = [], []
    for c in range(2):
        s = lax.dot_general(sk_ref[c], q[:, c * half:(c + 1) * half], nt,
                            preferred_element_type=F32)
        v, i, _ = _top_rows(s, N_KEYS, PEER_TOPK)
        sv.append(v)
        si.append(i)
    width = [PEER_TOPK // (a + 1) for a in range(PEER_TOPK)]
    n_cand = sum(width)
    pad = -n_cand % 8
    cand = jnp.concatenate([sv[0][a:a + 1] + sv[1][:width[a]] for a in range(PEER_TOPK)]
                           + [jnp.full((pad, q.shape[0]), -jnp.inf, F32)], axis=0)
    cidx = jnp.concatenate([si[0][a:a + 1] * float(N_KEYS) + si[1][:width[a]]
                            for a in range(PEER_TOPK)]
                           + [jnp.full((pad, q.shape[0]), -1.0, F32)], axis=0)
    fv, _, fe = _top_rows(cand, n_cand + pad, PEER_TOPK, payload=cidx)
    p = jnp.exp(fv - fv[0:1])
    g_ref[...] = p / jnp.sum(p, axis=0, keepdims=True)
    e_ref[...] = fe.astype(jnp.int32)


def _peer_topk(qb, sub_keys_b, *, tn=256):
    m = qb.shape[0]
    tn = _tile(m, tn, LANE)
    half = PEER_DKEY // 2
    out_spec = pl.BlockSpec((None, PEER_TOPK, tn), lambda i, h: (h, 0, i))
    return pl.pallas_call(
        _peer_topk_kernel,
        grid=(m // tn, PEER_HEADS),
        in_specs=[pl.BlockSpec((tn, PEER_DKEY), lambda i, h: (i, h)),
                  pl.BlockSpec((None, 2, N_KEYS, half), lambda i, h: (h, 0, 0, 0))],
        out_specs=[out_spec, out_spec],
        out_shape=[jax.ShapeDtypeStruct((PEER_HEADS, PEER_TOPK, m), jnp.int32),
                   jax.ShapeDtypeStruct((PEER_HEADS, PEER_TOPK, m), F32)],
        compiler_params=_params("parallel", "parallel"),
        name="peer_topk",
    )(qb, sub_keys_b)


GATE_ROWS = 8
GATE_UNROLL = 32


def _peer_gate_kernel(e_ref, g_ref, o_ref, *, rows):
    npick = PEER_HEADS * PEER_TOPK
    key = lax.broadcasted_iota(jnp.int32, (N_KEYS, npick), 0).astype(F32).astype(BF16)
    one, zero = jnp.ones((), BF16), jnp.zeros((), BF16)
    nt = (((1,), (1,)), ((), ()))
    shift = N_KEYS.bit_length() - 1

    def body(blk, carry):
        for u in range(GATE_UNROLL):
            n = blk * GATE_UNROLL + u
            e = e_ref[pl.ds(n, 1), :]
            g = g_ref[pl.ds(n, 1), :].astype(BF16)
            e_first = (e >> shift).astype(F32).astype(BF16)
            e_second = (e & (N_KEYS - 1)).astype(F32).astype(BF16)
            first = jnp.where(key == e_first, one, zero)
            second = jnp.where(key == e_second, g, zero)
            w = lax.dot_general(first, second, nt, preferred_element_type=F32)
            o_ref[:, n] = w.reshape(N_KEYS // GATE_ROWS, GATE_ROWS, N_KEYS)
        return carry

    lax.fori_loop(0, rows // GATE_UNROLL, body, 0)


def _peer_gate(e_t, g_t, *, tb=64):
    m, npick = e_t.shape
    assert N_KEYS & (N_KEYS - 1) == 0
    tb = _tile(m, tb, GATE_UNROLL)
    nib = N_KEYS // GATE_ROWS
    return pl.pallas_call(
        functools.partial(_peer_gate_kernel, rows=tb),
        grid=(m // tb,),
        in_specs=[pl.BlockSpec((tb, npick), lambda i: (i, 0)),
                  pl.BlockSpec((tb, npick), lambda i: (i, 0))],
        out_specs=pl.BlockSpec((nib, tb, GATE_ROWS, N_KEYS), lambda i: (0, i, 0, 0)),
        out_shape=jax.ShapeDtypeStruct((nib, m, GATE_ROWS, N_KEYS), F32),
        compiler_params=_params("parallel"),
        name="peer_gate",
    )(e_t, g_t)


def _gelu_tanh(x):
    c = np.sqrt(2 / np.pi).astype(np.float32)
    return x * (0.5 * (1.0 + jnp.tanh(c * (x + 0.044715 * (x * x * x)))))


DENSE_PARTS = 3


def _peer_dense_kernel(xb_ref, xf_ref, wd_ref, ut_ref, v_ref, g_ref, b_ref,
                       of_ref, ob_ref, w_ref, *, ne, tn):
    e = pl.program_id(1)

    @pl.when(e == 0)
    def _():
        of_ref[...] = jnp.zeros_like(of_ref)

    rows_per_part = tn // DENSE_PARTS
    for part in range(DENSE_PARTS):
        rows = slice(part * rows_per_part, (part + 1) * rows_per_part)
        h = jnp.dot(xb_ref[rows, :], ut_ref[...], preferred_element_type=F32)
        for r in range(GATE_ROWS):
            lanes = slice(r * N_KEYS, (r + 1) * N_KEYS)
            gate = wd_ref[pl.ds(rows.start * GATE_ROWS + r, rows_per_part, stride=GATE_ROWS), :]
            w_ref[rows, lanes] = (gate * _gelu_tanh(h[:, lanes])).astype(BF16)
        of_ref[rows, :] += jnp.dot(w_ref[rows, :], v_ref[...], preferred_element_type=F32)

    @pl.when(e == ne - 1)
    def _():
        _write_ln(ALPHA * xf_ref[...] + of_ref[...], g_ref, b_ref, of_ref, ob_ref)


def _peer_dense(xb, xf, wd, u_t, v, layer, g, b, *, tn=528):
    m, d = xb.shape
    ne = wd.shape[0]
    te = GATE_ROWS * N_KEYS
    assert v.shape[1] == ne * te
    tn = _tile(m, tn, BF16_SUBLANE * DENSE_PARTS)
    row = lambda i, e: (i, 0)
    once = dict(pipeline_mode=pl.Buffered(1))
    return pl.pallas_call(
        functools.partial(_peer_dense_kernel, ne=ne, tn=tn),
        grid=(m // tn, ne),
        in_specs=[pl.BlockSpec((tn, d), row, **once), pl.BlockSpec((tn, d), row, **once),
                  pl.BlockSpec((None, tn * GATE_ROWS, N_KEYS), lambda i, e: (e, i, 0)),
                  pl.BlockSpec((None, d, te), lambda i, e: (layer, 0, e)),
                  pl.BlockSpec((None, te, d), lambda i, e: (layer, e, 0)),
                  _resident((1, d)), _resident((1, d))],
        out_specs=[pl.BlockSpec((tn, d), row), pl.BlockSpec((tn, d), row)],
        out_shape=[jax.ShapeDtypeStruct((m, d), F32), jax.ShapeDtypeStruct((m, d), BF16)],
        scratch_shapes=[pltpu.VMEM((tn, te), BF16)],
        compiler_params=_params("parallel", "arbitrary"),
        name="peer_dense",
    )(xb, xf, wd, u_t, v, g.reshape(1, d), b.reshape(1, d))


def _peer_layer(xf, xb, w_query, sub_keys, u_t_all, v_all, layer, ln_g, ln_b):
    m = xf.shape[0]
    npick = PEER_HEADS * PEER_TOPK
    (qb,) = _mm(xb, w_query.astype(BF16), [BF16])
    eidx, gates = _peer_topk(qb, sub_keys.astype(BF16))
    e_t = eidx.reshape(npick, m).T
    g_t = gates.reshape(npick, m).T
    wd = _peer_gate(e_t, g_t).reshape(N_KEYS // GATE_ROWS, m * GATE_ROWS, N_KEYS)
    return _peer_dense(xb, xf, wd, u_t_all, v_all, layer, ln_g, ln_b)


def _rope_tables(pos):
    half = MLA_ROPE // 2
    inv = ROPE_THETA ** (-jnp.arange(half, dtype=jnp.float32) / half)
    ang = pos.astype(jnp.float32)[:, None] * inv
    cos, sin = jnp.cos(ang), jnp.sin(ang)
    z = jnp.zeros_like(cos)
    return (jnp.concatenate([cos, cos, z, z], -1), jnp.concatenate([-sin, z, z, z], -1),
            jnp.concatenate([z, sin, z, z], -1))


def kernel(x_prompt, x_sample, cache_mla_ckv, cache_mla_krope, cache_diff_k, cache_diff_v,
           mla_w_dqkv, mla_g_q, mla_w_uq, mla_g_kv, mla_w_ukv, mla_w_o,
           diff_w_qkv, diff_lam_q1, diff_lam_k1, diff_lam_q2, diff_lam_k2, diff_g_sub, diff_w_o,
           peer_w_query, peer_sub_keys, peer_u, peer_v,
           ln_mix_g, ln_mix_b, ln_ffn_g, ln_ffn_b):
    bp, tp, d = x_prompt.shape
    bs, ts, _ = x_sample.shape
    past = cache_mla_ckv.shape[2]
    mp, ms = bp * tp, bs * ts
    tks = past + ts
    tkp = -(-tks // LANE) * LANE
    tq_p = _tile(tp, 512, LANE)

    def pad_keys(a):
        return jnp.pad(a, ((0, 0), (0, tkp - tks), (0, 0))).reshape(bs * tkp, a.shape[-1])

    u_t_all = peer_u.astype(BF16).swapaxes(1, 2)
    v_all = peer_v.astype(BF16)
    xf = jnp.concatenate([x_prompt.reshape(mp, d), x_sample.reshape(ms, d)], 0)
    xb = xf.astype(BF16)
    pos = jnp.concatenate([jnp.tile(jnp.arange(tp), bp), jnp.tile(past + jnp.arange(ts), bs)])
    cos, sin_a, sin_b = _rope_tables(pos)

    j = 0
    w_dqkv = jnp.pad(mla_w_dqkv[j], ((0, 0), (0, LANE - MLA_ROPE))).astype(BF16)
    cq, ckv_f, ckv_b, kr_f, kr_b = _mla_proj(xb, w_dqkv, mla_g_q[j], mla_g_kv[j], cos, sin_a, sin_b)
    hq = MLA_NOPE + MLA_ROPE
    w_uq = jnp.pad(mla_w_uq[j].reshape(MLA_Q_LORA, MLA_HEADS, hq),
                   ((0, 0), (0, 0), (0, 2 * LANE - hq))).reshape(MLA_Q_LORA, MLA_HEADS * 2 * LANE)
    q = _q_proj(cq, w_uq.astype(BF16), cos, sin_a, sin_b)
    w_ukv = mla_w_ukv[j].reshape(MLA_KV_LORA, MLA_HEADS, MLA_NOPE + MLA_V)
    w_uk_t = w_ukv[:, :, :MLA_NOPE].reshape(MLA_KV_LORA, -1).T.astype(BF16)
    w_uv = w_ukv[:, :, MLA_NOPE:].reshape(MLA_KV_LORA, -1).astype(BF16)
    lora_tiles = dict(tm=1024, tn=2048)
    (knt_p,) = _mm(w_uk_t, ckv_b[:mp].T, [BF16], **lora_tiles)
    (v_p,) = _mm(ckv_b[:mp], w_uv, [BF16], **lora_tiles)
    o_p = _mla_attention(q, knt_p, kr_b[:mp].T, v_p, batch=bp, t_q=tp, t_k=tp,
                         t_valid=tp, q_row0=0, q_off=0, tq=tq_p, tk=tq_p)
    ckv_cat = pad_keys(jnp.concatenate([cache_mla_ckv[j].astype(BF16),
                                        ckv_b[mp:].reshape(bs, ts, MLA_KV_LORA)], 1))
    kr_cache = jnp.pad(cache_mla_krope[j], ((0, 0), (0, 0), (0, LANE - MLA_ROPE))).astype(BF16)
    kr_cat = pad_keys(jnp.concatenate([kr_cache, kr_b[mp:].reshape(bs, ts, LANE)], 1))
    (knt_s,) = _mm(w_uk_t, ckv_cat.T, [BF16], **lora_tiles)
    (v_s,) = _mm(ckv_cat, w_uv, [BF16], **lora_tiles)
    o_s = _mla_attention(q, knt_s, kr_cat.T, v_s, batch=bs, t_q=ts, t_k=tkp,
                         t_valid=tks, q_row0=mp, q_off=past, tq=ts, tk=tkp)
    o = jnp.concatenate([o_p, o_s], 0)
    xf, xb = _mm_ln(o, mla_w_o[j].astype(BF16), xf, ln_mix_g[0], ln_mix_b[0])
    xf, xb = _peer_layer(xf, xb, peer_w_query[0], peer_sub_keys[0], u_t_all, v_all, 0,
                         ln_ffn_g[0], ln_ffn_b[0])

    i = 1
    lam_init = 0.8 - 0.6 * math.exp(-0.3 * i)
    nqk = DIFF_HEADS * 2 * DIFF_QK
    w_qkv = diff_w_qkv[j].astype(BF16)
    (dq,) = _mm(xb, w_qkv[:, :nqk], [BF16])
    dk_f, dk_b = _mm(xb, w_qkv[:, nqk:2 * nqk], [F32, BF16])
    dv_f, dv_b = _mm(xb, w_qkv[:, 2 * nqk:], [F32, BF16])
    slopes = 2.0 ** (-8.0 * jnp.arange(1, DIFF_HEADS + 1, dtype=jnp.float32) / DIFF_HEADS)
    lam_vecs = jnp.stack([diff_lam_q1[j], diff_lam_k1[j], diff_lam_q2[j], diff_lam_k2[j]])
    diff_args = (slopes, lam_vecs, diff_g_sub[j])
    o_p = _diff_attention(dq, dk_b[:mp].T, dv_b, *diff_args, k_transposed=True, batch=bp,
                          t_q=tp, t_k=tp, q_row0=0, q_off=0, tq=tq_p, tk=tq_p, lam_init=lam_init)
    k_cat = jnp.concatenate([cache_diff_k[j].reshape(bs, past, nqk).astype(BF16),
                             dk_b[mp:].reshape(bs, ts, nqk)], 1).reshape(bs * tks, nqk)
    v_cat = jnp.concatenate([cache_diff_v[j].reshape(bs, past, -1).astype(BF16),
                             dv_b[mp:].reshape(bs, ts, -1)], 1).reshape(bs * tks, -1)
    o_s = _diff_attention(dq, k_cat, v_cat, *diff_args, k_transposed=False, batch=bs,
                          t_q=ts, t_k=tks, q_row0=mp, q_off=past, tq=ts, tk=tks, lam_init=lam_init)
    o = jnp.concatenate([o_p, o_s], 0)
    xf, xb = _mm_ln(o, diff_w_o[j].astype(BF16), xf, ln_mix_g[1], ln_mix_b[1])
    xf, xb = _peer_layer(xf, xb, peer_w_query[1], peer_sub_keys[1], u_t_all, v_all, 1,
                         ln_ffn_g[1], ln_ffn_b[1])

    kr_f = kr_f[:, :MLA_ROPE]
    return (xf[:mp].reshape(bp, tp, d), xf[mp:].reshape(bs, ts, d),
            ckv_f[:mp].reshape(1, bp, tp, -1), kr_f[:mp].reshape(1, bp, tp, -1),
            dk_f[:mp].reshape(1, bp, tp, DIFF_HEADS, -1), dv_f[:mp].reshape(1, bp, tp, DIFF_HEADS, -1),
            ckv_f[mp:].reshape(1, bs, ts, -1), kr_f[mp:].reshape(1, bs, ts, -1),
            dk_f[mp:].reshape(1, bs, ts, DIFF_HEADS, -1), dv_f[mp:].reshape(1, bs, ts, DIFF_HEADS, -1))
```

```python
import functools
import math

import jax
import jax.numpy as jnp
import numpy as np
from jax import lax
from jax.experimental import pallas as pl
from jax.experimental.pallas import tpu as pltpu

F32 = jnp.float32
BF16 = jnp.bfloat16

DEPTH = 2
CHUNK = 64
ALPHA = (2 * DEPTH) ** 0.25
LN_EPS = 1e-5
RMS_EPS = 1e-6
NEG = -1e30
ROPE_THETA = 10000.0
MLA_HEADS = 16
MLA_Q_LORA = 512
MLA_KV_LORA = 512
MLA_NOPE = 128
MLA_ROPE = 64
MLA_V = 128
DIFF_HEADS = 8
DIFF_QK = 128
DIFF_V = 256
PEER_HEADS = 8
PEER_TOPK = 16
N_KEYS = 128
PEER_DKEY = 256

LANE = 128
BF16_SUBLANE = 16
VMEM_LIMIT = 56 << 20


def _tile(n, target, mult=BF16_SUBLANE):
    for t in range(min(n, target), 0, -1):
        if n % t == 0 and t % mult == 0:
            return t
    raise ValueError(f"no tile for {n} (target {target}, multiple of {mult})")


def _params(*sem):
    return pltpu.CompilerParams(dimension_semantics=sem, vmem_limit_bytes=VMEM_LIMIT)


def _layer_norm(y, g, b):
    mu = jnp.mean(y, axis=-1, keepdims=True)
    d = y - mu
    var = jnp.mean(d * d, axis=-1, keepdims=True)
    return d * lax.rsqrt(var + LN_EPS) * g + b


def _rms_norm(y, g):
    return y * lax.rsqrt(jnp.mean(y * y, axis=-1, keepdims=True) + RMS_EPS) * g


def _rope128(r, cos, sin_a, sin_b):
    return r * cos + pltpu.roll(r, 96, 1) * sin_a + pltpu.roll(r, 32, 1) * sin_b


def _mm_kernel(x_ref, w_ref, *outs):
    y = jnp.dot(x_ref[...], w_ref[...], preferred_element_type=F32)
    for o in outs:
        o[...] = y.astype(o.dtype)


def _mm(x, w, out_dtypes, *, tm=528, tn=1024):
    m, kdim = x.shape
    n = w.shape[1]
    tm, tn = _tile(m, tm), _tile(n, tn, LANE)
    outs = pl.pallas_call(
        _mm_kernel,
        grid=(m // tm, n // tn),
        in_specs=[pl.BlockSpec((tm, kdim), lambda i, j: (i, 0)),
                  pl.BlockSpec((kdim, tn), lambda i, j: (0, j))],
        out_specs=[pl.BlockSpec((tm, tn), lambda i, j: (i, j)) for _ in out_dtypes],
        out_shape=[jax.ShapeDtypeStruct((m, n), dt) for dt in out_dtypes],
        compiler_params=_params("parallel", "parallel"),
        name="mm",
    )(x, w)
    return outs


def _resident(shape):
    return pl.BlockSpec(shape, lambda *_: (0,) * len(shape), pipeline_mode=pl.Buffered(1))


def _write_ln(y, g_ref, b_ref, of_ref, ob_ref):
    y = _layer_norm(y, g_ref[...], b_ref[...])
    of_ref[...] = y
    ob_ref[...] = y.astype(BF16)


def _mm_ln_kernel(a_ref, w_ref, r_ref, g_ref, b_ref, of_ref, ob_ref):
    mix = jnp.dot(a_ref[...], w_ref[...], preferred_element_type=F32)
    _write_ln(ALPHA * r_ref[...] + mix, g_ref, b_ref, of_ref, ob_ref)


def _mm_ln(a, w, resid, g, b, *, tm=528):
    m, kdim = a.shape
    d = w.shape[1]
    tm = _tile(m, tm)
    row = lambda i: (i, 0)
    return pl.pallas_call(
        _mm_ln_kernel,
        grid=(m // tm,),
        in_specs=[pl.BlockSpec((tm, kdim), row), _resident((kdim, d)),
                  pl.BlockSpec((tm, d), row), _resident((1, d)), _resident((1, d))],
        out_specs=[pl.BlockSpec((tm, d), row), pl.BlockSpec((tm, d), row)],
        out_shape=[jax.ShapeDtypeStruct((m, d), F32), jax.ShapeDtypeStruct((m, d), BF16)],
        compiler_params=_params("parallel"),
        name="mm_ln",
    )(a, w, resid, g.reshape(1, d), b.reshape(1, d))


def _mla_proj_kernel(x_ref, w_ref, gq_ref, gkv_ref, cos_ref, sa_ref, sb_ref,
                     cq_ref, ckvf_ref, ckvb_ref, krf_ref, krb_ref, *, nq, nkv):
    lat = jnp.dot(x_ref[...], w_ref[...], preferred_element_type=F32)
    cq_ref[...] = _rms_norm(lat[:, :nq], gq_ref[...]).astype(BF16)
    ckv = _rms_norm(lat[:, nq:nq + nkv], gkv_ref[...])
    ckvf_ref[...] = ckv
    ckvb_ref[...] = ckv.astype(BF16)
    kr = _rope128(lat[:, nq + nkv:], cos_ref[...], sa_ref[...], sb_ref[...])
    krf_ref[...] = kr
    krb_ref[...] = kr.astype(BF16)


def _mla_proj(xb, w_pad, g_q, g_kv, cos, sin_a, sin_b, *, tm=528):
    m, kdim = xb.shape
    nq, nkv = g_q.shape[0], g_kv.shape[0]
    n = w_pad.shape[1]
    assert n == nq + nkv + LANE
    tm = _tile(m, tm)
    row = lambda i: (i, 0)
    return pl.pallas_call(
        functools.partial(_mla_proj_kernel, nq=nq, nkv=nkv),
        grid=(m // tm,),
        in_specs=[pl.BlockSpec((tm, kdim), row), _resident((kdim, n)),
                  _resident((1, nq)), _resident((1, nkv)),
                  pl.BlockSpec((tm, LANE), row), pl.BlockSpec((tm, LANE), row),
                  pl.BlockSpec((tm, LANE), row)],
        out_specs=[pl.BlockSpec((tm, nq), row), pl.BlockSpec((tm, nkv), row),
                   pl.BlockSpec((tm, nkv), row), pl.BlockSpec((tm, LANE), row),
                   pl.BlockSpec((tm, LANE), row)],
        out_shape=[jax.ShapeDtypeStruct((m, nq), BF16), jax.ShapeDtypeStruct((m, nkv), F32),
                   jax.ShapeDtypeStruct((m, nkv), BF16), jax.ShapeDtypeStruct((m, LANE), F32),
                   jax.ShapeDtypeStruct((m, LANE), BF16)],
        compiler_params=_params("parallel"),
        name="mla_proj",
    )(xb, w_pad, g_q.reshape(1, nq), g_kv.reshape(1, nkv), cos, sin_a, sin_b)


def _q_proj_kernel(c_ref, w_ref, cos_ref, sa_ref, sb_ref, o_ref, *, heads):
    y = jnp.dot(c_ref[...], w_ref[...], preferred_element_type=F32)
    for h in range(heads):
        lo = h * 2 * LANE
        o_ref[:, lo:lo + LANE] = y[:, lo:lo + LANE].astype(BF16)
        r = _rope128(y[:, lo + LANE:lo + 2 * LANE], cos_ref[...], sa_ref[...], sb_ref[...])
        o_ref[:, lo + LANE:lo + 2 * LANE] = r.astype(BF16)


def _q_proj(cq, w_arr, cos, sin_a, sin_b, *, tm=528, heads_per_step=8):
    m, kdim = cq.shape
    n = w_arr.shape[1]
    tm = _tile(m, tm)
    tn = heads_per_step * 2 * LANE
    row = lambda i, j: (i, 0)
    return pl.pallas_call(
        functools.partial(_q_proj_kernel, heads=heads_per_step),
        grid=(m // tm, n // tn),
        in_specs=[pl.BlockSpec((tm, kdim), row),
                  pl.BlockSpec((kdim, tn), lambda i, j: (0, j)),
                  pl.BlockSpec((tm, LANE), row), pl.BlockSpec((tm, LANE), row),
                  pl.BlockSpec((tm, LANE), row)],
        out_specs=pl.BlockSpec((tm, tn), lambda i, j: (i, j)),
        out_shape=jax.ShapeDtypeStruct((m, n), BF16),
        compiler_params=_params("parallel", "parallel"),
        name="q_proj",
    )(cq, w_arr, cos, sin_a, sin_b)


Q_CHAIN_ROWS = 256


def _row_parts(tq):
    n = max(1, tq // Q_CHAIN_ROWS)
    assert tq % n == 0
    return [slice(p * (tq // n), (p + 1) * (tq // n)) for p in range(n)]


def _visible_blocks(qi, *, q_off, tq, tk, nk):
    q_first = q_off + qi * tq
    q_last = q_first + tq - 1
    n_before = jnp.minimum(q_first // tk, nk)
    n_vis = jnp.minimum(((q_last // CHUNK + 1) * CHUNK + tk - 1) // tk, nk)
    return n_before, n_vis


def _positions(qi, ki, rows, *, q_off, tq, tk):
    shape = (rows.stop - rows.start, tk)
    qp = q_off + qi * tq + rows.start + lax.broadcasted_iota(jnp.int32, shape, 0)
    kp = ki * tk + lax.broadcasted_iota(jnp.int32, shape, 1)
    return qp, kp


def _key_mask(qi, ki, rows, geo):
    qp, kp = _positions(qi, ki, rows, q_off=geo["q_off"], tq=geo["tq"], tk=geo["tk"])
    k_chunk = kp // CHUNK
    if geo["t_valid"] < geo["nk"] * geo["tk"]:
        k_chunk = jnp.where(kp < geo["t_valid"], k_chunk, jnp.iinfo(jnp.int32).max)
    return k_chunk <= (qp // CHUNK), qp, kp


def _for_blocks(lo, hi, step):
    lax.fori_loop(lo, hi, lambda ki, carry: (step(ki), carry)[1], 0)


def _online_softmax_step(s, v, m_ref, l_ref, acc_ref, rows):
    m_prev = m_ref[rows, :]
    m_new = jnp.maximum(m_prev, jnp.max(s, axis=-1, keepdims=True))
    alpha = jnp.exp(m_prev - m_new)
    p = jnp.exp(s - m_new)
    l_ref[rows, :] = alpha * l_ref[rows, :] + jnp.sum(p, axis=-1, keepdims=True)
    acc_ref[rows, :] = (alpha * acc_ref[rows, :]
                        + jnp.dot(p.astype(BF16), v, preferred_element_type=F32))
    m_ref[rows, :] = m_new


def _mla_attn_kernel(q_ref, knt_ref, krt_ref, v_ref, o_ref, kcat_t, m_ref, l_ref, acc_ref,
                     *, geo, scale):
    qi = pl.program_id(2)
    tk = geo["tk"]

    @pl.when(qi == 0)
    def _():
        kcat_t[:MLA_NOPE, :] = knt_ref[...]
        kcat_t[MLA_NOPE:, :] = krt_ref[...]

    m_ref[...] = jnp.full_like(m_ref, -jnp.inf)
    l_ref[...] = jnp.zeros_like(l_ref)
    acc_ref[...] = jnp.zeros_like(acc_ref)

    def step(ki, masked):
        r0 = pl.multiple_of(ki * tk, tk)
        k_t = kcat_t[:, pl.ds(r0, tk)]
        v = v_ref[pl.ds(r0, tk), :]
        for rows in _row_parts(geo["tq"]):
            s = jnp.dot(q_ref[rows, :], k_t, preferred_element_type=F32) * scale
            if masked:
                s = jnp.where(_key_mask(qi, ki, rows, geo)[0], s, NEG)
            _online_softmax_step(s, v, m_ref, l_ref, acc_ref, rows)

    n_before, n_vis = _visible_blocks(qi, q_off=geo["q_off"], tq=geo["tq"], tk=tk, nk=geo["nk"])
    _for_blocks(0, n_before, functools.partial(step, masked=False))
    _for_blocks(n_before, n_vis, functools.partial(step, masked=True))
    o_ref[...] = (acc_ref[...] / l_ref[...]).astype(BF16)


def _mla_attention(q2d, knt, krt, v2d, *, batch, t_q, t_k, t_valid, q_row0, q_off, tq, tk):
    nq, nk = t_q // tq, t_k // tk
    assert q_row0 % tq == 0 and t_k % tk == 0 and tk % LANE == 0
    geo = dict(q_off=q_off, tq=tq, tk=tk, nk=nk, t_valid=t_valid)
    qb0 = q_row0 // tq
    return pl.pallas_call(
        functools.partial(_mla_attn_kernel, geo=geo, scale=(MLA_NOPE + MLA_ROPE) ** -0.5),
        grid=(batch, MLA_HEADS, nq),
        in_specs=[pl.BlockSpec((tq, 2 * LANE), lambda b, h, qi: (qb0 + b * nq + qi, h)),
                  pl.BlockSpec((MLA_NOPE, t_k), lambda b, h, qi: (h, b)),
                  pl.BlockSpec((LANE, t_k), lambda b, h, qi: (0, b)),
                  pl.BlockSpec((t_k, MLA_V), lambda b, h, qi: (b, h))],
        out_specs=pl.BlockSpec((tq, MLA_V), lambda b, h, qi: (b * nq + qi, h)),
        out_shape=jax.ShapeDtypeStruct((batch * t_q, MLA_HEADS * MLA_V), BF16),
        scratch_shapes=[pltpu.VMEM((2 * LANE, t_k), BF16),
                        pltpu.VMEM((tq, 1), F32), pltpu.VMEM((tq, 1), F32),
                        pltpu.VMEM((tq, MLA_V), F32)],
        compiler_params=_params("parallel", "parallel", "arbitrary"),
        name="mla_attn",
    )(q2d, knt, krt, v2d)


def _diff_attn_kernel(slope_ref, lam_ref, gsub_ref, q_ref, k_ref, v_ref, o_ref,
                      b0_ref, m1, l1, a1, m2, l2, a2, *, geo, scale, lam_init, k_transposed):
    h, qi = pl.program_id(1), pl.program_id(2)
    tq, tk = geo["tq"], geo["tk"]
    for m, l, a in ((m1, l1, a1), (m2, l2, a2)):
        m[...] = jnp.full_like(m, -jnp.inf)
        l[...] = jnp.zeros_like(l)
        a[...] = jnp.zeros_like(a)
    slope = slope_ref[h]
    q_first = geo["q_off"] + qi * tq
    b0_ref[...] = -slope * (lax.broadcasted_iota(jnp.int32, (tq, tk), 0)
                            - lax.broadcasted_iota(jnp.int32, (tq, tk), 1)).astype(F32)

    def scores(q_half, r0, lo):
        if k_transposed:
            return jnp.dot(q_half, k_ref[lo:lo + DIFF_QK, pl.ds(r0, tk)],
                           preferred_element_type=F32)
        return lax.dot_general(q_half, k_ref[pl.ds(r0, tk), lo:lo + DIFF_QK],
                               (((1,), (1,)), ((), ())), preferred_element_type=F32)

    def step(ki, masked):
        r0 = pl.multiple_of(ki * tk, tk)
        v = v_ref[pl.ds(r0, tk), :]
        for rows in _row_parts(tq):
            if masked:
                mask, qp, kp = _key_mask(qi, ki, rows, geo)
                bias = jnp.where(mask, -slope * jnp.abs(qp - kp).astype(F32), NEG)
            else:
                shift = -slope * jnp.full((1, tk), q_first - r0, jnp.int32).astype(F32)
                bias = b0_ref[rows, :] + shift
            s1 = scores(q_ref[rows, :DIFF_QK], r0, 0)
            _online_softmax_step(s1 * scale + bias, v, m1, l1, a1, rows)
            s2 = scores(q_ref[rows, DIFF_QK:], r0, DIFF_QK)
            _online_softmax_step(s2 * scale + bias, v, m2, l2, a2, rows)

    n_before, n_vis = _visible_blocks(qi, q_off=geo["q_off"], tq=tq, tk=tk, nk=geo["nk"])
    _for_blocks(0, n_before, functools.partial(step, masked=False))
    _for_blocks(n_before, n_vis, functools.partial(step, masked=True))

    lam_v = lam_ref[...]
    lam = (jnp.exp(jnp.sum(lam_v[0:1] * lam_v[1:2], axis=-1, keepdims=True))
           - jnp.exp(jnp.sum(lam_v[2:3] * lam_v[3:4], axis=-1, keepdims=True)) + lam_init)
    o = a1[...] / l1[...] - lam * (a2[...] / l2[...])
    o_ref[...] = (_rms_norm(o, gsub_ref[...]) * (1.0 - lam_init)).astype(BF16)


def _diff_attention(q2d, k, v2d, slopes, lam_vecs, g_sub, *, k_transposed,
                    batch, t_q, t_k, q_row0, q_off, tq, tk, lam_init):
    nq, nk = t_q // tq, t_k // tk
    assert q_row0 % tq == 0 and t_k % tk == 0 and (tk % LANE == 0 or not k_transposed)
    geo = dict(q_off=q_off, tq=tq, tk=tk, nk=nk, t_valid=t_k)
    qb0 = q_row0 // tq
    hd = 2 * DIFF_QK
    k_spec = (pl.BlockSpec((hd, t_k), lambda b, h, qi: (h, b)) if k_transposed
              else pl.BlockSpec((t_k, hd), lambda b, h, qi: (b, h)))
    return pl.pallas_call(
        functools.partial(_diff_attn_kernel, geo=geo, scale=DIFF_QK ** -0.5, lam_init=lam_init,
                          k_transposed=k_transposed),
        grid=(batch, DIFF_HEADS, nq),
        in_specs=[pl.BlockSpec(memory_space=pltpu.SMEM),
                  pl.BlockSpec((4, DIFF_QK), lambda b, h, qi: (0, 0)),
                  pl.BlockSpec((1, DIFF_V), lambda b, h, qi: (0, 0)),
                  pl.BlockSpec((tq, hd), lambda b, h, qi: (qb0 + b * nq + qi, h)),
                  k_spec,
                  pl.BlockSpec((t_k, DIFF_V), lambda b, h, qi: (b, h))],
        out_specs=pl.BlockSpec((tq, DIFF_V), lambda b, h, qi: (b * nq + qi, h)),
        out_shape=jax.ShapeDtypeStruct((batch * t_q, DIFF_HEADS * DIFF_V), BF16),
        scratch_shapes=[pltpu.VMEM((tq, tk), F32),
                        pltpu.VMEM((tq, 1), F32), pltpu.VMEM((tq, 1), F32),
                        pltpu.VMEM((tq, DIFF_V), F32),
                        pltpu.VMEM((tq, 1), F32), pltpu.VMEM((tq, 1), F32),
                        pltpu.VMEM((tq, DIFF_V), F32)],
        compiler_params=_params("parallel", "parallel", "parallel"),
        name="diff_attn",
    )(slopes, lam_vecs, g_sub.reshape(1, DIFF_V), q2d, k, v2d)


def _top_rows(s, n_rows, k, payload=None):
    pos = lax.broadcasted_iota(jnp.int32, s.shape, 0).astype(F32)
    vals, idxs, pays = [], [], []
    for _ in range(k):
        m = jnp.max(s, axis=0, keepdims=True)
        sel = jnp.min(jnp.where(s == m, pos, float(n_rows)), axis=0, keepdims=True)
        hit = pos == sel
        vals.append(m)
        idxs.append(sel)
        if payload is not None:
            pays.append(jnp.max(jnp.where(hit, payload, -1.0), axis=0, keepdims=True))
        s = jnp.where(hit, -jnp.inf, s)
    cat = lambda xs: jnp.concatenate(xs, axis=0)
    return cat(vals), cat(idxs), (cat(pays) if payload is not None else None)


F32_SUBLANE = 8


def _batcher_pairs(n):
    pairs = []

    def merge(lo, cnt, r):
        step = r * 2
        if step < cnt:
            merge(lo, cnt, step)
            merge(lo + r, cnt, step)
            pairs.extend((i, i + r) for i in range(lo + r, lo + cnt - r, step))
        else:
            pairs.append((lo, lo + r))

    def sort(lo, cnt):
        if cnt > 1:
            sort(lo, cnt // 2)
            sort(lo + cnt // 2, cnt // 2)
            merge(lo, cnt, 1)

    sort(0, n)
    return pairs


def _top_rows_by_merge(s, k):
    n_rows, cols = s.shape
    depth = n_rows // F32_SUBLANE
    assert depth * F32_SUBLANE == n_rows and depth >= k
    sub = lax.broadcasted_iota(jnp.int32, (F32_SUBLANE, cols), 0).astype(F32)
    vals = [s[r * F32_SUBLANE:(r + 1) * F32_SUBLANE] for r in range(depth)]
    idxs = [sub + float(r * F32_SUBLANE) for r in range(depth)]
    for a, b in _batcher_pairs(depth):
        swap = vals[b] > vals[a]
        vals[a], vals[b] = jnp.maximum(vals[a], vals[b]), jnp.minimum(vals[a], vals[b])
        idxs[a], idxs[b] = jnp.where(swap, idxs[b], idxs[a]), jnp.where(swap, idxs[a], idxs[b])
    tie = jnp.zeros_like(sub)
    for r in range(depth - 1):
        tie = jnp.where(vals[r] == vals[r + 1], 1.0, tie)
    out_v, out_i = [], []
    for j in range(k):
        m = jnp.max(vals[0], axis=0, keepdims=True)
        sel = jnp.min(jnp.where(vals[0] == m, idxs[0], float(n_rows)), axis=0, keepdims=True)
        hit = idxs[0] == sel
        out_v.append(m)
        out_i.append(sel)
        for r in range(depth - 1 - j):
            vals[r] = jnp.where(hit, vals[r + 1], vals[r])
            idxs[r] = jnp.where(hit, idxs[r + 1], idxs[r])
    return jnp.concatenate(out_v, axis=0), jnp.concatenate(out_i, axis=0), jnp.max(tie)


def _peer_topk_kernel(q_ref, sk_ref, e_ref, g_ref):
    half = PEER_DKEY // 2
    q = q_ref[...]
    nt = (((1,), (1,)), ((), ()))
    scores = [lax.dot_general(sk_ref[c], q[:, c * half:(c + 1) * half], nt,
                              preferred_element_type=F32) for c in range(2)]
    fast = [_top_rows_by_merge(s, PEER_TOPK) for s in scores]
    _peer_pick(q.shape[0], [f[0] for f in fast], [f[1] for f in fast], e_ref, g_ref)

    @pl.when(jnp.maximum(fast[0][2], fast[1][2]) > 0.0)
    def _():
        slow = [_top_rows(s, N_KEYS, PEER_TOPK) for s in scores]
        _peer_pick(q.shape[0], [f[0] for f in slow], [f[1] for f in slow], e_ref, g_ref)


def _peer_pick(tokens, sv, si, e_ref, g_ref):
    width = [PEER_TOPK // (a + 1) for a in range(PEER_TOPK)]
    n_cand = sum(width)
    pad = -n_cand % 8
    cand = jnp.concatenate([sv[0][a:a + 1] + sv[1][:width[a]] for a in range(PEER_TOPK)]
                           + [jnp.full((pad, tokens), -jnp.inf, F32)], axis=0)
    cidx = jnp.concatenate([si[0][a:a + 1] * float(N_KEYS) + si[1][:width[a]]
                            for a in range(PEER_TOPK)]
                           + [jnp.full((pad, tokens), -1.0, F32)], axis=0)
    fv, _, fe = _top_rows(cand, n_cand + pad, PEER_TOPK, payload=cidx)
    p = jnp.exp(fv - fv[0:1])
    g_ref[...] = p / jnp.sum(p, axis=0, keepdims=True)
    e_ref[...] = fe.astype(jnp.int32)


def _peer_topk(qb, sub_keys_b, *, tn=256):
    m = qb.shape[0]
    tn = _tile(m, tn, LANE)
    half = PEER_DKEY // 2
    out_spec = pl.BlockSpec((None, PEER_TOPK, tn), lambda i, h: (h, 0, i))
    return pl.pallas_call(
        _peer_topk_kernel,
        grid=(m // tn, PEER_HEADS),
        in_specs=[pl.BlockSpec((tn, PEER_DKEY), lambda i, h: (i, h)),
                  pl.BlockSpec((None, 2, N_KEYS, half), lambda i, h: (h, 0, 0, 0))],
        out_specs=[out_spec, out_spec],
        out_shape=[jax.ShapeDtypeStruct((PEER_HEADS, PEER_TOPK, m), jnp.int32),
                   jax.ShapeDtypeStruct((PEER_HEADS, PEER_TOPK, m), F32)],
        compiler_params=_params("parallel", "parallel"),
        name="peer_topk",
    )(qb, sub_keys_b)


GATE_ROWS = 8
GATE_UNROLL = 32


def _peer_gate_kernel(e_ref, g_ref, o_ref, *, rows):
    npick = PEER_HEADS * PEER_TOPK
    key = lax.broadcasted_iota(jnp.int32, (N_KEYS, npick), 0).astype(F32).astype(BF16)
    one, zero = jnp.ones((), BF16), jnp.zeros((), BF16)
    nt = (((1,), (1,)), ((), ()))
    shift = N_KEYS.bit_length() - 1

    def body(blk, carry):
        for u in range(GATE_UNROLL):
            n = blk * GATE_UNROLL + u
            e = e_ref[pl.ds(n, 1), :]
            g = g_ref[pl.ds(n, 1), :].astype(BF16)
            e_first = (e >> shift).astype(F32).astype(BF16)
            e_second = (e & (N_KEYS - 1)).astype(F32).astype(BF16)
            first = jnp.where(key == e_first, one, zero)
            second = jnp.where(key == e_second, g, zero)
            w = lax.dot_general(first, second, nt, preferred_element_type=F32)
            o_ref[:, n] = w.reshape(N_KEYS // GATE_ROWS, GATE_ROWS, N_KEYS)
        return carry

    lax.fori_loop(0, rows // GATE_UNROLL, body, 0)


def _peer_gate(e_t, g_t, *, tb=64):
    m, npick = e_t.shape
    assert N_KEYS & (N_KEYS - 1) == 0
    tb = _tile(m, tb, GATE_UNROLL)
    nib = N_KEYS // GATE_ROWS
    return pl.pallas_call(
        functools.partial(_peer_gate_kernel, rows=tb),
        grid=(m // tb,),
        in_specs=[pl.BlockSpec((tb, npick), lambda i: (i, 0)),
                  pl.BlockSpec((tb, npick), lambda i: (i, 0))],
        out_specs=pl.BlockSpec((nib, tb, GATE_ROWS, N_KEYS), lambda i: (0, i, 0, 0)),
        out_shape=jax.ShapeDtypeStruct((nib, m, GATE_ROWS, N_KEYS), F32),
        compiler_params=_params("parallel"),
        name="peer_gate",
    )(e_t, g_t)


def _gelu_tanh(x):
    c = np.sqrt(2 / np.pi).astype(np.float32)
    return x * (0.5 * (1.0 + jnp.tanh(c * (x + 0.044715 * (x * x * x)))))


DENSE_PARTS = 1


def _peer_dense_kernel(xb_ref, xf_ref, wd_ref, ut_ref, v_ref, g_ref, b_ref,
                       of_ref, ob_ref, w_ref, *, ne, tn):
    e = pl.program_id(1)

    @pl.when(e == 0)
    def _():
        of_ref[...] = jnp.zeros_like(of_ref)

    rows_per_part = tn // DENSE_PARTS
    for part in range(DENSE_PARTS):
        rows = slice(part * rows_per_part, (part + 1) * rows_per_part)
        h = jnp.dot(xb_ref[rows, :], ut_ref[...], preferred_element_type=F32)
        for r in range(GATE_ROWS):
            lanes = slice(r * N_KEYS, (r + 1) * N_KEYS)
            gate = wd_ref[pl.ds(rows.start * GATE_ROWS + r, rows_per_part, stride=GATE_ROWS), :]
            w_ref[rows, lanes] = (gate * _gelu_tanh(h[:, lanes])).astype(BF16)
        of_ref[rows, :] += jnp.dot(w_ref[rows, :], v_ref[...], preferred_element_type=F32)

    @pl.when(e == ne - 1)
    def _():
        _write_ln(ALPHA * xf_ref[...] + of_ref[...], g_ref, b_ref, of_ref, ob_ref)


def _peer_dense(xb, xf, wd, u_t, v, layer, g, b, *, tn=528):
    m, d = xb.shape
    ne = wd.shape[0]
    te = GATE_ROWS * N_KEYS
    assert v.shape[1] == ne * te
    tn = _tile(m, tn, BF16_SUBLANE * DENSE_PARTS)
    row = lambda i, e: (i, 0)
    return pl.pallas_call(
        functools.partial(_peer_dense_kernel, ne=ne, tn=tn),
        grid=(m // tn, ne),
        in_specs=[pl.BlockSpec((tn, d), row), pl.BlockSpec((tn, d), row),
                  pl.BlockSpec((None, tn * GATE_ROWS, N_KEYS), lambda i, e: (e, i, 0)),
                  pl.BlockSpec((None, d, te), lambda i, e: (layer, 0, e)),
                  pl.BlockSpec((None, te, d), lambda i, e: (layer, e, 0)),
                  _resident((1, d)), _resident((1, d))],
        out_specs=[pl.BlockSpec((tn, d), row), pl.BlockSpec((tn, d), row)],
        out_shape=[jax.ShapeDtypeStruct((m, d), F32), jax.ShapeDtypeStruct((m, d), BF16)],
        scratch_shapes=[pltpu.VMEM((tn, te), BF16)],
        compiler_params=_params("parallel", "arbitrary"),
        name="peer_dense",
    )(xb, xf, wd, u_t, v, g.reshape(1, d), b.reshape(1, d))


def _peer_layer(xf, xb, w_query, sub_keys, u_t_all, v_all, layer, ln_g, ln_b):
    m = xf.shape[0]
    npick = PEER_HEADS * PEER_TOPK
    (qb,) = _mm(xb, w_query.astype(BF16), [BF16])
    eidx, gates = _peer_topk(qb, sub_keys.astype(BF16))
    e_t = eidx.reshape(npick, m).T
    g_t = gates.reshape(npick, m).T
    wd = _peer_gate(e_t, g_t).reshape(N_KEYS // GATE_ROWS, m * GATE_ROWS, N_KEYS)
    return _peer_dense(xb, xf, wd, u_t_all, v_all, layer, ln_g, ln_b)


def _rope_tables(pos):
    half = MLA_ROPE // 2
    inv = ROPE_THETA ** (-jnp.arange(half, dtype=jnp.float32) / half)
    ang = pos.astype(jnp.float32)[:, None] * inv
    cos, sin = jnp.cos(ang), jnp.sin(ang)
    z = jnp.zeros_like(cos)
    return (jnp.concatenate([cos, cos, z, z], -1), jnp.concatenate([-sin, z, z, z], -1),
            jnp.concatenate([z, sin, z, z], -1))


def kernel(x_prompt, x_sample, cache_mla_ckv, cache_mla_krope, cache_diff_k, cache_diff_v,
           mla_w_dqkv, mla_g_q, mla_w_uq, mla_g_kv, mla_w_ukv, mla_w_o,
           diff_w_qkv, diff_lam_q1, diff_lam_k1, diff_lam_q2, diff_lam_k2, diff_g_sub, diff_w_o,
           peer_w_query, peer_sub_keys, peer_u, peer_v,
           ln_mix_g, ln_mix_b, ln_ffn_g, ln_ffn_b):
    bp, tp, d = x_prompt.shape
    bs, ts, _ = x_sample.shape
    past = cache_mla_ckv.shape[2]
    mp, ms = bp * tp, bs * ts
    tks = past + ts
    tkp = -(-tks // LANE) * LANE
    tq_p = _tile(tp, 512, LANE)

    def pad_keys(a):
        return jnp.pad(a, ((0, 0), (0, tkp - tks), (0, 0))).reshape(bs * tkp, a.shape[-1])

    u_t_all = peer_u.astype(BF16).swapaxes(1, 2)
    v_all = peer_v.astype(BF16)
    xf = jnp.concatenate([x_prompt.reshape(mp, d), x_sample.reshape(ms, d)], 0)
    xb = xf.astype(BF16)
    pos = jnp.concatenate([jnp.tile(jnp.arange(tp), bp), jnp.tile(past + jnp.arange(ts), bs)])
    cos, sin_a, sin_b = _rope_tables(pos)

    j = 0
    w_dqkv = jnp.pad(mla_w_dqkv[j], ((0, 0), (0, LANE - MLA_ROPE))).astype(BF16)
    cq, ckv_f, ckv_b, kr_f, kr_b = _mla_proj(xb, w_dqkv, mla_g_q[j], mla_g_kv[j], cos, sin_a, sin_b)
    hq = MLA_NOPE + MLA_ROPE
    w_uq = jnp.pad(mla_w_uq[j].reshape(MLA_Q_LORA, MLA_HEADS, hq),
                   ((0, 0), (0, 0), (0, 2 * LANE - hq))).reshape(MLA_Q_LORA, MLA_HEADS * 2 * LANE)
    q = _q_proj(cq, w_uq.astype(BF16), cos, sin_a, sin_b)
    w_ukv = mla_w_ukv[j].reshape(MLA_KV_LORA, MLA_HEADS, MLA_NOPE + MLA_V)
    w_uk_t = w_ukv[:, :, :MLA_NOPE].reshape(MLA_KV_LORA, -1).T.astype(BF16)
    w_uv = w_ukv[:, :, MLA_NOPE:].reshape(MLA_KV_LORA, -1).astype(BF16)
    lora_tiles = dict(tm=1024, tn=2048)
    (knt_p,) = _mm(w_uk_t, ckv_b[:mp].T, [BF16], **lora_tiles)
    (v_p,) = _mm(ckv_b[:mp], w_uv, [BF16], **lora_tiles)
    o_p = _mla_attention(q, knt_p, kr_b[:mp].T, v_p, batch=bp, t_q=tp, t_k=tp,
                         t_valid=tp, q_row0=0, q_off=0, tq=tq_p, tk=tq_p)
    ckv_cat = pad_keys(jnp.concatenate([cache_mla_ckv[j].astype(BF16),
                                        ckv_b[mp:].reshape(bs, ts, MLA_KV_LORA)], 1))
    kr_cache = jnp.pad(cache_mla_krope[j], ((0, 0), (0, 0), (0, LANE - MLA_ROPE))).astype(BF16)
    kr_cat = pad_keys(jnp.concatenate([kr_cache, kr_b[mp:].reshape(bs, ts, LANE)], 1))
    (knt_s,) = _mm(w_uk_t, ckv_cat.T, [BF16], **lora_tiles)
    (v_s,) = _mm(ckv_cat, w_uv, [BF16], **lora_tiles)
    o_s = _mla_attention(q, knt_s, kr_cat.T, v_s, batch=bs, t_q=ts, t_k=tkp,
                         t_valid=tks, q_row0=mp, q_off=past, tq=ts, tk=tkp)
    o = jnp.concatenate([o_p, o_s], 0)
    xf, xb = _mm_ln(o, mla_w_o[j].astype(BF16), xf, ln_mix_g[0], ln_mix_b[0])
    xf, xb = _peer_layer(xf, xb, peer_w_query[0], peer_sub_keys[0], u_t_all, v_all, 0,
                         ln_ffn_g[0], ln_ffn_b[0])

    i = 1
    lam_init = 0.8 - 0.6 * math.exp(-0.3 * i)
    nqk = DIFF_HEADS * 2 * DIFF_QK
    w_qkv = diff_w_qkv[j].astype(BF16)
    (dq,) = _mm(xb, w_qkv[:, :nqk], [BF16])
    dk_f, dk_b = _mm(xb, w_qkv[:, nqk:2 * nqk], [F32, BF16])
    dv_f, dv_b = _mm(xb, w_qkv[:, 2 * nqk:], [F32, BF16])
    slopes = 2.0 ** (-8.0 * jnp.arange(1, DIFF_HEADS + 1, dtype=jnp.float32) / DIFF_HEADS)
    lam_vecs = jnp.stack([diff_lam_q1[j], diff_lam_k1[j], diff_lam_q2[j], diff_lam_k2[j]])
    diff_args = (slopes, lam_vecs, diff_g_sub[j])
    o_p = _diff_attention(dq, dk_b[:mp].T, dv_b, *diff_args, k_transposed=True, batch=bp,
                          t_q=tp, t_k=tp, q_row0=0, q_off=0, tq=tq_p, tk=tq_p, lam_init=lam_init)
    k_cat = jnp.concatenate([cache_diff_k[j].reshape(bs, past, nqk).astype(BF16),
                             dk_b[mp:].reshape(bs, ts, nqk)], 1).reshape(bs * tks, nqk)
    v_cat = jnp.concatenate([cache_diff_v[j].reshape(bs, past, -1).astype(BF16),
                             dv_b[mp:].reshape(bs, ts, -1)], 1).reshape(bs * tks, -1)
    o_s = _diff_attention(dq, k_cat, v_cat, *diff_args, k_transposed=False, batch=bs,
                          t_q=ts, t_k=tks, q_row0=mp, q_off=past, tq=ts, tk=tks, lam_init=lam_init)
    o = jnp.concatenate([o_p, o_s], 0)
    xf, xb = _mm_ln(o, diff_w_o[j].astype(BF16), xf, ln_mix_g[1], ln_mix_b[1])
    xf, xb = _peer_layer(xf, xb, peer_w_query[1], peer_sub_keys[1], u_t_all, v_all, 1,
                         ln_ffn_g[1], ln_ffn_b[1])

    kr_f = kr_f[:, :MLA_ROPE]
    return (xf[:mp].reshape(bp, tp, d), xf[mp:].reshape(bs, ts, d),
            ckv_f[:mp].reshape(1, bp, tp, -1), kr_f[:mp].reshape(1, bp, tp, -1),
            dk_f[:mp].reshape(1, bp, tp, DIFF_HEADS, -1), dv_f[:mp].reshape(1, bp, tp, DIFF_HEADS, -1),
            ckv_f[mp:].reshape(1, bs, ts, -1), kr_f[mp:].reshape(1, bs, ts, -1),
            dk_f[mp:].reshape(1, bs, ts, DIFF_HEADS, -1), dv_f[mp:].reshape(1, bs, ts, DIFF_HEADS, -1))
```

```python
import functools
import math

import jax
import jax.numpy as jnp
import numpy as np
from jax import lax
from jax.experimental import pallas as pl
from jax.experimental.pallas import tpu as pltpu

F32 = jnp.float32
BF16 = jnp.bfloat16

DEPTH = 2
CHUNK = 64
ALPHA = (2 * DEPTH) ** 0.25
LN_EPS = 1e-5
RMS_EPS = 1e-6
NEG = -1e30
ROPE_THETA = 10000.0
MLA_HEADS = 16
MLA_Q_LORA = 512
MLA_KV_LORA = 512
MLA_NOPE = 128
MLA_ROPE = 64
MLA_V = 128
DIFF_HEADS = 8
DIFF_QK = 128
DIFF_V = 256
PEER_HEADS = 8
PEER_TOPK = 16
N_KEYS = 128
PEER_DKEY = 256

LANE = 128
BF16_SUBLANE = 16
VMEM_LIMIT = 56 << 20


def _tile(n, target, mult=BF16_SUBLANE):
    for t in range(min(n, target), 0, -1):
        if n % t == 0 and t % mult == 0:
            return t
    raise ValueError(f"no tile for {n} (target {target}, multiple of {mult})")


def _params(*sem):
    return pltpu.CompilerParams(dimension_semantics=sem, vmem_limit_bytes=VMEM_LIMIT)


def _layer_norm(y, g, b):
    mu = jnp.mean(y, axis=-1, keepdims=True)
    d = y - mu
    var = jnp.mean(d * d, axis=-1, keepdims=True)
    return d * lax.rsqrt(var + LN_EPS) * g + b


def _rms_norm(y, g):
    return y * lax.rsqrt(jnp.mean(y * y, axis=-1, keepdims=True) + RMS_EPS) * g


def _rope128(r, cos, sin_a, sin_b):
    return r * cos + pltpu.roll(r, 96, 1) * sin_a + pltpu.roll(r, 32, 1) * sin_b


def _mm_kernel(x_ref, w_ref, *outs):
    y = jnp.dot(x_ref[...], w_ref[...], preferred_element_type=F32)
    for o in outs:
        o[...] = y.astype(o.dtype)


def _mm(x, w, out_dtypes, *, tm=528, tn=1024):
    m, kdim = x.shape
    n = w.shape[1]
    tm, tn = _tile(m, tm), _tile(n, tn, LANE)
    outs = pl.pallas_call(
        _mm_kernel,
        grid=(m // tm, n // tn),
        in_specs=[pl.BlockSpec((tm, kdim), lambda i, j: (i, 0)),
                  pl.BlockSpec((kdim, tn), lambda i, j: (0, j))],
        out_specs=[pl.BlockSpec((tm, tn), lambda i, j: (i, j)) for _ in out_dtypes],
        out_shape=[jax.ShapeDtypeStruct((m, n), dt) for dt in out_dtypes],
        compiler_params=_params("parallel", "parallel"),
        name="mm",
    )(x, w)
    return outs


def _resident(shape):
    return pl.BlockSpec(shape, lambda *_: (0,) * len(shape), pipeline_mode=pl.Buffered(1))


def _write_ln(y, g_ref, b_ref, of_ref, ob_ref):
    y = _layer_norm(y, g_ref[...], b_ref[...])
    of_ref[...] = y
    ob_ref[...] = y.astype(BF16)


def _mm_ln_kernel(a_ref, w_ref, r_ref, g_ref, b_ref, of_ref, ob_ref):
    mix = jnp.dot(a_ref[...], w_ref[...], preferred_element_type=F32)
    _write_ln(ALPHA * r_ref[...] + mix, g_ref, b_ref, of_ref, ob_ref)


def _mm_ln(a, w, resid, g, b, *, tm=528):
    m, kdim = a.shape
    d = w.shape[1]
    tm = _tile(m, tm)
    row = lambda i: (i, 0)
    return pl.pallas_call(
        _mm_ln_kernel,
        grid=(m // tm,),
        in_specs=[pl.BlockSpec((tm, kdim), row), _resident((kdim, d)),
                  pl.BlockSpec((tm, d), row), _resident((1, d)), _resident((1, d))],
        out_specs=[pl.BlockSpec((tm, d), row), pl.BlockSpec((tm, d), row)],
        out_shape=[jax.ShapeDtypeStruct((m, d), F32), jax.ShapeDtypeStruct((m, d), BF16)],
        compiler_params=_params("parallel"),
        name="mm_ln",
    )(a, w, resid, g.reshape(1, d), b.reshape(1, d))


def _mla_proj_kernel(x_ref, w_ref, gq_ref, gkv_ref, cos_ref, sa_ref, sb_ref,
                     cq_ref, ckvf_ref, ckvb_ref, krf_ref, krb_ref, *, nq, nkv):
    lat = jnp.dot(x_ref[...].astype(BF16), w_ref[...], preferred_element_type=F32)
    cq_ref[...] = _rms_norm(lat[:, :nq], gq_ref[...]).astype(BF16)
    ckv = _rms_norm(lat[:, nq:nq + nkv], gkv_ref[...])
    ckvf_ref[...] = ckv
    ckvb_ref[...] = ckv.astype(BF16)
    kr = _rope128(lat[:, nq + nkv:], cos_ref[...], sa_ref[...], sb_ref[...])
    krf_ref[...] = kr
    krb_ref[...] = kr.astype(BF16)


def _mla_proj(x, w_pad, g_q, g_kv, cos, sin_a, sin_b, *, tm=528):
    m, kdim = x.shape
    nq, nkv = g_q.shape[0], g_kv.shape[0]
    n = w_pad.shape[1]
    assert n == nq + nkv + LANE
    tm = _tile(m, tm)
    row = lambda i: (i, 0)
    return pl.pallas_call(
        functools.partial(_mla_proj_kernel, nq=nq, nkv=nkv),
        grid=(m // tm,),
        in_specs=[pl.BlockSpec((tm, kdim), row), _resident((kdim, n)),
                  _resident((1, nq)), _resident((1, nkv)),
                  pl.BlockSpec((tm, LANE), row), pl.BlockSpec((tm, LANE), row),
                  pl.BlockSpec((tm, LANE), row)],
        out_specs=[pl.BlockSpec((tm, nq), row), pl.BlockSpec((tm, nkv), row),
                   pl.BlockSpec((tm, nkv), row), pl.BlockSpec((tm, LANE), row),
                   pl.BlockSpec((tm, LANE), row)],
        out_shape=[jax.ShapeDtypeStruct((m, nq), BF16), jax.ShapeDtypeStruct((m, nkv), F32),
                   jax.ShapeDtypeStruct((m, nkv), BF16), jax.ShapeDtypeStruct((m, LANE), F32),
                   jax.ShapeDtypeStruct((m, LANE), BF16)],
        compiler_params=_params("parallel"),
        name="mla_proj",
    )(x, w_pad, g_q.reshape(1, nq), g_kv.reshape(1, nkv), cos, sin_a, sin_b)


def _q_proj_kernel(c_ref, w_ref, cos_ref, sa_ref, sb_ref, o_ref, *, heads):
    y = jnp.dot(c_ref[...], w_ref[...], preferred_element_type=F32)
    for h in range(heads):
        lo = h * 2 * LANE
        o_ref[:, lo:lo + LANE] = y[:, lo:lo + LANE].astype(BF16)
        r = _rope128(y[:, lo + LANE:lo + 2 * LANE], cos_ref[...], sa_ref[...], sb_ref[...])
        o_ref[:, lo + LANE:lo + 2 * LANE] = r.astype(BF16)


def _q_proj(cq, w_arr, cos, sin_a, sin_b, *, tm=528, heads_per_step=8):
    m, kdim = cq.shape
    n = w_arr.shape[1]
    tm = _tile(m, tm)
    tn = heads_per_step * 2 * LANE
    row = lambda i, j: (i, 0)
    return pl.pallas_call(
        functools.partial(_q_proj_kernel, heads=heads_per_step),
        grid=(m // tm, n // tn),
        in_specs=[pl.BlockSpec((tm, kdim), row),
                  pl.BlockSpec((kdim, tn), lambda i, j: (0, j)),
                  pl.BlockSpec((tm, LANE), row), pl.BlockSpec((tm, LANE), row),
                  pl.BlockSpec((tm, LANE), row)],
        out_specs=pl.BlockSpec((tm, tn), lambda i, j: (i, j)),
        out_shape=jax.ShapeDtypeStruct((m, n), BF16),
        compiler_params=_params("parallel", "parallel"),
        name="q_proj",
    )(cq, w_arr, cos, sin_a, sin_b)


Q_CHAIN_ROWS = 256


def _row_parts(tq):
    n = max(1, tq // Q_CHAIN_ROWS)
    assert tq % n == 0
    return [slice(p * (tq // n), (p + 1) * (tq // n)) for p in range(n)]


def _visible_blocks(qi, *, q_off, tq, tk, nk):
    q_first = q_off + qi * tq
    q_last = q_first + tq - 1
    n_before = jnp.minimum(q_first // tk, nk)
    n_vis = jnp.minimum(((q_last // CHUNK + 1) * CHUNK + tk - 1) // tk, nk)
    return n_before, n_vis


def _positions(qi, ki, rows, *, q_off, tq, tk):
    shape = (rows.stop - rows.start, tk)
    qp = q_off + qi * tq + rows.start + lax.broadcasted_iota(jnp.int32, shape, 0)
    kp = ki * tk + lax.broadcasted_iota(jnp.int32, shape, 1)
    return qp, kp


def _key_mask(qi, ki, rows, geo):
    qp, kp = _positions(qi, ki, rows, q_off=geo["q_off"], tq=geo["tq"], tk=geo["tk"])
    k_chunk = kp // CHUNK
    if geo["t_valid"] < geo["nk"] * geo["tk"]:
        k_chunk = jnp.where(kp < geo["t_valid"], k_chunk, jnp.iinfo(jnp.int32).max)
    return k_chunk <= (qp // CHUNK), qp, kp


def _diagonal_only(geo):
    return (geo["tq"] == geo["tk"] and geo["q_off"] % geo["tk"] == 0
            and geo["t_valid"] == geo["nk"] * geo["tk"])


def _diagonal_visible(tq, tk):
    row = lax.broadcasted_iota(jnp.int32, (tq, tk), 0)
    col = lax.broadcasted_iota(jnp.int32, (tq, tk), 1)
    return (col // CHUNK) <= (row // CHUNK), row, col


def _for_blocks(lo, hi, step):
    lax.fori_loop(lo, hi, lambda ki, carry: (step(ki), carry)[1], 0)


def _online_softmax_step(s, v, m_ref, l_ref, acc_ref, rows):
    m_prev = m_ref[rows, :]
    m_new = jnp.maximum(m_prev, jnp.max(s, axis=-1, keepdims=True))
    alpha = jnp.exp(m_prev - m_new)
    p = jnp.exp(s - m_new)
    l_ref[rows, :] = alpha * l_ref[rows, :] + jnp.sum(p, axis=-1, keepdims=True)
    acc_ref[rows, :] = (alpha * acc_ref[rows, :]
                        + jnp.dot(p.astype(BF16), v, preferred_element_type=F32))
    m_ref[rows, :] = m_new


def _mla_attn_kernel(q_ref, knt_ref, krt_ref, v_ref, o_ref, kcat_t, diag_ref, s_buf,
                     m_ref, l_ref, acc_ref, *, geo, scale):
    qi = pl.program_id(2)
    tk = geo["tk"]
    parts = _row_parts(geo["tq"])
    diagonal_only = _diagonal_only(geo)

    @pl.when(qi == 0)
    def _():
        kcat_t[:MLA_NOPE, :] = knt_ref[...]
        kcat_t[MLA_NOPE:, :] = krt_ref[...]
        if diagonal_only:
            diag_ref[...] = jnp.where(_diagonal_visible(geo["tq"], tk)[0], 0.0, NEG)

    m_ref[...] = jnp.full_like(m_ref, -jnp.inf)
    l_ref[...] = jnp.zeros_like(l_ref)
    acc_ref[...] = jnp.zeros_like(acc_ref)
    n_before, n_vis = _visible_blocks(qi, q_off=geo["q_off"], tq=geo["tq"], tk=tk, nk=geo["nk"])

    def scores(ki, slot):
        r0 = pl.multiple_of(ki * tk, tk)
        k_t = kcat_t[:, pl.ds(r0, tk)]
        for rows in parts:
            s_buf[slot, rows, :] = jnp.dot(q_ref[rows, :], k_t, preferred_element_type=F32) * scale

    def step(ki, masked):
        slot = lax.rem(ki, 2)
        r0 = pl.multiple_of(ki * tk, tk)
        v = v_ref[pl.ds(r0, tk), :]
        for rows in parts:
            s = s_buf[slot, rows, :]
            if masked and diagonal_only:
                s = jnp.where(diag_ref[rows, :] < 0.0, NEG, s)
            elif masked:
                s = jnp.where(_key_mask(qi, ki, rows, geo)[0], s, NEG)
            _online_softmax_step(s, v, m_ref, l_ref, acc_ref, rows)
        scores(jnp.minimum(ki + 1, n_vis - 1), 1 - slot)

    scores(0, 0)
    _for_blocks(0, n_before, functools.partial(step, masked=False))
    _for_blocks(n_before, n_vis, functools.partial(step, masked=True))
    o_ref[...] = (acc_ref[...] / l_ref[...]).astype(BF16)


def _mla_attention(q2d, knt, krt, v2d, *, batch, t_q, t_k, t_valid, q_row0, q_off, tq, tk):
    nq, nk = t_q // tq, t_k // tk
    assert q_row0 % tq == 0 and t_k % tk == 0 and tk % LANE == 0
    geo = dict(q_off=q_off, tq=tq, tk=tk, nk=nk, t_valid=t_valid)
    qb0 = q_row0 // tq
    return pl.pallas_call(
        functools.partial(_mla_attn_kernel, geo=geo, scale=(MLA_NOPE + MLA_ROPE) ** -0.5),
        grid=(batch, MLA_HEADS, nq),
        in_specs=[pl.BlockSpec((tq, 2 * LANE), lambda b, h, qi: (qb0 + b * nq + qi, h)),
                  pl.BlockSpec((MLA_NOPE, t_k), lambda b, h, qi: (h, b)),
                  pl.BlockSpec((LANE, t_k), lambda b, h, qi: (0, b)),
                  pl.BlockSpec((t_k, MLA_V), lambda b, h, qi: (b, h))],
        out_specs=pl.BlockSpec((tq, MLA_V), lambda b, h, qi: (b * nq + qi, h)),
        out_shape=jax.ShapeDtypeStruct((batch * t_q, MLA_HEADS * MLA_V), BF16),
        scratch_shapes=[pltpu.VMEM((2 * LANE, t_k), BF16),
                        pltpu.VMEM((tq, tk) if _diagonal_only(geo) else (8, LANE), F32),
                        pltpu.VMEM((2, tq, tk), F32),
                        pltpu.VMEM((tq, 1), F32), pltpu.VMEM((tq, 1), F32),
                        pltpu.VMEM((tq, MLA_V), F32)],
        compiler_params=_params("parallel", "parallel", "arbitrary"),
        name="mla_attn",
    )(q2d, knt, krt, v2d)


def _diff_attn_kernel(slope_ref, lam_ref, gsub_ref, q_ref, k_ref, v_ref, o_ref,
                      b0_ref, diag_ref, s_buf, m1, l1, a1, m2, l2, a2,
                      *, geo, scale, lam_init, k_transposed):
    h, qi = pl.program_id(1), pl.program_id(2)
    tq, tk = geo["tq"], geo["tk"]
    parts = _row_parts(tq)
    diagonal_only = _diagonal_only(geo)
    stats = ((m1, l1, a1), (m2, l2, a2))
    for m, l, a in stats:
        m[...] = jnp.full_like(m, -jnp.inf)
        l[...] = jnp.zeros_like(l)
        a[...] = jnp.zeros_like(a)
    slope = slope_ref[h]
    q_first = geo["q_off"] + qi * tq

    @pl.when(qi == 0)
    def _():
        visible, row, col = _diagonal_visible(tq, tk)
        b0_ref[...] = -slope * (row - col).astype(F32)
        if diagonal_only:
            diag_ref[...] = jnp.where(visible, -slope * jnp.abs(row - col).astype(F32), NEG)

    n_before, n_vis = _visible_blocks(qi, q_off=geo["q_off"], tq=tq, tk=tk, nk=geo["nk"])

    def scores(ki, slot):
        r0 = pl.multiple_of(ki * tk, tk)
        for rows in parts:
            for half in range(2):
                lo = half * DIFF_QK
                q_half = q_ref[rows, lo:lo + DIFF_QK]
                if k_transposed:
                    s = jnp.dot(q_half, k_ref[lo:lo + DIFF_QK, pl.ds(r0, tk)],
                                preferred_element_type=F32)
                else:
                    s = lax.dot_general(q_half, k_ref[pl.ds(r0, tk), lo:lo + DIFF_QK],
                                        (((1,), (1,)), ((), ())), preferred_element_type=F32)
                s_buf[slot, half, rows, :] = s

    def step(ki, masked):
        slot = lax.rem(ki, 2)
        r0 = pl.multiple_of(ki * tk, tk)
        v = v_ref[pl.ds(r0, tk), :]
        for rows in parts:
            if masked and diagonal_only:
                bias = diag_ref[rows, :]
            elif masked:
                mask, qp, kp = _key_mask(qi, ki, rows, geo)
                bias = jnp.where(mask, -slope * jnp.abs(qp - kp).astype(F32), NEG)
            else:
                shift = -slope * jnp.full((1, tk), q_first - r0, jnp.int32).astype(F32)
                bias = b0_ref[rows, :] + shift
            for half, (m, l, a) in enumerate(stats):
                _online_softmax_step(s_buf[slot, half, rows, :] * scale + bias, v, m, l, a, rows)
        scores(jnp.minimum(ki + 1, n_vis - 1), 1 - slot)

    scores(0, 0)
    _for_blocks(0, n_before, functools.partial(step, masked=False))
    _for_blocks(n_before, n_vis, functools.partial(step, masked=True))

    lam_v = lam_ref[...]
    lam = (jnp.exp(jnp.sum(lam_v[0:1] * lam_v[1:2], axis=-1, keepdims=True))
           - jnp.exp(jnp.sum(lam_v[2:3] * lam_v[3:4], axis=-1, keepdims=True)) + lam_init)
    o = a1[...] / l1[...] - lam * (a2[...] / l2[...])
    o_ref[...] = (_rms_norm(o, gsub_ref[...]) * (1.0 - lam_init)).astype(BF16)


def _diff_attention(q2d, k, v2d, slopes, lam_vecs, g_sub, *, k_transposed,
                    batch, t_q, t_k, q_row0, q_off, tq, tk, lam_init):
    nq, nk = t_q // tq, t_k // tk
    assert q_row0 % tq == 0 and t_k % tk == 0 and (tk % LANE == 0 or not k_transposed)
    geo = dict(q_off=q_off, tq=tq, tk=tk, nk=nk, t_valid=t_k)
    qb0 = q_row0 // tq
    hd = 2 * DIFF_QK
    k_spec = (pl.BlockSpec((hd, t_k), lambda b, h, qi: (h, b)) if k_transposed
              else pl.BlockSpec((t_k, hd), lambda b, h, qi: (b, h)))
    return pl.pallas_call(
        functools.partial(_diff_attn_kernel, geo=geo, scale=DIFF_QK ** -0.5, lam_init=lam_init,
                          k_transposed=k_transposed),
        grid=(batch, DIFF_HEADS, nq),
        in_specs=[pl.BlockSpec(memory_space=pltpu.SMEM),
                  pl.BlockSpec((4, DIFF_QK), lambda b, h, qi: (0, 0)),
                  pl.BlockSpec((1, DIFF_V), lambda b, h, qi: (0, 0)),
                  pl.BlockSpec((tq, hd), lambda b, h, qi: (qb0 + b * nq + qi, h)),
                  k_spec,
                  pl.BlockSpec((t_k, DIFF_V), lambda b, h, qi: (b, h))],
        out_specs=pl.BlockSpec((tq, DIFF_V), lambda b, h, qi: (b * nq + qi, h)),
        out_shape=jax.ShapeDtypeStruct((batch * t_q, DIFF_HEADS * DIFF_V), BF16),
        scratch_shapes=[pltpu.VMEM((tq, tk), F32),
                        pltpu.VMEM((tq, tk) if _diagonal_only(geo) else (8, LANE), F32),
                        pltpu.VMEM((2, 2, tq, tk), F32),
                        pltpu.VMEM((tq, 1), F32), pltpu.VMEM((tq, 1), F32),
                        pltpu.VMEM((tq, DIFF_V), F32),
                        pltpu.VMEM((tq, 1), F32), pltpu.VMEM((tq, 1), F32),
                        pltpu.VMEM((tq, DIFF_V), F32)],
        compiler_params=_params("parallel", "parallel", "arbitrary"),
        name="diff_attn",
    )(slopes, lam_vecs, g_sub.reshape(1, DIFF_V), q2d, k, v2d)


def _top_rows(s, n_rows, k, payload=None):
    pos = lax.broadcasted_iota(jnp.int32, s.shape, 0).astype(F32)
    vals, idxs, pays = [], [], []
    for _ in range(k):
        m = jnp.max(s, axis=0, keepdims=True)
        sel = jnp.min(jnp.where(s == m, pos, float(n_rows)), axis=0, keepdims=True)
        hit = pos == sel
        vals.append(m)
        idxs.append(sel)
        if payload is not None:
            pays.append(jnp.max(jnp.where(hit, payload, -1.0), axis=0, keepdims=True))
        s = jnp.where(hit, -jnp.inf, s)
    cat = lambda xs: jnp.concatenate(xs, axis=0)
    return cat(vals), cat(idxs), (cat(pays) if payload is not None else None)


F32_SUBLANE = 8


def _batcher_pairs(n):
    pairs = []

    def merge(lo, cnt, r):
        step = r * 2
        if step < cnt:
            merge(lo, cnt, step)
            merge(lo + r, cnt, step)
            pairs.extend((i, i + r) for i in range(lo + r, lo + cnt - r, step))
        else:
            pairs.append((lo, lo + r))

    def sort(lo, cnt):
        if cnt > 1:
            sort(lo, cnt // 2)
            sort(lo + cnt // 2, cnt // 2)
            merge(lo, cnt, 1)

    sort(0, n)
    return pairs


def _top_rows_by_merge(s, k):
    n_rows, cols = s.shape
    depth = n_rows // F32_SUBLANE
    assert depth * F32_SUBLANE == n_rows and depth >= k
    sub = lax.broadcasted_iota(jnp.int32, (F32_SUBLANE, cols), 0).astype(F32)
    vals = [s[r * F32_SUBLANE:(r + 1) * F32_SUBLANE] for r in range(depth)]
    idxs = [sub + float(r * F32_SUBLANE) for r in range(depth)]
    for a, b in _batcher_pairs(depth):
        swap = vals[b] > vals[a]
        vals[a], vals[b] = jnp.maximum(vals[a], vals[b]), jnp.minimum(vals[a], vals[b])
        idxs[a], idxs[b] = jnp.where(swap, idxs[b], idxs[a]), jnp.where(swap, idxs[a], idxs[b])
    tie = jnp.zeros_like(sub)
    for r in range(depth - 1):
        tie = jnp.where(vals[r] == vals[r + 1], 1.0, tie)
    out_v, out_i = [], []
    for j in range(k):
        m = jnp.max(vals[0], axis=0, keepdims=True)
        sel = jnp.min(jnp.where(vals[0] == m, idxs[0], float(n_rows)), axis=0, keepdims=True)
        hit = idxs[0] == sel
        out_v.append(m)
        out_i.append(sel)
        for r in range(depth - 1 - j):
            vals[r] = jnp.where(hit, vals[r + 1], vals[r])
            idxs[r] = jnp.where(hit, idxs[r + 1], idxs[r])
    return jnp.concatenate(out_v, axis=0), jnp.concatenate(out_i, axis=0), jnp.max(tie)


def _peer_topk_kernel(q_ref, sk_ref, e_ref, g_ref):
    half = PEER_DKEY // 2
    q = q_ref[...]
    nt = (((1,), (1,)), ((), ()))
    scores = [lax.dot_general(sk_ref[c], q[:, c * half:(c + 1) * half], nt,
                              preferred_element_type=F32) for c in range(2)]
    fast = [_top_rows_by_merge(s, PEER_TOPK) for s in scores]
    _peer_pick(q.shape[0], [f[0] for f in fast], [f[1] for f in fast], e_ref, g_ref)

    @pl.when(jnp.maximum(fast[0][2], fast[1][2]) > 0.0)
    def _():
        slow = [_top_rows(s, N_KEYS, PEER_TOPK) for s in scores]
        _peer_pick(q.shape[0], [f[0] for f in slow], [f[1] for f in slow], e_ref, g_ref)


def _peer_pick(tokens, sv, si, e_ref, g_ref):
    width = [PEER_TOPK // (a + 1) for a in range(PEER_TOPK)]
    n_cand = sum(width)
    pad = -n_cand % 8
    cand = jnp.concatenate([sv[0][a:a + 1] + sv[1][:width[a]] for a in range(PEER_TOPK)]
                           + [jnp.full((pad, tokens), -jnp.inf, F32)], axis=0)
    cidx = jnp.concatenate([si[0][a:a + 1] * float(N_KEYS) + si[1][:width[a]]
                            for a in range(PEER_TOPK)]
                           + [jnp.full((pad, tokens), -1.0, F32)], axis=0)
    fv, _, fe = _top_rows(cand, n_cand + pad, PEER_TOPK, payload=cidx)
    p = jnp.exp(fv - fv[0:1])
    g_ref[...] = p / jnp.sum(p, axis=0, keepdims=True)
    e_ref[...] = fe.astype(jnp.int32)


def _peer_topk(qb, sub_keys_b, *, tn=256):
    m = qb.shape[0]
    tn = _tile(m, tn, LANE)
    half = PEER_DKEY // 2
    out_spec = pl.BlockSpec((None, PEER_TOPK, tn), lambda i, h: (h, 0, i))
    return pl.pallas_call(
        _peer_topk_kernel,
        grid=(m // tn, PEER_HEADS),
        in_specs=[pl.BlockSpec((tn, PEER_DKEY), lambda i, h: (i, h)),
                  pl.BlockSpec((None, 2, N_KEYS, half), lambda i, h: (h, 0, 0, 0))],
        out_specs=[out_spec, out_spec],
        out_shape=[jax.ShapeDtypeStruct((PEER_HEADS, PEER_TOPK, m), jnp.int32),
                   jax.ShapeDtypeStruct((PEER_HEADS, PEER_TOPK, m), F32)],
        compiler_params=_params("parallel", "parallel"),
        name="peer_topk",
    )(qb, sub_keys_b)


GATE_ROWS = 8
GATE_UNROLL = 32


def _peer_gate_kernel(e_ref, g_ref, o_ref, *, rows):
    npick = PEER_HEADS * PEER_TOPK
    key = lax.broadcasted_iota(jnp.int32, (N_KEYS, npick), 0).astype(F32).astype(BF16)
    one, zero = jnp.ones((), BF16), jnp.zeros((), BF16)
    nt = (((1,), (1,)), ((), ()))
    shift = N_KEYS.bit_length() - 1

    def body(blk, carry):
        for u in range(GATE_UNROLL):
            n = blk * GATE_UNROLL + u
            e = e_ref[pl.ds(n, 1), :]
            g = g_ref[pl.ds(n, 1), :].astype(BF16)
            e_first = (e >> shift).astype(F32).astype(BF16)
            e_second = (e & (N_KEYS - 1)).astype(F32).astype(BF16)
            first = jnp.where(key == e_first, one, zero)
            second = jnp.where(key == e_second, g, zero)
            w = lax.dot_general(first, second, nt, preferred_element_type=F32)
            o_ref[:, n] = w.reshape(N_KEYS // GATE_ROWS, GATE_ROWS, N_KEYS)
        return carry

    lax.fori_loop(0, rows // GATE_UNROLL, body, 0)


def _peer_gate(e_t, g_t, *, tb=64):
    m, npick = e_t.shape
    assert N_KEYS & (N_KEYS - 1) == 0
    tb = _tile(m, tb, GATE_UNROLL)
    nib = N_KEYS // GATE_ROWS
    return pl.pallas_call(
        functools.partial(_peer_gate_kernel, rows=tb),
        grid=(m // tb,),
        in_specs=[pl.BlockSpec((tb, npick), lambda i: (i, 0)),
                  pl.BlockSpec((tb, npick), lambda i: (i, 0))],
        out_specs=pl.BlockSpec((nib, tb, GATE_ROWS, N_KEYS), lambda i: (0, i, 0, 0)),
        out_shape=jax.ShapeDtypeStruct((nib, m, GATE_ROWS, N_KEYS), F32),
        compiler_params=_params("parallel"),
        name="peer_gate",
    )(e_t, g_t)


def _gelu_tanh(x):
    c = np.sqrt(2 / np.pi).astype(np.float32)
    return x * (0.5 * (1.0 + jnp.tanh(c * (x + 0.044715 * (x * x * x)))))


DENSE_PARTS = 1


def _peer_dense_kernel(xb_ref, xf_ref, wd_ref, ut_ref, v_ref, g_ref, b_ref,
                       of_ref, ob_ref, w_ref, *, ne, tn):
    e = pl.program_id(1)

    @pl.when(e == 0)
    def _():
        of_ref[...] = jnp.zeros_like(of_ref)

    rows_per_part = tn // DENSE_PARTS
    for part in range(DENSE_PARTS):
        rows = slice(part * rows_per_part, (part + 1) * rows_per_part)
        h = jnp.dot(xb_ref[rows, :], ut_ref[...], preferred_element_type=F32)
        for r in range(GATE_ROWS):
            lanes = slice(r * N_KEYS, (r + 1) * N_KEYS)
            gate = wd_ref[pl.ds(rows.start * GATE_ROWS + r, rows_per_part, stride=GATE_ROWS), :]
            w_ref[rows, lanes] = (gate * _gelu_tanh(h[:, lanes])).astype(BF16)
        of_ref[rows, :] += jnp.dot(w_ref[rows, :], v_ref[...], preferred_element_type=F32)

    @pl.when(e == ne - 1)
    def _():
        _write_ln(ALPHA * xf_ref[...] + of_ref[...], g_ref, b_ref, of_ref, ob_ref)


def _peer_dense(xb, xf, wd, u_t, v, layer, g, b, *, tn=528):
    m, d = xb.shape
    ne = wd.shape[0]
    te = GATE_ROWS * N_KEYS
    assert v.shape[1] == ne * te
    tn = _tile(m, tn, BF16_SUBLANE * DENSE_PARTS)
    row = lambda i, e: (i, 0)
    return pl.pallas_call(
        functools.partial(_peer_dense_kernel, ne=ne, tn=tn),
        grid=(m // tn, ne),
        in_specs=[pl.BlockSpec((tn, d), row), pl.BlockSpec((tn, d), row),
                  pl.BlockSpec((None, tn * GATE_ROWS, N_KEYS), lambda i, e: (e, i, 0)),
                  pl.BlockSpec((None, d, te), lambda i, e: (layer, 0, e)),
                  pl.BlockSpec((None, te, d), lambda i, e: (layer, e, 0)),
                  _resident((1, d)), _resident((1, d))],
        out_specs=[pl.BlockSpec((tn, d), row), pl.BlockSpec((tn, d), row)],
        out_shape=[jax.ShapeDtypeStruct((m, d), F32), jax.ShapeDtypeStruct((m, d), BF16)],
        scratch_shapes=[pltpu.VMEM((tn, te), BF16)],
        compiler_params=_params("parallel", "arbitrary"),
        name="peer_dense",
    )(xb, xf, wd, u_t, v, g.reshape(1, d), b.reshape(1, d))


def _peer_layer(xf, xb, w_query, sub_keys, u_t_all, v_all, layer, ln_g, ln_b):
    m = xf.shape[0]
    npick = PEER_HEADS * PEER_TOPK
    (qb,) = _mm(xb, w_query.astype(BF16), [BF16])
    eidx, gates = _peer_topk(qb, sub_keys.astype(BF16))
    e_t = eidx.reshape(npick, m).T
    g_t = gates.reshape(npick, m).T
    wd = _peer_gate(e_t, g_t).reshape(N_KEYS // GATE_ROWS, m * GATE_ROWS, N_KEYS)
    return _peer_dense(xb, xf, wd, u_t_all, v_all, layer, ln_g, ln_b)


def _rope_tables(pos):
    half = MLA_ROPE // 2
    inv = ROPE_THETA ** (-jnp.arange(half, dtype=jnp.float32) / half)
    ang = pos.astype(jnp.float32)[:, None] * inv
    cos, sin = jnp.cos(ang), jnp.sin(ang)
    z = jnp.zeros_like(cos)
    return (jnp.concatenate([cos, cos, z, z], -1), jnp.concatenate([-sin, z, z, z], -1),
            jnp.concatenate([z, sin, z, z], -1))


def kernel(x_prompt, x_sample, cache_mla_ckv, cache_mla_krope, cache_diff_k, cache_diff_v,
           mla_w_dqkv, mla_g_q, mla_w_uq, mla_g_kv, mla_w_ukv, mla_w_o,
           diff_w_qkv, diff_lam_q1, diff_lam_k1, diff_lam_q2, diff_lam_k2, diff_g_sub, diff_w_o,
           peer_w_query, peer_sub_keys, peer_u, peer_v,
           ln_mix_g, ln_mix_b, ln_ffn_g, ln_ffn_b):
    bp, tp, d = x_prompt.shape
    bs, ts, _ = x_sample.shape
    past = cache_mla_ckv.shape[2]
    mp, ms = bp * tp, bs * ts
    tks = past + ts
    tkp = -(-tks // LANE) * LANE
    tq_p = _tile(tp, 512, LANE)

    def pad_keys(a):
        return jnp.pad(a, ((0, 0), (0, tkp - tks), (0, 0))).reshape(bs * tkp, a.shape[-1])

    u_t_all = peer_u.astype(BF16).swapaxes(1, 2)
    v_all = peer_v.astype(BF16)
    xf = jnp.concatenate([x_prompt.reshape(mp, d), x_sample.reshape(ms, d)], 0)
    pos = jnp.concatenate([jnp.tile(jnp.arange(tp), bp), jnp.tile(past + jnp.arange(ts), bs)])
    cos, sin_a, sin_b = _rope_tables(pos)

    j = 0
    w_dqkv = jnp.pad(mla_w_dqkv[j], ((0, 0), (0, LANE - MLA_ROPE))).astype(BF16)
    cq, ckv_f, ckv_b, kr_f, kr_b = _mla_proj(xf, w_dqkv, mla_g_q[j], mla_g_kv[j], cos, sin_a, sin_b)
    hq = MLA_NOPE + MLA_ROPE
    w_uq = jnp.pad(mla_w_uq[j].reshape(MLA_Q_LORA, MLA_HEADS, hq),
                   ((0, 0), (0, 0), (0, 2 * LANE - hq))).reshape(MLA_Q_LORA, MLA_HEADS * 2 * LANE)
    q = _q_proj(cq, w_uq.astype(BF16), cos, sin_a, sin_b)
    w_ukv = mla_w_ukv[j].reshape(MLA_KV_LORA, MLA_HEADS, MLA_NOPE + MLA_V)
    w_uk_t = w_ukv[:, :, :MLA_NOPE].reshape(MLA_KV_LORA, -1).T.astype(BF16)
    w_uv = w_ukv[:, :, MLA_NOPE:].reshape(MLA_KV_LORA, -1).astype(BF16)
    lora_tiles = dict(tm=1024, tn=2048)
    (knt_p,) = _mm(w_uk_t, ckv_b[:mp].T, [BF16], **lora_tiles)
    (v_p,) = _mm(ckv_b[:mp], w_uv, [BF16], **lora_tiles)
    o_p = _mla_attention(q, knt_p, kr_b[:mp].T, v_p, batch=bp, t_q=tp, t_k=tp,
                         t_valid=tp, q_row0=0, q_off=0, tq=tq_p, tk=tq_p)
    ckv_cat = pad_keys(jnp.concatenate([cache_mla_ckv[j].astype(BF16),
                                        ckv_b[mp:].reshape(bs, ts, MLA_KV_LORA)], 1))
    kr_cache = jnp.pad(cache_mla_krope[j], ((0, 0), (0, 0), (0, LANE - MLA_ROPE))).astype(BF16)
    kr_cat = pad_keys(jnp.concatenate([kr_cache, kr_b[mp:].reshape(bs, ts, LANE)], 1))
    (knt_s,) = _mm(w_uk_t, ckv_cat.T, [BF16], **lora_tiles)
    (v_s,) = _mm(ckv_cat, w_uv, [BF16], **lora_tiles)
    o_s = _mla_attention(q, knt_s, kr_cat.T, v_s, batch=bs, t_q=ts, t_k=tkp,
                         t_valid=tks, q_row0=mp, q_off=past, tq=ts, tk=tkp)
    o = jnp.concatenate([o_p, o_s], 0)
    xf, xb = _mm_ln(o, mla_w_o[j].astype(BF16), xf, ln_mix_g[0], ln_mix_b[0])
    xf, xb = _peer_layer(xf, xb, peer_w_query[0], peer_sub_keys[0], u_t_all, v_all, 0,
                         ln_ffn_g[0], ln_ffn_b[0])

    i = 1
    lam_init = 0.8 - 0.6 * math.exp(-0.3 * i)
    nqk = DIFF_HEADS * 2 * DIFF_QK
    w_qkv = diff_w_qkv[j].astype(BF16)
    (dq,) = _mm(xb, w_qkv[:, :nqk], [BF16])
    dk_f, dk_b = _mm(xb, w_qkv[:, nqk:2 * nqk], [F32, BF16])
    dv_f, dv_b = _mm(xb, w_qkv[:, 2 * nqk:], [F32, BF16])
    slopes = 2.0 ** (-8.0 * jnp.arange(1, DIFF_HEADS + 1, dtype=jnp.float32) / DIFF_HEADS)
    lam_vecs = jnp.stack([diff_lam_q1[j], diff_lam_k1[j], diff_lam_q2[j], diff_lam_k2[j]])
    diff_args = (slopes, lam_vecs, diff_g_sub[j])
    o_p = _diff_attention(dq, dk_b[:mp].T, dv_b, *diff_args, k_transposed=True, batch=bp,
                          t_q=tp, t_k=tp, q_row0=0, q_off=0, tq=tq_p, tk=tq_p, lam_init=lam_init)
    k_cat = jnp.concatenate([cache_diff_k[j].reshape(bs, past, nqk).astype(BF16),
                             dk_b[mp:].reshape(bs, ts, nqk)], 1).reshape(bs * tks, nqk)
    v_cat = jnp.concatenate([cache_diff_v[j].reshape(bs, past, -1).astype(BF16),
                             dv_b[mp:].reshape(bs, ts, -1)], 1).reshape(bs * tks, -1)
    o_s = _diff_attention(dq, k_cat, v_cat, *diff_args, k_transposed=False, batch=bs,
                          t_q=ts, t_k=tks, q_row0=mp, q_off=past, tq=ts, tk=tks, lam_init=lam_init)
    o = jnp.concatenate([o_p, o_s], 0)
    xf, xb = _mm_ln(o, diff_w_o[j].astype(BF16), xf, ln_mix_g[1], ln_mix_b[1])
    xf, xb = _peer_layer(xf, xb, peer_w_query[1], peer_sub_keys[1], u_t_all, v_all, 1,
                         ln_ffn_g[1], ln_ffn_b[1])

    kr_f = kr_f[:, :MLA_ROPE]
    return (xf[:mp].reshape(bp, tp, d), xf[mp:].reshape(bs, ts, d),
            ckv_f[:mp].reshape(1, bp, tp, -1), kr_f[:mp].reshape(1, bp, tp, -1),
            dk_f[:mp].reshape(1, bp, tp, DIFF_HEADS, -1), dv_f[:mp].reshape(1, bp, tp, DIFF_HEADS, -1),
            ckv_f[mp:].reshape(1, bs, ts, -1), kr_f[mp:].reshape(1, bs, ts, -1),
            dk_f[mp:].reshape(1, bs, ts, DIFF_HEADS, -1), dv_f[mp:].reshape(1, bs, ts, DIFF_HEADS, -1))
```

```python
import functools
import math

import jax
import jax.numpy as jnp
import numpy as np
from jax import lax
from jax.experimental import pallas as pl
from jax.experimental.pallas import tpu as pltpu

F32 = jnp.float32
BF16 = jnp.bfloat16

DEPTH = 2
CHUNK = 64
ALPHA = (2 * DEPTH) ** 0.25
LN_EPS = 1e-5
RMS_EPS = 1e-6
NEG = -1e30
ROPE_THETA = 10000.0
MLA_HEADS = 16
MLA_Q_LORA = 512
MLA_KV_LORA = 512
MLA_NOPE = 128
MLA_ROPE = 64
MLA_V = 128
DIFF_HEADS = 8
DIFF_QK = 128
DIFF_V = 256
PEER_HEADS = 8
PEER_TOPK = 16
N_KEYS = 128
PEER_DKEY = 256

LANE = 128
BF16_SUBLANE = 16
VMEM_LIMIT = 56 << 20


def _tile(n, target, mult=BF16_SUBLANE):
    for t in range(min(n, target), 0, -1):
        if n % t == 0 and t % mult == 0:
            return t
    raise ValueError(f"no tile for {n} (target {target}, multiple of {mult})")


def _params(*sem):
    return pltpu.CompilerParams(dimension_semantics=sem, vmem_limit_bytes=VMEM_LIMIT)


def _layer_norm(y, g, b):
    mu = jnp.mean(y, axis=-1, keepdims=True)
    d = y - mu
    var = jnp.mean(d * d, axis=-1, keepdims=True)
    return d * lax.rsqrt(var + LN_EPS) * g + b


def _rms_norm(y, g):
    return y * lax.rsqrt(jnp.mean(y * y, axis=-1, keepdims=True) + RMS_EPS) * g


def _rope128(r, cos, sin_a, sin_b):
    return r * cos + pltpu.roll(r, 96, 1) * sin_a + pltpu.roll(r, 32, 1) * sin_b


def _mm_kernel(x_ref, w_ref, *outs):
    y = jnp.dot(x_ref[...], w_ref[...], preferred_element_type=F32)
    for o in outs:
        o[...] = y.astype(o.dtype)


def _mm(x, w, out_dtypes, *, tm=528, tn=1024):
    m, kdim = x.shape
    n = w.shape[1]
    tm, tn = _tile(m, tm), _tile(n, tn, LANE)
    outs = pl.pallas_call(
        _mm_kernel,
        grid=(m // tm, n // tn),
        in_specs=[pl.BlockSpec((tm, kdim), lambda i, j: (i, 0)),
                  pl.BlockSpec((kdim, tn), lambda i, j: (0, j))],
        out_specs=[pl.BlockSpec((tm, tn), lambda i, j: (i, j)) for _ in out_dtypes],
        out_shape=[jax.ShapeDtypeStruct((m, n), dt) for dt in out_dtypes],
        compiler_params=_params("parallel", "parallel"),
        name="mm",
    )(x, w)
    return outs


def _resident(shape):
    return pl.BlockSpec(shape, lambda *_: (0,) * len(shape), pipeline_mode=pl.Buffered(1))


def _write_ln(y, g_ref, b_ref, of_ref, ob_ref):
    y = _layer_norm(y, g_ref[...], b_ref[...])
    of_ref[...] = y
    ob_ref[...] = y.astype(BF16)


def _mm_ln_kernel(a_ref, w_ref, r_ref, g_ref, b_ref, of_ref, ob_ref):
    mix = jnp.dot(a_ref[...], w_ref[...], preferred_element_type=F32)
    _write_ln(ALPHA * r_ref[...] + mix, g_ref, b_ref, of_ref, ob_ref)


def _mm_ln(a, w, resid, g, b, *, tm=528):
    m, kdim = a.shape
    d = w.shape[1]
    tm = _tile(m, tm)
    row = lambda i: (i, 0)
    return pl.pallas_call(
        _mm_ln_kernel,
        grid=(m // tm,),
        in_specs=[pl.BlockSpec((tm, kdim), row), _resident((kdim, d)),
                  pl.BlockSpec((tm, d), row), _resident((1, d)), _resident((1, d))],
        out_specs=[pl.BlockSpec((tm, d), row), pl.BlockSpec((tm, d), row)],
        out_shape=[jax.ShapeDtypeStruct((m, d), F32), jax.ShapeDtypeStruct((m, d), BF16)],
        compiler_params=_params("parallel"),
        name="mm_ln",
    )(a, w, resid, g.reshape(1, d), b.reshape(1, d))


def _mla_proj_kernel(x_ref, w_ref, gq_ref, gkv_ref, cos_ref, sa_ref, sb_ref,
                     cq_ref, ckvf_ref, ckvb_ref, krf_ref, krb_ref, *, nq, nkv):
    lat = jnp.dot(x_ref[...].astype(BF16), w_ref[...], preferred_element_type=F32)
    cq_ref[...] = _rms_norm(lat[:, :nq], gq_ref[...]).astype(BF16)
    ckv = _rms_norm(lat[:, nq:nq + nkv], gkv_ref[...])
    ckvf_ref[...] = ckv
    ckvb_ref[...] = ckv.astype(BF16)
    kr = _rope128(lat[:, nq + nkv:], cos_ref[...], sa_ref[...], sb_ref[...])
    krf_ref[...] = kr
    krb_ref[...] = kr.astype(BF16)


def _mla_proj(x, w_pad, g_q, g_kv, cos, sin_a, sin_b, *, tm=528):
    m, kdim = x.shape
    nq, nkv = g_q.shape[0], g_kv.shape[0]
    n = w_pad.shape[1]
    assert n == nq + nkv + LANE
    tm = _tile(m, tm)
    row = lambda i: (i, 0)
    return pl.pallas_call(
        functools.partial(_mla_proj_kernel, nq=nq, nkv=nkv),
        grid=(m // tm,),
        in_specs=[pl.BlockSpec((tm, kdim), row), _resident((kdim, n)),
                  _resident((1, nq)), _resident((1, nkv)),
                  pl.BlockSpec((tm, LANE), row), pl.BlockSpec((tm, LANE), row),
                  pl.BlockSpec((tm, LANE), row)],
        out_specs=[pl.BlockSpec((tm, nq), row), pl.BlockSpec((tm, nkv), row),
                   pl.BlockSpec((tm, nkv), row), pl.BlockSpec((tm, LANE), row),
                   pl.BlockSpec((tm, LANE), row)],
        out_shape=[jax.ShapeDtypeStruct((m, nq), BF16), jax.ShapeDtypeStruct((m, nkv), F32),
                   jax.ShapeDtypeStruct((m, nkv), BF16), jax.ShapeDtypeStruct((m, LANE), F32),
                   jax.ShapeDtypeStruct((m, LANE), BF16)],
        compiler_params=_params("parallel"),
        name="mla_proj",
    )(x, w_pad, g_q.reshape(1, nq), g_kv.reshape(1, nkv), cos, sin_a, sin_b)


def _q_proj_kernel(c_ref, w_ref, cos_ref, sa_ref, sb_ref, o_ref, *, heads):
    y = jnp.dot(c_ref[...], w_ref[...], preferred_element_type=F32)
    for h in range(heads):
        lo = h * 2 * LANE
        o_ref[:, lo:lo + LANE] = y[:, lo:lo + LANE].astype(BF16)
        r = _rope128(y[:, lo + LANE:lo + 2 * LANE], cos_ref[...], sa_ref[...], sb_ref[...])
        o_ref[:, lo + LANE:lo + 2 * LANE] = r.astype(BF16)


def _q_proj(cq, w_arr, cos, sin_a, sin_b, *, tm=528, heads_per_step=8):
    m, kdim = cq.shape
    n = w_arr.shape[1]
    tm = _tile(m, tm)
    tn = heads_per_step * 2 * LANE
    row = lambda i, j: (i, 0)
    return pl.pallas_call(
        functools.partial(_q_proj_kernel, heads=heads_per_step),
        grid=(m // tm, n // tn),
        in_specs=[pl.BlockSpec((tm, kdim), row),
                  pl.BlockSpec((kdim, tn), lambda i, j: (0, j)),
                  pl.BlockSpec((tm, LANE), row), pl.BlockSpec((tm, LANE), row),
                  pl.BlockSpec((tm, LANE), row)],
        out_specs=pl.BlockSpec((tm, tn), lambda i, j: (i, j)),
        out_shape=jax.ShapeDtypeStruct((m, n), BF16),
        compiler_params=_params("parallel", "parallel"),
        name="q_proj",
    )(cq, w_arr, cos, sin_a, sin_b)


Q_CHAIN_ROWS = 256


def _row_parts(tq):
    n = max(1, tq // Q_CHAIN_ROWS)
    assert tq % n == 0
    return [slice(p * (tq // n), (p + 1) * (tq // n)) for p in range(n)]


def _visible_blocks(qi, *, q_off, tq, tk, nk):
    q_first = q_off + qi * tq
    q_last = q_first + tq - 1
    n_before = jnp.minimum(q_first // tk, nk)
    n_vis = jnp.minimum(((q_last // CHUNK + 1) * CHUNK + tk - 1) // tk, nk)
    return n_before, n_vis


def _positions(qi, ki, rows, *, q_off, tq, tk):
    shape = (rows.stop - rows.start, tk)
    qp = q_off + qi * tq + rows.start + lax.broadcasted_iota(jnp.int32, shape, 0)
    kp = ki * tk + lax.broadcasted_iota(jnp.int32, shape, 1)
    return qp, kp


def _key_mask(qi, ki, rows, geo):
    qp, kp = _positions(qi, ki, rows, q_off=geo["q_off"], tq=geo["tq"], tk=geo["tk"])
    k_chunk = kp // CHUNK
    if geo["t_valid"] < geo["nk"] * geo["tk"]:
        k_chunk = jnp.where(kp < geo["t_valid"], k_chunk, jnp.iinfo(jnp.int32).max)
    return k_chunk <= (qp // CHUNK), qp, kp


def _diagonal_only(geo):
    return (geo["tq"] == geo["tk"] and geo["q_off"] % geo["tk"] == 0
            and geo["t_valid"] == geo["nk"] * geo["tk"])


def _diagonal_visible(tq, tk):
    row = lax.broadcasted_iota(jnp.int32, (tq, tk), 0)
    col = lax.broadcasted_iota(jnp.int32, (tq, tk), 1)
    return (col // CHUNK) <= (row // CHUNK), row, col


def _for_blocks(lo, hi, step):
    lax.fori_loop(lo, hi, lambda ki, carry: (step(ki), carry)[1], 0)


def _online_softmax_step(s, v, m_ref, l_ref, acc_ref, rows):
    m_prev = m_ref[rows, :]
    m_new = jnp.maximum(m_prev, jnp.max(s, axis=-1, keepdims=True))
    alpha = jnp.exp(m_prev - m_new)
    p = jnp.exp(s - m_new)
    l_ref[rows, :] = alpha * l_ref[rows, :] + jnp.sum(p, axis=-1, keepdims=True)
    acc_ref[rows, :] = (alpha * acc_ref[rows, :]
                        + jnp.dot(p.astype(BF16), v, preferred_element_type=F32))
    m_ref[rows, :] = m_new


def _mla_attn_kernel(q_ref, knt_ref, krt_ref, v_ref, o_ref, kcat_t, diag_ref, s_buf,
                     m_ref, l_ref, acc_ref, *, geo, scale):
    qi = pl.program_id(2)
    tk = geo["tk"]
    parts = _row_parts(geo["tq"])
    diagonal_only = _diagonal_only(geo)

    @pl.when(qi == 0)
    def _():
        kcat_t[:MLA_NOPE, :] = knt_ref[...]
        kcat_t[MLA_NOPE:, :] = krt_ref[...]
        if diagonal_only:
            diag_ref[...] = jnp.where(_diagonal_visible(geo["tq"], tk)[0], 0.0, NEG)

    m_ref[...] = jnp.full_like(m_ref, -jnp.inf)
    l_ref[...] = jnp.zeros_like(l_ref)
    acc_ref[...] = jnp.zeros_like(acc_ref)
    n_before, n_vis = _visible_blocks(qi, q_off=geo["q_off"], tq=geo["tq"], tk=tk, nk=geo["nk"])

    def scores(ki, slot):
        r0 = pl.multiple_of(ki * tk, tk)
        k_t = kcat_t[:, pl.ds(r0, tk)]
        for rows in parts:
            s_buf[slot, rows, :] = jnp.dot(q_ref[rows, :], k_t, preferred_element_type=F32) * scale

    def step(ki, masked):
        slot = lax.rem(ki, 2)
        r0 = pl.multiple_of(ki * tk, tk)
        v = v_ref[pl.ds(r0, tk), :]
        for rows in parts:
            s = s_buf[slot, rows, :]
            if masked and diagonal_only:
                s = jnp.where(diag_ref[rows, :] < 0.0, NEG, s)
            elif masked:
                s = jnp.where(_key_mask(qi, ki, rows, geo)[0], s, NEG)
            _online_softmax_step(s, v, m_ref, l_ref, acc_ref, rows)
        scores(jnp.minimum(ki + 1, n_vis - 1), 1 - slot)

    scores(0, 0)
    _for_blocks(0, n_before, functools.partial(step, masked=False))
    _for_blocks(n_before, n_vis, functools.partial(step, masked=True))
    o_ref[...] = (acc_ref[...] / l_ref[...]).astype(BF16)


def _mla_attention(q2d, knt, krt, v2d, *, batch, t_q, t_k, t_valid, q_row0, q_off, tq, tk):
    nq, nk = t_q // tq, t_k // tk
    assert q_row0 % tq == 0 and t_k % tk == 0 and tk % LANE == 0
    geo = dict(q_off=q_off, tq=tq, tk=tk, nk=nk, t_valid=t_valid)
    qb0 = q_row0 // tq
    return pl.pallas_call(
        functools.partial(_mla_attn_kernel, geo=geo, scale=(MLA_NOPE + MLA_ROPE) ** -0.5),
        grid=(batch, MLA_HEADS, nq),
        in_specs=[pl.BlockSpec((tq, 2 * LANE), lambda b, h, qi: (qb0 + b * nq + qi, h)),
                  pl.BlockSpec((MLA_NOPE, t_k), lambda b, h, qi: (h, b)),
                  pl.BlockSpec((LANE, t_k), lambda b, h, qi: (0, b)),
                  pl.BlockSpec((t_k, MLA_V), lambda b, h, qi: (b, h))],
        out_specs=pl.BlockSpec((tq, MLA_V), lambda b, h, qi: (b * nq + qi, h)),
        out_shape=jax.ShapeDtypeStruct((batch * t_q, MLA_HEADS * MLA_V), BF16),
        scratch_shapes=[pltpu.VMEM((2 * LANE, t_k), BF16),
                        pltpu.VMEM((tq, tk) if _diagonal_only(geo) else (8, LANE), F32),
                        pltpu.VMEM((2, tq, tk), F32),
                        pltpu.VMEM((tq, 1), F32), pltpu.VMEM((tq, 1), F32),
                        pltpu.VMEM((tq, MLA_V), F32)],
        compiler_params=_params("parallel", "parallel", "arbitrary"),
        name="mla_attn",
    )(q2d, knt, krt, v2d)


def _diff_attn_kernel(slope_ref, lam_ref, gsub_ref, q_ref, k_ref, v_ref, o_ref,
                      b0_ref, diag_ref, s_buf, m1, l1, a1, m2, l2, a2,
                      *, geo, scale, lam_init, k_transposed):
    h, qi = pl.program_id(1), pl.program_id(2)
    tq, tk = geo["tq"], geo["tk"]
    parts = _row_parts(tq)
    diagonal_only = _diagonal_only(geo)
    stats = ((m1, l1, a1), (m2, l2, a2))
    for m, l, a in stats:
        m[...] = jnp.full_like(m, -jnp.inf)
        l[...] = jnp.zeros_like(l)
        a[...] = jnp.zeros_like(a)
    slope = slope_ref[h]
    q_first = geo["q_off"] + qi * tq

    @pl.when(qi == 0)
    def _():
        visible, row, col = _diagonal_visible(tq, tk)
        b0_ref[...] = -slope * (row - col).astype(F32)
        if diagonal_only:
            diag_ref[...] = jnp.where(visible, -slope * jnp.abs(row - col).astype(F32), NEG)

    n_before, n_vis = _visible_blocks(qi, q_off=geo["q_off"], tq=tq, tk=tk, nk=geo["nk"])

    def scores(ki, slot):
        r0 = pl.multiple_of(ki * tk, tk)
        for rows in parts:
            for half in range(2):
                lo = half * DIFF_QK
                q_half = q_ref[rows, lo:lo + DIFF_QK]
                if k_transposed:
                    s = jnp.dot(q_half, k_ref[lo:lo + DIFF_QK, pl.ds(r0, tk)],
                                preferred_element_type=F32)
                else:
                    s = lax.dot_general(q_half, k_ref[pl.ds(r0, tk), lo:lo + DIFF_QK],
                                        (((1,), (1,)), ((), ())), preferred_element_type=F32)
                s_buf[slot, half, rows, :] = s

    def step(ki, masked):
        slot = lax.rem(ki, 2)
        r0 = pl.multiple_of(ki * tk, tk)
        v = v_ref[pl.ds(r0, tk), :]
        for rows in parts:
            if masked and diagonal_only:
                bias = diag_ref[rows, :]
            elif masked:
                mask, qp, kp = _key_mask(qi, ki, rows, geo)
                bias = jnp.where(mask, -slope * jnp.abs(qp - kp).astype(F32), NEG)
            else:
                shift = -slope * jnp.full((1, tk), q_first - r0, jnp.int32).astype(F32)
                bias = b0_ref[rows, :] + shift
            for half, (m, l, a) in enumerate(stats):
                _online_softmax_step(s_buf[slot, half, rows, :] * scale + bias, v, m, l, a, rows)
        scores(jnp.minimum(ki + 1, n_vis - 1), 1 - slot)

    scores(0, 0)
    _for_blocks(0, n_before, functools.partial(step, masked=False))
    _for_blocks(n_before, n_vis, functools.partial(step, masked=True))

    lam_v = lam_ref[...]
    lam = (jnp.exp(jnp.sum(lam_v[0:1] * lam_v[1:2], axis=-1, keepdims=True))
           - jnp.exp(jnp.sum(lam_v[2:3] * lam_v[3:4], axis=-1, keepdims=True)) + lam_init)
    o = a1[...] / l1[...] - lam * (a2[...] / l2[...])
    o_ref[...] = (_rms_norm(o, gsub_ref[...]) * (1.0 - lam_init)).astype(BF16)


def _diff_attention(q2d, k, v2d, slopes, lam_vecs, g_sub, *, k_transposed,
                    batch, t_q, t_k, q_row0, q_off, tq, tk, lam_init):
    nq, nk = t_q // tq, t_k // tk
    assert q_row0 % tq == 0 and t_k % tk == 0 and (tk % LANE == 0 or not k_transposed)
    geo = dict(q_off=q_off, tq=tq, tk=tk, nk=nk, t_valid=t_k)
    qb0 = q_row0 // tq
    hd = 2 * DIFF_QK
    k_spec = (pl.BlockSpec((hd, t_k), lambda b, h, qi: (h, b)) if k_transposed
              else pl.BlockSpec((t_k, hd), lambda b, h, qi: (b, h)))
    return pl.pallas_call(
        functools.partial(_diff_attn_kernel, geo=geo, scale=DIFF_QK ** -0.5, lam_init=lam_init,
                          k_transposed=k_transposed),
        grid=(batch, DIFF_HEADS, nq),
        in_specs=[pl.BlockSpec(memory_space=pltpu.SMEM),
                  pl.BlockSpec((4, DIFF_QK), lambda b, h, qi: (0, 0)),
                  pl.BlockSpec((1, DIFF_V), lambda b, h, qi: (0, 0)),
                  pl.BlockSpec((tq, hd), lambda b, h, qi: (qb0 + b * nq + qi, h)),
                  k_spec,
                  pl.BlockSpec((t_k, DIFF_V), lambda b, h, qi: (b, h))],
        out_specs=pl.BlockSpec((tq, DIFF_V), lambda b, h, qi: (b * nq + qi, h)),
        out_shape=jax.ShapeDtypeStruct((batch * t_q, DIFF_HEADS * DIFF_V), BF16),
        scratch_shapes=[pltpu.VMEM((tq, tk), F32),
                        pltpu.VMEM((tq, tk) if _diagonal_only(geo) else (8, LANE), F32),
                        pltpu.VMEM((2, 2, tq, tk), F32),
                        pltpu.VMEM((tq, 1), F32), pltpu.VMEM((tq, 1), F32),
                        pltpu.VMEM((tq, DIFF_V), F32),
                        pltpu.VMEM((tq, 1), F32), pltpu.VMEM((tq, 1), F32),
                        pltpu.VMEM((tq, DIFF_V), F32)],
        compiler_params=_params("parallel", "parallel", "arbitrary"),
        name="diff_attn",
    )(slopes, lam_vecs, g_sub.reshape(1, DIFF_V), q2d, k, v2d)


def _top_rows(s, n_rows, k, payload=None):
    pos = lax.broadcasted_iota(jnp.int32, s.shape, 0).astype(F32)
    vals, idxs, pays = [], [], []
    for _ in range(k):
        m = jnp.max(s, axis=0, keepdims=True)
        sel = jnp.min(jnp.where(s == m, pos, float(n_rows)), axis=0, keepdims=True)
        hit = pos == sel
        vals.append(m)
        idxs.append(sel)
        if payload is not None:
            pays.append(jnp.max(jnp.where(hit, payload, -1.0), axis=0, keepdims=True))
        s = jnp.where(hit, -jnp.inf, s)
    cat = lambda xs: jnp.concatenate(xs, axis=0)
    return cat(vals), cat(idxs), (cat(pays) if payload is not None else None)


F32_SUBLANE = 8


def _batcher_pairs(n):
    pairs = []

    def merge(lo, cnt, r):
        step = r * 2
        if step < cnt:
            merge(lo, cnt, step)
            merge(lo + r, cnt, step)
            pairs.extend((i, i + r) for i in range(lo + r, lo + cnt - r, step))
        else:
            pairs.append((lo, lo + r))

    def sort(lo, cnt):
        if cnt > 1:
            sort(lo, cnt // 2)
            sort(lo + cnt // 2, cnt // 2)
            merge(lo, cnt, 1)

    sort(0, n)
    return pairs


def _top_rows_by_merge(s, k):
    n_rows, cols = s.shape
    depth = n_rows // F32_SUBLANE
    assert depth * F32_SUBLANE == n_rows and depth >= k
    sub = lax.broadcasted_iota(jnp.int32, (F32_SUBLANE, cols), 0).astype(F32)
    vals = [s[r * F32_SUBLANE:(r + 1) * F32_SUBLANE] for r in range(depth)]
    idxs = [sub + float(r * F32_SUBLANE) for r in range(depth)]
    for a, b in _batcher_pairs(depth):
        swap = vals[b] > vals[a]
        vals[a], vals[b] = jnp.maximum(vals[a], vals[b]), jnp.minimum(vals[a], vals[b])
        idxs[a], idxs[b] = jnp.where(swap, idxs[b], idxs[a]), jnp.where(swap, idxs[a], idxs[b])
    tie = jnp.zeros_like(sub)
    for r in range(depth - 1):
        tie = jnp.where(vals[r] == vals[r + 1], 1.0, tie)
    out_v, out_i = [], []
    for j in range(k):
        m = jnp.max(vals[0], axis=0, keepdims=True)
        sel = jnp.min(jnp.where(vals[0] == m, idxs[0], float(n_rows)), axis=0, keepdims=True)
        hit = idxs[0] == sel
        out_v.append(m)
        out_i.append(sel)
        for r in range(depth - 1 - j):
            vals[r] = jnp.where(hit, vals[r + 1], vals[r])
            idxs[r] = jnp.where(hit, idxs[r + 1], idxs[r])
    return jnp.concatenate(out_v, axis=0), jnp.concatenate(out_i, axis=0), jnp.max(tie)


def _peer_topk_kernel(q_ref, sk_ref, e_ref, g_ref):
    half = PEER_DKEY // 2
    q = q_ref[...]
    nt = (((1,), (1,)), ((), ()))
    scores = [lax.dot_general(sk_ref[c], q[:, c * half:(c + 1) * half], nt,
                              preferred_element_type=F32) for c in range(2)]
    fast = [_top_rows_by_merge(s, PEER_TOPK) for s in scores]
    _peer_pick(q.shape[0], [f[0] for f in fast], [f[1] for f in fast], e_ref, g_ref)

    @pl.when(jnp.maximum(fast[0][2], fast[1][2]) > 0.0)
    def _():
        slow = [_top_rows(s, N_KEYS, PEER_TOPK) for s in scores]
        _peer_pick(q.shape[0], [f[0] for f in slow], [f[1] for f in slow], e_ref, g_ref)


def _peer_pick(tokens, sv, si, e_ref, g_ref):
    k = PEER_TOPK
    rows = F32_SUBLANE
    assert k == 2 * rows
    a_pos = lax.broadcasted_iota(jnp.int32, (k, tokens), 0).astype(F32)
    a_low = lax.broadcasted_iota(jnp.int32, (rows, tokens), 0).astype(F32)
    a_high = a_low + float(rows)
    base_e = si[0] * float(N_KEYS)

    def depth_row(b):
        n, ids = (k, a_pos) if b == 0 else (rows, a_low)
        live = ids < float(k // (b + 1))
        return (jnp.where(live, sv[0][:n] + sv[1][b:b + 1], -jnp.inf),
                jnp.where(live, base_e[:n] + si[1][b:b + 1], -1.0))

    vals, pays = map(list, zip(*[depth_row(b) for b in range(k)]))
    out_v, out_e = [], []
    for j in range(k):
        m = jnp.max(vals[0], axis=0, keepdims=True)
        sel = jnp.min(jnp.where(vals[0] == m, a_pos, float(k)), axis=0, keepdims=True)
        hit = a_pos == sel
        out_v.append(m)
        out_e.append(jnp.max(jnp.where(hit, pays[0], -1.0), axis=0, keepdims=True))
        if j == k - 1:
            break
        low = a_low == sel
        vals[0] = jnp.concatenate([jnp.where(low, vals[1], vals[0][:rows]),
                                   jnp.where(a_high == sel, -jnp.inf, vals[0][rows:])], axis=0)
        pays[0] = jnp.concatenate([jnp.where(low, pays[1], pays[0][:rows]), pays[0][rows:]], axis=0)
        for r in range(1, k - 1 - j):
            vals[r] = jnp.where(low, vals[r + 1], vals[r])
            pays[r] = jnp.where(low, pays[r + 1], pays[r])
    fv, fe = jnp.concatenate(out_v, axis=0), jnp.concatenate(out_e, axis=0)
    p = jnp.exp(fv - fv[0:1])
    g_ref[...] = p / jnp.sum(p, axis=0, keepdims=True)
    e_ref[...] = fe.astype(jnp.int32)


def _peer_topk(qb, sub_keys_b, *, tn=256):
    m = qb.shape[0]
    tn = _tile(m, tn, LANE)
    half = PEER_DKEY // 2
    out_spec = pl.BlockSpec((None, PEER_TOPK, tn), lambda i, h: (h, 0, i))
    return pl.pallas_call(
        _peer_topk_kernel,
        grid=(m // tn, PEER_HEADS),
        in_specs=[pl.BlockSpec((tn, PEER_DKEY), lambda i, h: (i, h)),
                  pl.BlockSpec((None, 2, N_KEYS, half), lambda i, h: (h, 0, 0, 0))],
        out_specs=[out_spec, out_spec],
        out_shape=[jax.ShapeDtypeStruct((PEER_HEADS, PEER_TOPK, m), jnp.int32),
                   jax.ShapeDtypeStruct((PEER_HEADS, PEER_TOPK, m), F32)],
        compiler_params=_params("parallel", "parallel"),
        name="peer_topk",
    )(qb, sub_keys_b)


GATE_ROWS = 8
GATE_UNROLL = 32


def _peer_gate_kernel(e_ref, g_ref, o_ref, *, rows):
    npick = PEER_HEADS * PEER_TOPK
    key = lax.broadcasted_iota(jnp.int32, (N_KEYS, npick), 0).astype(F32).astype(BF16)
    one, zero = jnp.ones((), BF16), jnp.zeros((), BF16)
    nt = (((1,), (1,)), ((), ()))
    shift = N_KEYS.bit_length() - 1

    def body(blk, carry):
        for u in range(GATE_UNROLL):
            n = blk * GATE_UNROLL + u
            e = e_ref[pl.ds(n, 1), :]
            g = g_ref[pl.ds(n, 1), :].astype(BF16)
            e_first = (e >> shift).astype(F32).astype(BF16)
            e_second = (e & (N_KEYS - 1)).astype(F32).astype(BF16)
            first = jnp.where(key == e_first, one, zero)
            second = jnp.where(key == e_second, g, zero)
            w = lax.dot_general(first, second, nt, preferred_element_type=F32)
            o_ref[:, n] = w.reshape(N_KEYS // GATE_ROWS, GATE_ROWS, N_KEYS)
        return carry

    lax.fori_loop(0, rows // GATE_UNROLL, body, 0)


def _peer_gate(e_t, g_t, *, tb=64):
    m, npick = e_t.shape
    assert N_KEYS & (N_KEYS - 1) == 0
    tb = _tile(m, tb, GATE_UNROLL)
    nib = N_KEYS // GATE_ROWS
    return pl.pallas_call(
        functools.partial(_peer_gate_kernel, rows=tb),
        grid=(m // tb,),
        in_specs=[pl.BlockSpec((tb, npick), lambda i: (i, 0)),
                  pl.BlockSpec((tb, npick), lambda i: (i, 0))],
        out_specs=pl.BlockSpec((nib, tb, GATE_ROWS, N_KEYS), lambda i: (0, i, 0, 0)),
        out_shape=jax.ShapeDtypeStruct((nib, m, GATE_ROWS, N_KEYS), F32),
        compiler_params=_params("parallel"),
        name="peer_gate",
    )(e_t, g_t)


def _gelu_tanh(x):
    c = np.sqrt(2 / np.pi).astype(np.float32)
    return x * (0.5 * (1.0 + jnp.tanh(c * (x + 0.044715 * (x * x * x)))))


def _peer_dense_kernel(xb_ref, xf_ref, wd_ref, ut_ref, v_ref, g_ref, b_ref,
                       of_ref, ob_ref, h_buf, w_ref, *, ne, tn):
    s = pl.program_id(0)
    e_prev = lax.rem(jnp.maximum(s - 1, 0), ne)
    slot = lax.rem(s, 2)

    @pl.when(s == 0)
    def _():
        h_buf[1] = jnp.zeros(h_buf.shape[1:], F32)

    @pl.when(e_prev == 0)
    def _():
        of_ref[...] = jnp.zeros_like(of_ref)

    def work(cur):
        h_buf[cur] = jnp.dot(xb_ref[...], ut_ref[...], preferred_element_type=F32)
        h = h_buf[1 - cur]
        for r in range(GATE_ROWS):
            lanes = slice(r * N_KEYS, (r + 1) * N_KEYS)
            gate = wd_ref[pl.ds(r, tn, stride=GATE_ROWS), :]
            w_ref[:, lanes] = (gate * _gelu_tanh(h[:, lanes])).astype(BF16)
        of_ref[...] += jnp.dot(w_ref[...], v_ref[...], preferred_element_type=F32)

    for cur in range(2):
        pl.when(slot == cur)(functools.partial(work, cur))

    @pl.when(jnp.logical_and(s > 0, e_prev == ne - 1))
    def _():
        _write_ln(ALPHA * xf_ref[...] + of_ref[...], g_ref, b_ref, of_ref, ob_ref)


def _peer_dense(xb, xf, wd, u_t, v, layer, g, b, *, tn=528):
    m, d = xb.shape
    ne = wd.shape[0]
    te = GATE_ROWS * N_KEYS
    assert v.shape[1] == ne * te
    tn = _tile(m, tn)
    n_blocks = (m // tn) * ne
    made = lambda s: jnp.minimum(s, n_blocks - 1)
    used = lambda s: jnp.maximum(s - 1, 0)
    tile_used = lambda s: (used(s) // ne, 0)
    return pl.pallas_call(
        functools.partial(_peer_dense_kernel, ne=ne, tn=tn),
        grid=(n_blocks + 1,),
        in_specs=[pl.BlockSpec((tn, d), lambda s: (made(s) // ne, 0)),
                  pl.BlockSpec((tn, d), tile_used),
                  pl.BlockSpec((None, tn * GATE_ROWS, N_KEYS),
                               lambda s: (used(s) % ne, used(s) // ne, 0)),
                  pl.BlockSpec((None, d, te), lambda s: (layer, 0, made(s) % ne)),
                  pl.BlockSpec((None, te, d), lambda s: (layer, used(s) % ne, 0)),
                  _resident((1, d)), _resident((1, d))],
        out_specs=[pl.BlockSpec((tn, d), tile_used), pl.BlockSpec((tn, d), tile_used)],
        out_shape=[jax.ShapeDtypeStruct((m, d), F32), jax.ShapeDtypeStruct((m, d), BF16)],
        scratch_shapes=[pltpu.VMEM((2, tn, te), F32), pltpu.VMEM((tn, te), BF16)],
        compiler_params=_params("arbitrary"),
        name="peer_dense",
    )(xb, xf, wd, u_t, v, g.reshape(1, d), b.reshape(1, d))


def _peer_layer(xf, xb, w_query, sub_keys, u_t_all, v_all, layer, ln_g, ln_b):
    m = xf.shape[0]
    npick = PEER_HEADS * PEER_TOPK
    (qb,) = _mm(xb, w_query.astype(BF16), [BF16])
    eidx, gates = _peer_topk(qb, sub_keys.astype(BF16))
    e_t = eidx.reshape(npick, m).T
    g_t = gates.reshape(npick, m).T
    wd = _peer_gate(e_t, g_t).reshape(N_KEYS // GATE_ROWS, m * GATE_ROWS, N_KEYS)
    return _peer_dense(xb, xf, wd, u_t_all, v_all, layer, ln_g, ln_b)


def _rope_tables(pos):
    half = MLA_ROPE // 2
    inv = ROPE_THETA ** (-jnp.arange(half, dtype=jnp.float32) / half)
    ang = pos.astype(jnp.float32)[:, None] * inv
    cos, sin = jnp.cos(ang), jnp.sin(ang)
    z = jnp.zeros_like(cos)
    return (jnp.concatenate([cos, cos, z, z], -1), jnp.concatenate([-sin, z, z, z], -1),
            jnp.concatenate([z, sin, z, z], -1))


def kernel(x_prompt, x_sample, cache_mla_ckv, cache_mla_krope, cache_diff_k, cache_diff_v,
           mla_w_dqkv, mla_g_q, mla_w_uq, mla_g_kv, mla_w_ukv, mla_w_o,
           diff_w_qkv, diff_lam_q1, diff_lam_k1, diff_lam_q2, diff_lam_k2, diff_g_sub, diff_w_o,
           peer_w_query, peer_sub_keys, peer_u, peer_v,
           ln_mix_g, ln_mix_b, ln_ffn_g, ln_ffn_b):
    bp, tp, d = x_prompt.shape
    bs, ts, _ = x_sample.shape
    past = cache_mla_ckv.shape[2]
    mp, ms = bp * tp, bs * ts
    tks = past + ts
    tkp = -(-tks // LANE) * LANE
    tq_p = _tile(tp, 512, LANE)

    def pad_keys(a):
        return jnp.pad(a, ((0, 0), (0, tkp - tks), (0, 0))).reshape(bs * tkp, a.shape[-1])

    u_t_all = peer_u.astype(BF16).swapaxes(1, 2)
    v_all = peer_v.astype(BF16)
    xf = jnp.concatenate([x_prompt.reshape(mp, d), x_sample.reshape(ms, d)], 0)
    pos = jnp.concatenate([jnp.tile(jnp.arange(tp), bp), jnp.tile(past + jnp.arange(ts), bs)])
    cos, sin_a, sin_b = _rope_tables(pos)

    j = 0
    w_dqkv = jnp.pad(mla_w_dqkv[j], ((0, 0), (0, LANE - MLA_ROPE))).astype(BF16)
    cq, ckv_f, ckv_b, kr_f, kr_b = _mla_proj(xf, w_dqkv, mla_g_q[j], mla_g_kv[j], cos, sin_a, sin_b)
    hq = MLA_NOPE + MLA_ROPE
    w_uq = jnp.pad(mla_w_uq[j].reshape(MLA_Q_LORA, MLA_HEADS, hq),
                   ((0, 0), (0, 0), (0, 2 * LANE - hq))).reshape(MLA_Q_LORA, MLA_HEADS * 2 * LANE)
    q = _q_proj(cq, w_uq.astype(BF16), cos, sin_a, sin_b)
    w_ukv = mla_w_ukv[j].reshape(MLA_KV_LORA, MLA_HEADS, MLA_NOPE + MLA_V)
    w_uk_t = w_ukv[:, :, :MLA_NOPE].reshape(MLA_KV_LORA, -1).T.astype(BF16)
    w_uv = w_ukv[:, :, MLA_NOPE:].reshape(MLA_KV_LORA, -1).astype(BF16)
    lora_tiles = dict(tm=1024, tn=2048)
    (knt_p,) = _mm(w_uk_t, ckv_b[:mp].T, [BF16], **lora_tiles)
    (v_p,) = _mm(ckv_b[:mp], w_uv, [BF16], **lora_tiles)
    o_p = _mla_attention(q, knt_p, kr_b[:mp].T, v_p, batch=bp, t_q=tp, t_k=tp,
                         t_valid=tp, q_row0=0, q_off=0, tq=tq_p, tk=tq_p)
    ckv_cat = pad_keys(jnp.concatenate([cache_mla_ckv[j].astype(BF16),
                                        ckv_b[mp:].reshape(bs, ts, MLA_KV_LORA)], 1))
    kr_cache = jnp.pad(cache_mla_krope[j], ((0, 0), (0, 0), (0, LANE - MLA_ROPE))).astype(BF16)
    kr_cat = pad_keys(jnp.concatenate([kr_cache, kr_b[mp:].reshape(bs, ts, LANE)], 1))
    (knt_s,) = _mm(w_uk_t, ckv_cat.T, [BF16], **lora_tiles)
    (v_s,) = _mm(ckv_cat, w_uv, [BF16], **lora_tiles)
    o_s = _mla_attention(q, knt_s, kr_cat.T, v_s, batch=bs, t_q=ts, t_k=tkp,
                         t_valid=tks, q_row0=mp, q_off=past, tq=ts, tk=tkp)
    o = jnp.concatenate([o_p, o_s], 0)
    xf, xb = _mm_ln(o, mla_w_o[j].astype(BF16), xf, ln_mix_g[0], ln_mix_b[0])
    xf, xb = _peer_layer(xf, xb, peer_w_query[0], peer_sub_keys[0], u_t_all, v_all, 0,
                         ln_ffn_g[0], ln_ffn_b[0])

    i = 1
    lam_init = 0.8 - 0.6 * math.exp(-0.3 * i)
    nqk = DIFF_HEADS * 2 * DIFF_QK
    w_qkv = diff_w_qkv[j].astype(BF16)
    (dq,) = _mm(xb, w_qkv[:, :nqk], [BF16])
    dk_f, dk_b = _mm(xb, w_qkv[:, nqk:2 * nqk], [F32, BF16])
    dv_f, dv_b = _mm(xb, w_qkv[:, 2 * nqk:], [F32, BF16])
    slopes = 2.0 ** (-8.0 * jnp.arange(1, DIFF_HEADS + 1, dtype=jnp.float32) / DIFF_HEADS)
    lam_vecs = jnp.stack([diff_lam_q1[j], diff_lam_k1[j], diff_lam_q2[j], diff_lam_k2[j]])
    diff_args = (slopes, lam_vecs, diff_g_sub[j])
    o_p = _diff_attention(dq, dk_b[:mp].T, dv_b, *diff_args, k_transposed=True, batch=bp,
                          t_q=tp, t_k=tp, q_row0=0, q_off=0, tq=tq_p, tk=tq_p, lam_init=lam_init)
    k_cat = jnp.concatenate([cache_diff_k[j].reshape(bs, past, nqk).astype(BF16),
                             dk_b[mp:].reshape(bs, ts, nqk)], 1).reshape(bs * tks, nqk)
    v_cat = jnp.concatenate([cache_diff_v[j].reshape(bs, past, -1).astype(BF16),
                             dv_b[mp:].reshape(bs, ts, -1)], 1).reshape(bs * tks, -1)
    o_s = _diff_attention(dq, k_cat, v_cat, *diff_args, k_transposed=False, batch=bs,
                          t_q=ts, t_k=tks, q_row0=mp, q_off=past, tq=ts, tk=tks, lam_init=lam_init)
    o = jnp.concatenate([o_p, o_s], 0)
    xf, xb = _mm_ln(o, diff_w_o[j].astype(BF16), xf, ln_mix_g[1], ln_mix_b[1])
    xf, xb = _peer_layer(xf, xb, peer_w_query[1], peer_sub_keys[1], u_t_all, v_all, 1,
                         ln_ffn_g[1], ln_ffn_b[1])

    kr_f = kr_f[:, :MLA_ROPE]
    return (xf[:mp].reshape(bp, tp, d), xf[mp:].reshape(bs, ts, d),
            ckv_f[:mp].reshape(1, bp, tp, -1), kr_f[:mp].reshape(1, bp, tp, -1),
            dk_f[:mp].reshape(1, bp, tp, DIFF_HEADS, -1), dv_f[:mp].reshape(1, bp, tp, DIFF_HEADS, -1),
            ckv_f[mp:].reshape(1, bs, ts, -1), kr_f[mp:].reshape(1, bs, ts, -1),
            dk_f[mp:].reshape(1, bs, ts, DIFF_HEADS, -1), dv_f[mp:].reshape(1, bs, ts, DIFF_HEADS, -1))
```

```python
import functools
import math

import jax
import jax.numpy as jnp
import numpy as np
from jax import lax
from jax.experimental import pallas as pl
from jax.experimental.pallas import tpu as pltpu

F32 = jnp.float32
BF16 = jnp.bfloat16

DEPTH = 2
CHUNK = 64
ALPHA = (2 * DEPTH) ** 0.25
LN_EPS = 1e-5
RMS_EPS = 1e-6
NEG = -1e30
ROPE_THETA = 10000.0
MLA_HEADS = 16
MLA_Q_LORA = 512
MLA_KV_LORA = 512
MLA_NOPE = 128
MLA_ROPE = 64
MLA_V = 128
DIFF_HEADS = 8
DIFF_QK = 128
DIFF_V = 256
PEER_HEADS = 8
PEER_TOPK = 16
N_KEYS = 128
PEER_DKEY = 256

LANE = 128
BF16_SUBLANE = 16
VMEM_LIMIT = 56 << 20


def _tile(n, target, mult=BF16_SUBLANE):
    for t in range(min(n, target), 0, -1):
        if n % t == 0 and t % mult == 0:
            return t
    raise ValueError(f"no tile for {n} (target {target}, multiple of {mult})")


def _params(*sem):
    return pltpu.CompilerParams(dimension_semantics=sem, vmem_limit_bytes=VMEM_LIMIT)


def _layer_norm(y, g, b):
    mu = jnp.mean(y, axis=-1, keepdims=True)
    d = y - mu
    var = jnp.mean(d * d, axis=-1, keepdims=True)
    return d * lax.rsqrt(var + LN_EPS) * g + b


def _rms_norm(y, g):
    return y * lax.rsqrt(jnp.mean(y * y, axis=-1, keepdims=True) + RMS_EPS) * g


def _rope128(r, cos, sin_a, sin_b):
    return r * cos + pltpu.roll(r, 96, 1) * sin_a + pltpu.roll(r, 32, 1) * sin_b


def _mm_kernel(x_ref, w_ref, *outs):
    y = jnp.dot(x_ref[...], w_ref[...], preferred_element_type=F32)
    for o in outs:
        o[...] = y.astype(o.dtype)


def _mm(x, w, out_dtypes, *, tm=528, tn=1024):
    m, kdim = x.shape
    n = w.shape[1]
    tm, tn = _tile(m, tm), _tile(n, tn, LANE)
    outs = pl.pallas_call(
        _mm_kernel,
        grid=(m // tm, n // tn),
        in_specs=[pl.BlockSpec((tm, kdim), lambda i, j: (i, 0)),
                  pl.BlockSpec((kdim, tn), lambda i, j: (0, j))],
        out_specs=[pl.BlockSpec((tm, tn), lambda i, j: (i, j)) for _ in out_dtypes],
        out_shape=[jax.ShapeDtypeStruct((m, n), dt) for dt in out_dtypes],
        compiler_params=_params("parallel", "parallel"),
        name="mm",
    )(x, w)
    return outs


def _resident(shape):
    return pl.BlockSpec(shape, lambda *_: (0,) * len(shape), pipeline_mode=pl.Buffered(1))


def _write_ln(y, g_ref, b_ref, of_ref, ob_ref):
    y = _layer_norm(y, g_ref[...], b_ref[...])
    of_ref[...] = y
    ob_ref[...] = y.astype(BF16)


def _mm_ln_kernel(a_ref, w_ref, r_ref, g_ref, b_ref, of_ref, ob_ref):
    mix = jnp.dot(a_ref[...], w_ref[...], preferred_element_type=F32)
    _write_ln(ALPHA * r_ref[...] + mix, g_ref, b_ref, of_ref, ob_ref)


def _mm_ln(a, w, resid, g, b, *, tm=528):
    m, kdim = a.shape
    d = w.shape[1]
    tm = _tile(m, tm)
    row = lambda i: (i, 0)
    return pl.pallas_call(
        _mm_ln_kernel,
        grid=(m // tm,),
        in_specs=[pl.BlockSpec((tm, kdim), row), _resident((kdim, d)),
                  pl.BlockSpec((tm, d), row), _resident((1, d)), _resident((1, d))],
        out_specs=[pl.BlockSpec((tm, d), row), pl.BlockSpec((tm, d), row)],
        out_shape=[jax.ShapeDtypeStruct((m, d), F32), jax.ShapeDtypeStruct((m, d), BF16)],
        compiler_params=_params("parallel"),
        name="mm_ln",
    )(a, w, resid, g.reshape(1, d), b.reshape(1, d))


def _mla_proj_kernel(x_ref, w_ref, gq_ref, gkv_ref, cos_ref, sa_ref, sb_ref,
                     cq_ref, ckvf_ref, ckvb_ref, krf_ref, krb_ref, *, nq, nkv):
    lat = jnp.dot(x_ref[...].astype(BF16), w_ref[...], preferred_element_type=F32)
    cq_ref[...] = _rms_norm(lat[:, :nq], gq_ref[...]).astype(BF16)
    ckv = _rms_norm(lat[:, nq:nq + nkv], gkv_ref[...])
    ckvf_ref[...] = ckv
    ckvb_ref[...] = ckv.astype(BF16)
    kr = _rope128(lat[:, nq + nkv:], cos_ref[...], sa_ref[...], sb_ref[...])
    krf_ref[...] = kr
    krb_ref[...] = kr.astype(BF16)


def _mla_proj(x, w_pad, g_q, g_kv, cos, sin_a, sin_b, *, tm=528):
    m, kdim = x.shape
    nq, nkv = g_q.shape[0], g_kv.shape[0]
    n = w_pad.shape[1]
    assert n == nq + nkv + LANE
    tm = _tile(m, tm)
    row = lambda i: (i, 0)
    return pl.pallas_call(
        functools.partial(_mla_proj_kernel, nq=nq, nkv=nkv),
        grid=(m // tm,),
        in_specs=[pl.BlockSpec((tm, kdim), row), _resident((kdim, n)),
                  _resident((1, nq)), _resident((1, nkv)),
                  pl.BlockSpec((tm, LANE), row), pl.BlockSpec((tm, LANE), row),
                  pl.BlockSpec((tm, LANE), row)],
        out_specs=[pl.BlockSpec((tm, nq), row), pl.BlockSpec((tm, nkv), row),
                   pl.BlockSpec((tm, nkv), row), pl.BlockSpec((tm, LANE), row),
                   pl.BlockSpec((tm, LANE), row)],
        out_shape=[jax.ShapeDtypeStruct((m, nq), BF16), jax.ShapeDtypeStruct((m, nkv), F32),
                   jax.ShapeDtypeStruct((m, nkv), BF16), jax.ShapeDtypeStruct((m, LANE), F32),
                   jax.ShapeDtypeStruct((m, LANE), BF16)],
        compiler_params=_params("parallel"),
        name="mla_proj",
    )(x, w_pad, g_q.reshape(1, nq), g_kv.reshape(1, nkv), cos, sin_a, sin_b)


def _q_proj_kernel(c_ref, w_ref, cos_ref, sa_ref, sb_ref, o_ref, *, heads):
    y = jnp.dot(c_ref[...], w_ref[...], preferred_element_type=F32)
    for h in range(heads):
        lo = h * 2 * LANE
        o_ref[:, lo:lo + LANE] = y[:, lo:lo + LANE].astype(BF16)
        r = _rope128(y[:, lo + LANE:lo + 2 * LANE], cos_ref[...], sa_ref[...], sb_ref[...])
        o_ref[:, lo + LANE:lo + 2 * LANE] = r.astype(BF16)


def _q_proj(cq, w_arr, cos, sin_a, sin_b, *, tm=528, heads_per_step=8):
    m, kdim = cq.shape
    n = w_arr.shape[1]
    tm = _tile(m, tm)
    tn = heads_per_step * 2 * LANE
    row = lambda i, j: (i, 0)
    return pl.pallas_call(
        functools.partial(_q_proj_kernel, heads=heads_per_step),
        grid=(m // tm, n // tn),
        in_specs=[pl.BlockSpec((tm, kdim), row),
                  pl.BlockSpec((kdim, tn), lambda i, j: (0, j)),
                  pl.BlockSpec((tm, LANE), row), pl.BlockSpec((tm, LANE), row),
                  pl.BlockSpec((tm, LANE), row)],
        out_specs=pl.BlockSpec((tm, tn), lambda i, j: (i, j)),
        out_shape=jax.ShapeDtypeStruct((m, n), BF16),
        compiler_params=_params("parallel", "parallel"),
        name="q_proj",
    )(cq, w_arr, cos, sin_a, sin_b)


Q_CHAIN_ROWS = 256


def _row_parts(tq):
    n = max(1, tq // Q_CHAIN_ROWS)
    assert tq % n == 0
    return [slice(p * (tq // n), (p + 1) * (tq // n)) for p in range(n)]


def _visible_blocks(qi, *, q_off, tq, tk, nk):
    q_first = q_off + qi * tq
    q_last = q_first + tq - 1
    n_before = jnp.minimum(q_first // tk, nk)
    n_vis = jnp.minimum(((q_last // CHUNK + 1) * CHUNK + tk - 1) // tk, nk)
    return n_before, n_vis


def _positions(qi, ki, rows, *, q_off, tq, tk):
    shape = (rows.stop - rows.start, tk)
    qp = q_off + qi * tq + rows.start + lax.broadcasted_iota(jnp.int32, shape, 0)
    kp = ki * tk + lax.broadcasted_iota(jnp.int32, shape, 1)
    return qp, kp


def _key_mask(qi, ki, rows, geo):
    qp, kp = _positions(qi, ki, rows, q_off=geo["q_off"], tq=geo["tq"], tk=geo["tk"])
    k_chunk = kp // CHUNK
    if geo["t_valid"] < geo["nk"] * geo["tk"]:
        k_chunk = jnp.where(kp < geo["t_valid"], k_chunk, jnp.iinfo(jnp.int32).max)
    return k_chunk <= (qp // CHUNK), qp, kp


def _diagonal_only(geo):
    return (geo["tq"] == geo["tk"] and geo["q_off"] % geo["tk"] == 0
            and geo["t_valid"] == geo["nk"] * geo["tk"])


def _diagonal_visible(tq, tk):
    row = lax.broadcasted_iota(jnp.int32, (tq, tk), 0)
    col = lax.broadcasted_iota(jnp.int32, (tq, tk), 1)
    return (col // CHUNK) <= (row // CHUNK), row, col


def _for_blocks(lo, hi, step):
    lax.fori_loop(lo, hi, lambda ki, carry: (step(ki), carry)[1], 0)


def _online_softmax_step(s, v, m_ref, l_ref, acc_ref, rows):
    m_prev = m_ref[rows, :]
    m_new = jnp.maximum(m_prev, jnp.max(s, axis=-1, keepdims=True))
    alpha = jnp.exp(m_prev - m_new)
    p = jnp.exp(s - m_new)
    l_ref[rows, :] = alpha * l_ref[rows, :] + jnp.sum(p, axis=-1, keepdims=True)
    acc_ref[rows, :] = (alpha * acc_ref[rows, :]
                        + jnp.dot(p.astype(BF16), v, preferred_element_type=F32))
    m_ref[rows, :] = m_new


def _mla_attn_kernel(q_ref, knt_ref, krt_ref, v_ref, o_ref, kcat_t, diag_ref, s_buf,
                     m_ref, l_ref, acc_ref, *, geo, scale):
    qi = pl.program_id(2)
    tk = geo["tk"]
    parts = _row_parts(geo["tq"])
    diagonal_only = _diagonal_only(geo)

    @pl.when(qi == 0)
    def _():
        kcat_t[:MLA_NOPE, :] = knt_ref[...]
        kcat_t[MLA_NOPE:, :] = krt_ref[...]
        if diagonal_only:
            diag_ref[...] = jnp.where(_diagonal_visible(geo["tq"], tk)[0], 0.0, NEG)

    m_ref[...] = jnp.full_like(m_ref, -jnp.inf)
    l_ref[...] = jnp.zeros_like(l_ref)
    acc_ref[...] = jnp.zeros_like(acc_ref)
    n_before, n_vis = _visible_blocks(qi, q_off=geo["q_off"], tq=geo["tq"], tk=tk, nk=geo["nk"])

    def scores(ki, slot):
        r0 = pl.multiple_of(ki * tk, tk)
        k_t = kcat_t[:, pl.ds(r0, tk)]
        for rows in parts:
            s_buf[slot, rows, :] = jnp.dot(q_ref[rows, :], k_t, preferred_element_type=F32) * scale

    def step(ki, masked):
        slot = lax.rem(ki, 2)
        r0 = pl.multiple_of(ki * tk, tk)
        v = v_ref[pl.ds(r0, tk), :]
        for rows in parts:
            s = s_buf[slot, rows, :]
            if masked and diagonal_only:
                s = jnp.where(diag_ref[rows, :] < 0.0, NEG, s)
            elif masked:
                s = jnp.where(_key_mask(qi, ki, rows, geo)[0], s, NEG)
            _online_softmax_step(s, v, m_ref, l_ref, acc_ref, rows)
        if geo["nk"] > 1:
            scores(jnp.minimum(ki + 1, n_vis - 1), 1 - slot)

    scores(0, 0)
    _for_blocks(0, n_before, functools.partial(step, masked=False))
    _for_blocks(n_before, n_vis, functools.partial(step, masked=True))
    o_ref[...] = (acc_ref[...] / l_ref[...]).astype(BF16)


def _mla_attention(q2d, knt, krt, v2d, *, batch, t_q, t_k, t_valid, q_row0, q_off, tq, tk):
    nq, nk = t_q // tq, t_k // tk
    assert q_row0 % tq == 0 and t_k % tk == 0 and tk % LANE == 0
    geo = dict(q_off=q_off, tq=tq, tk=tk, nk=nk, t_valid=t_valid)
    qb0 = q_row0 // tq
    return pl.pallas_call(
        functools.partial(_mla_attn_kernel, geo=geo, scale=(MLA_NOPE + MLA_ROPE) ** -0.5),
        grid=(batch, MLA_HEADS, nq),
        in_specs=[pl.BlockSpec((tq, 2 * LANE), lambda b, h, qi: (qb0 + b * nq + qi, h)),
                  pl.BlockSpec((MLA_NOPE, t_k), lambda b, h, qi: (h, b)),
                  pl.BlockSpec((LANE, t_k), lambda b, h, qi: (0, b)),
                  pl.BlockSpec((t_k, MLA_V), lambda b, h, qi: (b, h))],
        out_specs=pl.BlockSpec((tq, MLA_V), lambda b, h, qi: (b * nq + qi, h)),
        out_shape=jax.ShapeDtypeStruct((batch * t_q, MLA_HEADS * MLA_V), BF16),
        scratch_shapes=[pltpu.VMEM((2 * LANE, t_k), BF16),
                        pltpu.VMEM((tq, tk) if _diagonal_only(geo) else (8, LANE), F32),
                        pltpu.VMEM((2, tq, tk), F32),
                        pltpu.VMEM((tq, 1), F32), pltpu.VMEM((tq, 1), F32),
                        pltpu.VMEM((tq, MLA_V), F32)],
        compiler_params=_params("parallel", "parallel", "arbitrary"),
        name="mla_attn",
    )(q2d, knt, krt, v2d)


def _diff_attn_kernel(slope_ref, lam_ref, gsub_ref, q_ref, k_ref, v_ref, o_ref,
                      b0_ref, diag_ref, s_buf, m1, l1, a1, m2, l2, a2,
                      *, geo, scale, lam_init, k_transposed):
    h, qi = pl.program_id(1), pl.program_id(2)
    tq, tk = geo["tq"], geo["tk"]
    parts = _row_parts(tq)
    diagonal_only = _diagonal_only(geo)
    stats = ((m1, l1, a1), (m2, l2, a2))
    for m, l, a in stats:
        m[...] = jnp.full_like(m, -jnp.inf)
        l[...] = jnp.zeros_like(l)
        a[...] = jnp.zeros_like(a)
    slope = slope_ref[h]
    q_first = geo["q_off"] + qi * tq

    @pl.when(qi == 0)
    def _():
        visible, row, col = _diagonal_visible(tq, tk)
        b0_ref[...] = -slope * (row - col).astype(F32)
        if diagonal_only:
            diag_ref[...] = jnp.where(visible, -slope * jnp.abs(row - col).astype(F32), NEG)

    n_before, n_vis = _visible_blocks(qi, q_off=geo["q_off"], tq=tq, tk=tk, nk=geo["nk"])

    def scores(ki, slot):
        r0 = pl.multiple_of(ki * tk, tk)
        for rows in parts:
            for half in range(2):
                lo = half * DIFF_QK
                q_half = q_ref[rows, lo:lo + DIFF_QK]
                if k_transposed:
                    s = jnp.dot(q_half, k_ref[lo:lo + DIFF_QK, pl.ds(r0, tk)],
                                preferred_element_type=F32)
                else:
                    s = lax.dot_general(q_half, k_ref[pl.ds(r0, tk), lo:lo + DIFF_QK],
                                        (((1,), (1,)), ((), ())), preferred_element_type=F32)
                s_buf[slot, half, rows, :] = s

    def step(ki, masked):
        slot = lax.rem(ki, 2)
        r0 = pl.multiple_of(ki * tk, tk)
        v = v_ref[pl.ds(r0, tk), :]
        for rows in parts:
            if masked and diagonal_only:
                bias = diag_ref[rows, :]
            elif masked:
                mask, qp, kp = _key_mask(qi, ki, rows, geo)
                bias = jnp.where(mask, -slope * jnp.abs(qp - kp).astype(F32), NEG)
            else:
                shift = -slope * jnp.full((1, tk), q_first - r0, jnp.int32).astype(F32)
                bias = b0_ref[rows, :] + shift
            for half, (m, l, a) in enumerate(stats):
                _online_softmax_step(s_buf[slot, half, rows, :] * scale + bias, v, m, l, a, rows)
        if geo["nk"] > 1:
            scores(jnp.minimum(ki + 1, n_vis - 1), 1 - slot)

    scores(0, 0)
    _for_blocks(0, n_before, functools.partial(step, masked=False))
    _for_blocks(n_before, n_vis, functools.partial(step, masked=True))

    lam_v = lam_ref[...]
    lam = (jnp.exp(jnp.sum(lam_v[0:1] * lam_v[1:2], axis=-1, keepdims=True))
           - jnp.exp(jnp.sum(lam_v[2:3] * lam_v[3:4], axis=-1, keepdims=True)) + lam_init)
    o = a1[...] / l1[...] - lam * (a2[...] / l2[...])
    o_ref[...] = (_rms_norm(o, gsub_ref[...]) * (1.0 - lam_init)).astype(BF16)


def _diff_attention(q2d, k, v2d, slopes, lam_vecs, g_sub, *, k_transposed,
                    batch, t_q, t_k, q_row0, q_off, tq, tk, lam_init):
    nq, nk = t_q // tq, t_k // tk
    assert q_row0 % tq == 0 and t_k % tk == 0 and (tk % LANE == 0 or not k_transposed)
    geo = dict(q_off=q_off, tq=tq, tk=tk, nk=nk, t_valid=t_k)
    qb0 = q_row0 // tq
    hd = 2 * DIFF_QK
    k_spec = (pl.BlockSpec((hd, t_k), lambda b, h, qi: (h, b)) if k_transposed
              else pl.BlockSpec((t_k, hd), lambda b, h, qi: (b, h)))
    return pl.pallas_call(
        functools.partial(_diff_attn_kernel, geo=geo, scale=DIFF_QK ** -0.5, lam_init=lam_init,
                          k_transposed=k_transposed),
        grid=(batch, DIFF_HEADS, nq),
        in_specs=[pl.BlockSpec(memory_space=pltpu.SMEM),
                  pl.BlockSpec((4, DIFF_QK), lambda b, h, qi: (0, 0)),
                  pl.BlockSpec((1, DIFF_V), lambda b, h, qi: (0, 0)),
                  pl.BlockSpec((tq, hd), lambda b, h, qi: (qb0 + b * nq + qi, h)),
                  k_spec,
                  pl.BlockSpec((t_k, DIFF_V), lambda b, h, qi: (b, h))],
        out_specs=pl.BlockSpec((tq, DIFF_V), lambda b, h, qi: (b * nq + qi, h)),
        out_shape=jax.ShapeDtypeStruct((batch * t_q, DIFF_HEADS * DIFF_V), BF16),
        scratch_shapes=[pltpu.VMEM((tq, tk), F32),
                        pltpu.VMEM((tq, tk) if _diagonal_only(geo) else (8, LANE), F32),
                        pltpu.VMEM((2, 2, tq, tk), F32),
                        pltpu.VMEM((tq, 1), F32), pltpu.VMEM((tq, 1), F32),
                        pltpu.VMEM((tq, DIFF_V), F32),
                        pltpu.VMEM((tq, 1), F32), pltpu.VMEM((tq, 1), F32),
                        pltpu.VMEM((tq, DIFF_V), F32)],
        compiler_params=_params("parallel", "parallel", "arbitrary"),
        name="diff_attn",
    )(slopes, lam_vecs, g_sub.reshape(1, DIFF_V), q2d, k, v2d)


def _top_rows(s, n_rows, k, payload=None):
    pos = lax.broadcasted_iota(jnp.int32, s.shape, 0).astype(F32)
    vals, idxs, pays = [], [], []
    for _ in range(k):
        m = jnp.max(s, axis=0, keepdims=True)
        sel = jnp.min(jnp.where(s == m, pos, float(n_rows)), axis=0, keepdims=True)
        hit = pos == sel
        vals.append(m)
        idxs.append(sel)
        if payload is not None:
            pays.append(jnp.max(jnp.where(hit, payload, -1.0), axis=0, keepdims=True))
        s = jnp.where(hit, -jnp.inf, s)
    cat = lambda xs: jnp.concatenate(xs, axis=0)
    return cat(vals), cat(idxs), (cat(pays) if payload is not None else None)


F32_SUBLANE = 8


def _batcher_pairs(n):
    pairs = []

    def merge(lo, cnt, r):
        step = r * 2
        if step < cnt:
            merge(lo, cnt, step)
            merge(lo + r, cnt, step)
            pairs.extend((i, i + r) for i in range(lo + r, lo + cnt - r, step))
        else:
            pairs.append((lo, lo + r))

    def sort(lo, cnt):
        if cnt > 1:
            sort(lo, cnt // 2)
            sort(lo + cnt // 2, cnt // 2)
            merge(lo, cnt, 1)

    sort(0, n)
    return pairs


def _top_rows_by_merge(s, k):
    n_rows, cols = s.shape
    depth = n_rows // F32_SUBLANE
    assert depth * F32_SUBLANE == n_rows and depth >= k
    sub = lax.broadcasted_iota(jnp.int32, (F32_SUBLANE, cols), 0).astype(F32)
    vals = [s[r * F32_SUBLANE:(r + 1) * F32_SUBLANE] for r in range(depth)]
    idxs = [sub + float(r * F32_SUBLANE) for r in range(depth)]
    for a, b in _batcher_pairs(depth):
        swap = vals[b] > vals[a]
        vals[a], vals[b] = jnp.maximum(vals[a], vals[b]), jnp.minimum(vals[a], vals[b])
        idxs[a], idxs[b] = jnp.where(swap, idxs[b], idxs[a]), jnp.where(swap, idxs[a], idxs[b])
    tie = jnp.zeros_like(sub)
    for r in range(depth - 1):
        tie = jnp.where(vals[r] == vals[r + 1], 1.0, tie)
    out_v, out_i = [], []
    for j in range(k):
        m = jnp.max(vals[0], axis=0, keepdims=True)
        sel = jnp.min(jnp.where(vals[0] == m, idxs[0], float(n_rows)), axis=0, keepdims=True)
        hit = idxs[0] == sel
        out_v.append(m)
        out_i.append(sel)
        for r in range(depth - 1 - j):
            vals[r] = jnp.where(hit, vals[r + 1], vals[r])
            idxs[r] = jnp.where(hit, idxs[r + 1], idxs[r])
    return jnp.concatenate(out_v, axis=0), jnp.concatenate(out_i, axis=0), jnp.max(tie)


def _peer_topk_kernel(q_ref, sk_ref, e_ref, g_ref):
    half = PEER_DKEY // 2
    q = q_ref[...]
    nt = (((1,), (1,)), ((), ()))
    scores = [lax.dot_general(sk_ref[c], q[:, c * half:(c + 1) * half], nt,
                              preferred_element_type=F32) for c in range(2)]
    fast = [_top_rows_by_merge(s, PEER_TOPK) for s in scores]
    _peer_pick(q.shape[0], [f[0] for f in fast], [f[1] for f in fast], e_ref, g_ref)

    @pl.when(jnp.maximum(fast[0][2], fast[1][2]) > 0.0)
    def _():
        slow = [_top_rows(s, N_KEYS, PEER_TOPK) for s in scores]
        _peer_pick(q.shape[0], [f[0] for f in slow], [f[1] for f in slow], e_ref, g_ref)


def _peer_pick(tokens, sv, si, e_ref, g_ref):
    k = PEER_TOPK
    rows = F32_SUBLANE
    assert k == 2 * rows
    a_pos = lax.broadcasted_iota(jnp.int32, (k, tokens), 0).astype(F32)
    a_low = lax.broadcasted_iota(jnp.int32, (rows, tokens), 0).astype(F32)
    a_high = a_low + float(rows)
    base_e = si[0] * float(N_KEYS)

    def depth_row(b):
        n, ids = (k, a_pos) if b == 0 else (rows, a_low)
        live = ids < float(k // (b + 1))
        return (jnp.where(live, sv[0][:n] + sv[1][b:b + 1], -jnp.inf),
                jnp.where(live, base_e[:n] + si[1][b:b + 1], -1.0))

    vals, pays = map(list, zip(*[depth_row(b) for b in range(k)]))
    out_v, out_e = [], []
    for j in range(k):
        m = jnp.max(vals[0], axis=0, keepdims=True)
        sel = jnp.min(jnp.where(vals[0] == m, a_pos, float(k)), axis=0, keepdims=True)
        hit = a_pos == sel
        out_v.append(m)
        out_e.append(jnp.max(jnp.where(hit, pays[0], -1.0), axis=0, keepdims=True))
        if j == k - 1:
            break
        low = a_low == sel
        vals[0] = jnp.concatenate([jnp.where(low, vals[1], vals[0][:rows]),
                                   jnp.where(a_high == sel, -jnp.inf, vals[0][rows:])], axis=0)
        pays[0] = jnp.concatenate([jnp.where(low, pays[1], pays[0][:rows]), pays[0][rows:]], axis=0)
        for r in range(1, k - 1 - j):
            vals[r] = jnp.where(low, vals[r + 1], vals[r])
            pays[r] = jnp.where(low, pays[r + 1], pays[r])
    fv, fe = jnp.concatenate(out_v, axis=0), jnp.concatenate(out_e, axis=0)
    p = jnp.exp(fv - fv[0:1])
    g_ref[...] = p / jnp.sum(p, axis=0, keepdims=True)
    e_ref[...] = fe.astype(jnp.int32)


def _peer_topk(qb, sub_keys_b, *, tn=256):
    m = qb.shape[0]
    tn = _tile(m, tn, LANE)
    half = PEER_DKEY // 2
    out_spec = pl.BlockSpec((None, PEER_TOPK, tn), lambda i, h: (h, 0, i))
    return pl.pallas_call(
        _peer_topk_kernel,
        grid=(m // tn, PEER_HEADS),
        in_specs=[pl.BlockSpec((tn, PEER_DKEY), lambda i, h: (i, h)),
                  pl.BlockSpec((None, 2, N_KEYS, half), lambda i, h: (h, 0, 0, 0))],
        out_specs=[out_spec, out_spec],
        out_shape=[jax.ShapeDtypeStruct((PEER_HEADS, PEER_TOPK, m), jnp.int32),
                   jax.ShapeDtypeStruct((PEER_HEADS, PEER_TOPK, m), F32)],
        compiler_params=_params("parallel", "parallel"),
        name="peer_topk",
    )(qb, sub_keys_b)


GATE_ROWS = 8
GATE_UNROLL = 64


def _peer_gate_kernel(e_ref, g_ref, o_ref, *, rows):
    npick = PEER_HEADS * PEER_TOPK
    key = lax.broadcasted_iota(jnp.int32, (N_KEYS, npick), 0).astype(F32).astype(BF16)
    one, zero = jnp.ones((), BF16), jnp.zeros((), BF16)
    nt = (((1,), (1,)), ((), ()))
    shift = N_KEYS.bit_length() - 1

    def body(blk, carry):
        for u in range(GATE_UNROLL):
            n = blk * GATE_UNROLL + u
            e = e_ref[pl.ds(n, 1), :]
            g = g_ref[pl.ds(n, 1), :].astype(BF16)
            e_first = (e >> shift).astype(F32).astype(BF16)
            e_second = (e & (N_KEYS - 1)).astype(F32).astype(BF16)
            first = jnp.where(key == e_first, one, zero)
            second = jnp.where(key == e_second, g, zero)
            w = lax.dot_general(first, second, nt, preferred_element_type=F32)
            o_ref[:, n] = w.reshape(N_KEYS // GATE_ROWS, GATE_ROWS, N_KEYS)
        return carry

    lax.fori_loop(0, rows // GATE_UNROLL, body, 0)


def _peer_gate(e_t, g_t, *, tb=64):
    m, npick = e_t.shape
    assert N_KEYS & (N_KEYS - 1) == 0
    tb = _tile(m, tb, GATE_UNROLL)
    nib = N_KEYS // GATE_ROWS
    return pl.pallas_call(
        functools.partial(_peer_gate_kernel, rows=tb),
        grid=(m // tb,),
        in_specs=[pl.BlockSpec((tb, npick), lambda i: (i, 0)),
                  pl.BlockSpec((tb, npick), lambda i: (i, 0))],
        out_specs=pl.BlockSpec((nib, tb, GATE_ROWS, N_KEYS), lambda i: (0, i, 0, 0)),
        out_shape=jax.ShapeDtypeStruct((nib, m, GATE_ROWS, N_KEYS), F32),
        compiler_params=_params("parallel"),
        name="peer_gate",
    )(e_t, g_t)


def _gelu_tanh(x):
    c = np.sqrt(2 / np.pi).astype(np.float32)
    return x * (0.5 * (1.0 + jnp.tanh(c * (x + 0.044715 * (x * x * x)))))


def _peer_dense_kernel(xb_ref, xf_ref, wd_ref, ut_ref, v_ref, g_ref, b_ref,
                       of_ref, ob_ref, w_ref, *, ne, tn):
    e = pl.program_id(1)

    @pl.when(e == 0)
    def _():
        of_ref[...] = jnp.zeros_like(of_ref)

    h = jnp.dot(xb_ref[...], ut_ref[...], preferred_element_type=F32)
    for r in range(GATE_ROWS):
        lanes = slice(r * N_KEYS, (r + 1) * N_KEYS)
        gate = wd_ref[pl.ds(r, tn, stride=GATE_ROWS), :]
        w_ref[:, lanes] = (gate * _gelu_tanh(h[:, lanes])).astype(BF16)
    of_ref[...] += jnp.dot(w_ref[...], v_ref[...], preferred_element_type=F32)

    @pl.when(e == ne - 1)
    def _():
        _write_ln(ALPHA * xf_ref[...] + of_ref[...], g_ref, b_ref, of_ref, ob_ref)


def _peer_dense(xb, xf, wd, u_t, v, layer, g, b, *, tn=528):
    m, d = xb.shape
    ne = wd.shape[0]
    te = GATE_ROWS * N_KEYS
    assert v.shape[1] == ne * te
    tn = _tile(m, tn)
    row = lambda i, e: (i, 0)
    return pl.pallas_call(
        functools.partial(_peer_dense_kernel, ne=ne, tn=tn),
        grid=(m // tn, ne),
        in_specs=[pl.BlockSpec((tn, d), row), pl.BlockSpec((tn, d), row),
                  pl.BlockSpec((None, tn * GATE_ROWS, N_KEYS), lambda i, e: (e, i, 0)),
                  pl.BlockSpec((None, d, te), lambda i, e: (layer, 0, e)),
                  pl.BlockSpec((None, te, d), lambda i, e: (layer, e, 0)),
                  _resident((1, d)), _resident((1, d))],
        out_specs=[pl.BlockSpec((tn, d), row), pl.BlockSpec((tn, d), row)],
        out_shape=[jax.ShapeDtypeStruct((m, d), F32), jax.ShapeDtypeStruct((m, d), BF16)],
        scratch_shapes=[pltpu.VMEM((tn, te), BF16)],
        compiler_params=_params("parallel", "arbitrary"),
        name="peer_dense",
    )(xb, xf, wd, u_t, v, g.reshape(1, d), b.reshape(1, d))


def _peer_layer(xf, xb, w_query, sub_keys, u_t_all, v_all, layer, ln_g, ln_b):
    m = xf.shape[0]
    npick = PEER_HEADS * PEER_TOPK
    (qb,) = _mm(xb, w_query.astype(BF16), [BF16])
    eidx, gates = _peer_topk(qb, sub_keys.astype(BF16))
    e_t = eidx.reshape(npick, m).T
    g_t = gates.reshape(npick, m).T
    wd = _peer_gate(e_t, g_t).reshape(N_KEYS // GATE_ROWS, m * GATE_ROWS, N_KEYS)
    return _peer_dense(xb, xf, wd, u_t_all, v_all, layer, ln_g, ln_b)


def _rope_tables(pos):
    half = MLA_ROPE // 2
    inv = ROPE_THETA ** (-jnp.arange(half, dtype=jnp.float32) / half)
    ang = pos.astype(jnp.float32)[:, None] * inv
    cos, sin = jnp.cos(ang), jnp.sin(ang)
    z = jnp.zeros_like(cos)
    return (jnp.concatenate([cos, cos, z, z], -1), jnp.concatenate([-sin, z, z, z], -1),
            jnp.concatenate([z, sin, z, z], -1))


def kernel(x_prompt, x_sample, cache_mla_ckv, cache_mla_krope, cache_diff_k, cache_diff_v,
           mla_w_dqkv, mla_g_q, mla_w_uq, mla_g_kv, mla_w_ukv, mla_w_o,
           diff_w_qkv, diff_lam_q1, diff_lam_k1, diff_lam_q2, diff_lam_k2, diff_g_sub, diff_w_o,
           peer_w_query, peer_sub_keys, peer_u, peer_v,
           ln_mix_g, ln_mix_b, ln_ffn_g, ln_ffn_b):
    bp, tp, d = x_prompt.shape
    bs, ts, _ = x_sample.shape
    past = cache_mla_ckv.shape[2]
    mp, ms = bp * tp, bs * ts
    tks = past + ts
    tkp = -(-tks // LANE) * LANE
    tq_p = _tile(tp, 512, LANE)

    def pad_keys(a):
        return jnp.pad(a, ((0, 0), (0, tkp - tks), (0, 0))).reshape(bs * tkp, a.shape[-1])

    u_t_all = peer_u.astype(BF16).swapaxes(1, 2)
    v_all = peer_v.astype(BF16)
    xf = jnp.concatenate([x_prompt.reshape(mp, d), x_sample.reshape(ms, d)], 0)
    pos = jnp.concatenate([jnp.tile(jnp.arange(tp), bp), jnp.tile(past + jnp.arange(ts), bs)])
    cos, sin_a, sin_b = _rope_tables(pos)

    j = 0
    w_dqkv = jnp.pad(mla_w_dqkv[j], ((0, 0), (0, LANE - MLA_ROPE))).astype(BF16)
    cq, ckv_f, ckv_b, kr_f, kr_b = _mla_proj(xf, w_dqkv, mla_g_q[j], mla_g_kv[j], cos, sin_a, sin_b)
    hq = MLA_NOPE + MLA_ROPE
    w_uq = jnp.pad(mla_w_uq[j].reshape(MLA_Q_LORA, MLA_HEADS, hq),
                   ((0, 0), (0, 0), (0, 2 * LANE - hq))).reshape(MLA_Q_LORA, MLA_HEADS * 2 * LANE)
    q = _q_proj(cq, w_uq.astype(BF16), cos, sin_a, sin_b)
    w_ukv = mla_w_ukv[j].reshape(MLA_KV_LORA, MLA_HEADS, MLA_NOPE + MLA_V)
    w_uk_t = w_ukv[:, :, :MLA_NOPE].reshape(MLA_KV_LORA, -1).T.astype(BF16)
    w_uv = w_ukv[:, :, MLA_NOPE:].reshape(MLA_KV_LORA, -1).astype(BF16)
    lora_tiles = dict(tm=1024, tn=2048)
    (knt_p,) = _mm(w_uk_t, ckv_b[:mp].T, [BF16], **lora_tiles)
    (v_p,) = _mm(ckv_b[:mp], w_uv, [BF16], **lora_tiles)
    o_p = _mla_attention(q, knt_p, kr_b[:mp].T, v_p, batch=bp, t_q=tp, t_k=tp,
                         t_valid=tp, q_row0=0, q_off=0, tq=tq_p, tk=tq_p)
    ckv_cat = pad_keys(jnp.concatenate([cache_mla_ckv[j].astype(BF16),
                                        ckv_b[mp:].reshape(bs, ts, MLA_KV_LORA)], 1))
    kr_cache = jnp.pad(cache_mla_krope[j], ((0, 0), (0, 0), (0, LANE - MLA_ROPE))).astype(BF16)
    kr_cat = pad_keys(jnp.concatenate([kr_cache, kr_b[mp:].reshape(bs, ts, LANE)], 1))
    (knt_s,) = _mm(w_uk_t, ckv_cat.T, [BF16], **lora_tiles)
    (v_s,) = _mm(ckv_cat, w_uv, [BF16], **lora_tiles)
    o_s = _mla_attention(q, knt_s, kr_cat.T, v_s, batch=bs, t_q=ts, t_k=tkp,
                         t_valid=tks, q_row0=mp, q_off=past, tq=ts, tk=tkp)
    o = jnp.concatenate([o_p, o_s], 0)
    xf, xb = _mm_ln(o, mla_w_o[j].astype(BF16), xf, ln_mix_g[0], ln_mix_b[0])
    xf, xb = _peer_layer(xf, xb, peer_w_query[0], peer_sub_keys[0], u_t_all, v_all, 0,
                         ln_ffn_g[0], ln_ffn_b[0])

    i = 1
    lam_init = 0.8 - 0.6 * math.exp(-0.3 * i)
    nqk = DIFF_HEADS * 2 * DIFF_QK
    w_qkv = diff_w_qkv[j].astype(BF16)
    (dq,) = _mm(xb, w_qkv[:, :nqk], [BF16])
    dk_f, dk_b = _mm(xb, w_qkv[:, nqk:2 * nqk], [F32, BF16], tm=1056)
    dv_f, dv_b = _mm(xb, w_qkv[:, 2 * nqk:], [F32, BF16], tm=1056)
    slopes = 2.0 ** (-8.0 * jnp.arange(1, DIFF_HEADS + 1, dtype=jnp.float32) / DIFF_HEADS)
    lam_vecs = jnp.stack([diff_lam_q1[j], diff_lam_k1[j], diff_lam_q2[j], diff_lam_k2[j]])
    diff_args = (slopes, lam_vecs, diff_g_sub[j])
    o_p = _diff_attention(dq, dk_b[:mp].T, dv_b, *diff_args, k_transposed=True, batch=bp,
                          t_q=tp, t_k=tp, q_row0=0, q_off=0, tq=tq_p, tk=tq_p, lam_init=lam_init)
    k_cat = jnp.concatenate([cache_diff_k[j].reshape(bs, past, nqk).astype(BF16),
                             dk_b[mp:].reshape(bs, ts, nqk)], 1).reshape(bs * tks, nqk)
    v_cat = jnp.concatenate([cache_diff_v[j].reshape(bs, past, -1).astype(BF16),
                             dv_b[mp:].reshape(bs, ts, -1)], 1).reshape(bs * tks, -1)
    o_s = _diff_attention(dq, k_cat, v_cat, *diff_args, k_transposed=False, batch=bs,
                          t_q=ts, t_k=tks, q_row0=mp, q_off=past, tq=ts, tk=tks, lam_init=lam_init)
    o = jnp.concatenate([o_p, o_s], 0)
    xf, xb = _mm_ln(o, diff_w_o[j].astype(BF16), xf, ln_mix_g[1], ln_mix_b[1])
    xf, xb = _peer_layer(xf, xb, peer_w_query[1], peer_sub_keys[1], u_t_all, v_all, 1,
                         ln_ffn_g[1], ln_ffn_b[1])

    kr_f = kr_f[:, :MLA_ROPE]
    return (xf[:mp].reshape(bp, tp, d), xf[mp:].reshape(bs, ts, d),
            ckv_f[:mp].reshape(1, bp, tp, -1), kr_f[:mp].reshape(1, bp, tp, -1),
            dk_f[:mp].reshape(1, bp, tp, DIFF_HEADS, -1), dv_f[:mp].reshape(1, bp, tp, DIFF_HEADS, -1),
            ckv_f[mp:].reshape(1, bs, ts, -1), kr_f[mp:].reshape(1, bs, ts, -1),
            dk_f[mp:].reshape(1, bs, ts, DIFF_HEADS, -1), dv_f[mp:].reshape(1, bs, ts, DIFF_HEADS, -1))
```

```python
import functools
import math

import jax
import jax.numpy as jnp
import numpy as np
from jax import lax
from jax.experimental import pallas as pl
from jax.experimental.pallas import tpu as pltpu

F32 = jnp.float32
BF16 = jnp.bfloat16

DEPTH = 2
CHUNK = 64
ALPHA = (2 * DEPTH) ** 0.25
LN_EPS = 1e-5
RMS_EPS = 1e-6
NEG = -1e30
ROPE_THETA = 10000.0
MLA_HEADS = 16
MLA_Q_LORA = 512
MLA_KV_LORA = 512
MLA_NOPE = 128
MLA_ROPE = 64
MLA_V = 128
DIFF_HEADS = 8
DIFF_QK = 128
DIFF_V = 256
PEER_HEADS = 8
PEER_TOPK = 16
N_KEYS = 128
PEER_DKEY = 256

LANE = 128
BF16_SUBLANE = 16
VMEM_LIMIT = 56 << 20


def _tile(n, target, mult=BF16_SUBLANE):
    for t in range(min(n, target), 0, -1):
        if n % t == 0 and t % mult == 0:
            return t
    raise ValueError(f"no tile for {n} (target {target}, multiple of {mult})")


def _params(*sem):
    return pltpu.CompilerParams(dimension_semantics=sem, vmem_limit_bytes=VMEM_LIMIT)


def _layer_norm(y, g, b):
    mu = jnp.mean(y, axis=-1, keepdims=True)
    d = y - mu
    var = jnp.mean(d * d, axis=-1, keepdims=True)
    return d * lax.rsqrt(var + LN_EPS) * g + b


def _rms_norm(y, g):
    return y * lax.rsqrt(jnp.mean(y * y, axis=-1, keepdims=True) + RMS_EPS) * g


def _rope128(r, cos, sin_a, sin_b):
    return r * cos + pltpu.roll(r, 96, 1) * sin_a + pltpu.roll(r, 32, 1) * sin_b


def _mm_kernel(x_ref, w_ref, *outs):
    y = jnp.dot(x_ref[...], w_ref[...], preferred_element_type=F32)
    for o in outs:
        o[...] = y.astype(o.dtype)


def _mm(x, w, out_dtypes, *, tm=528, tn=1024):
    m, kdim = x.shape
    n = w.shape[1]
    tm, tn = _tile(m, tm), _tile(n, tn, LANE)
    outs = pl.pallas_call(
        _mm_kernel,
        grid=(m // tm, n // tn),
        in_specs=[pl.BlockSpec((tm, kdim), lambda i, j: (i, 0)),
                  pl.BlockSpec((kdim, tn), lambda i, j: (0, j))],
        out_specs=[pl.BlockSpec((tm, tn), lambda i, j: (i, j)) for _ in out_dtypes],
        out_shape=[jax.ShapeDtypeStruct((m, n), dt) for dt in out_dtypes],
        compiler_params=_params("parallel", "parallel"),
        name="mm",
    )(x, w)
    return outs


def _resident(shape):
    return pl.BlockSpec(shape, lambda *_: (0,) * len(shape), pipeline_mode=pl.Buffered(1))


def _write_ln(y, g_ref, b_ref, of_ref, ob_ref):
    y = _layer_norm(y, g_ref[...], b_ref[...])
    of_ref[...] = y
    ob_ref[...] = y.astype(BF16)


def _mm_ln_kernel(a_ref, w_ref, r_ref, g_ref, b_ref, of_ref, ob_ref):
    mix = jnp.dot(a_ref[...], w_ref[...], preferred_element_type=F32)
    _write_ln(ALPHA * r_ref[...] + mix, g_ref, b_ref, of_ref, ob_ref)


def _mm_ln(a, w, resid, g, b, *, tm=528):
    m, kdim = a.shape
    d = w.shape[1]
    tm = _tile(m, tm)
    row = lambda i: (i, 0)
    return pl.pallas_call(
        _mm_ln_kernel,
        grid=(m // tm,),
        in_specs=[pl.BlockSpec((tm, kdim), row), _resident((kdim, d)),
                  pl.BlockSpec((tm, d), row), _resident((1, d)), _resident((1, d))],
        out_specs=[pl.BlockSpec((tm, d), row), pl.BlockSpec((tm, d), row)],
        out_shape=[jax.ShapeDtypeStruct((m, d), F32), jax.ShapeDtypeStruct((m, d), BF16)],
        compiler_params=_params("parallel"),
        name="mm_ln",
    )(a, w, resid, g.reshape(1, d), b.reshape(1, d))


def _mla_proj_kernel(x_ref, w_ref, gq_ref, gkv_ref, cos_ref, sa_ref, sb_ref,
                     cq_ref, ckvf_ref, ckvb_ref, krf_ref, krb_ref, *, nq, nkv):
    lat = jnp.dot(x_ref[...].astype(BF16), w_ref[...], preferred_element_type=F32)
    cq_ref[...] = _rms_norm(lat[:, :nq], gq_ref[...]).astype(BF16)
    ckv = _rms_norm(lat[:, nq:nq + nkv], gkv_ref[...])
    ckvf_ref[...] = ckv
    ckvb_ref[...] = ckv.astype(BF16)
    kr = _rope128(lat[:, nq + nkv:], cos_ref[...], sa_ref[...], sb_ref[...])
    krf_ref[...] = kr
    krb_ref[...] = kr.astype(BF16)


def _mla_proj(x, w_pad, g_q, g_kv, cos, sin_a, sin_b, *, tm=528):
    m, kdim = x.shape
    nq, nkv = g_q.shape[0], g_kv.shape[0]
    n = w_pad.shape[1]
    assert n == nq + nkv + LANE
    tm = _tile(m, tm)
    row = lambda i: (i, 0)
    return pl.pallas_call(
        functools.partial(_mla_proj_kernel, nq=nq, nkv=nkv),
        grid=(m // tm,),
        in_specs=[pl.BlockSpec((tm, kdim), row), _resident((kdim, n)),
                  _resident((1, nq)), _resident((1, nkv)),
                  pl.BlockSpec((tm, LANE), row), pl.BlockSpec((tm, LANE), row),
                  pl.BlockSpec((tm, LANE), row)],
        out_specs=[pl.BlockSpec((tm, nq), row), pl.BlockSpec((tm, nkv), row),
                   pl.BlockSpec((tm, nkv), row), pl.BlockSpec((tm, LANE), row),
                   pl.BlockSpec((tm, LANE), row)],
        out_shape=[jax.ShapeDtypeStruct((m, nq), BF16), jax.ShapeDtypeStruct((m, nkv), F32),
                   jax.ShapeDtypeStruct((m, nkv), BF16), jax.ShapeDtypeStruct((m, LANE), F32),
                   jax.ShapeDtypeStruct((m, LANE), BF16)],
        compiler_params=_params("parallel"),
        name="mla_proj",
    )(x, w_pad, g_q.reshape(1, nq), g_kv.reshape(1, nkv), cos, sin_a, sin_b)


def _q_proj_kernel(c_ref, w_ref, cos_ref, sa_ref, sb_ref, o_ref, *, heads):
    y = jnp.dot(c_ref[...], w_ref[...], preferred_element_type=F32)
    for h in range(heads):
        lo = h * 2 * LANE
        o_ref[:, lo:lo + LANE] = y[:, lo:lo + LANE].astype(BF16)
        r = _rope128(y[:, lo + LANE:lo + 2 * LANE], cos_ref[...], sa_ref[...], sb_ref[...])
        o_ref[:, lo + LANE:lo + 2 * LANE] = r.astype(BF16)


def _q_proj(cq, w_arr, cos, sin_a, sin_b, *, tm=528, heads_per_step=8):
    m, kdim = cq.shape
    n = w_arr.shape[1]
    tm = _tile(m, tm)
    tn = heads_per_step * 2 * LANE
    row = lambda i, j: (i, 0)
    return pl.pallas_call(
        functools.partial(_q_proj_kernel, heads=heads_per_step),
        grid=(m // tm, n // tn),
        in_specs=[pl.BlockSpec((tm, kdim), row),
                  pl.BlockSpec((kdim, tn), lambda i, j: (0, j)),
                  pl.BlockSpec((tm, LANE), row), pl.BlockSpec((tm, LANE), row),
                  pl.BlockSpec((tm, LANE), row)],
        out_specs=pl.BlockSpec((tm, tn), lambda i, j: (i, j)),
        out_shape=jax.ShapeDtypeStruct((m, n), BF16),
        compiler_params=_params("parallel", "parallel"),
        name="q_proj",
    )(cq, w_arr, cos, sin_a, sin_b)


Q_CHAIN_ROWS = 256


def _row_parts(tq):
    n = max(1, tq // Q_CHAIN_ROWS)
    assert tq % n == 0
    return [slice(p * (tq // n), (p + 1) * (tq // n)) for p in range(n)]


def _visible_blocks(qi, *, q_off, tq, tk, nk):
    q_first = q_off + qi * tq
    q_last = q_first + tq - 1
    n_before = jnp.minimum(q_first // tk, nk)
    n_vis = jnp.minimum(((q_last // CHUNK + 1) * CHUNK + tk - 1) // tk, nk)
    return n_before, n_vis


def _positions(qi, ki, rows, *, q_off, tq, tk):
    shape = (rows.stop - rows.start, tk)
    qp = q_off + qi * tq + rows.start + lax.broadcasted_iota(jnp.int32, shape, 0)
    kp = ki * tk + lax.broadcasted_iota(jnp.int32, shape, 1)
    return qp, kp


def _key_mask(qi, ki, rows, geo):
    qp, kp = _positions(qi, ki, rows, q_off=geo["q_off"], tq=geo["tq"], tk=geo["tk"])
    k_chunk = kp // CHUNK
    if geo["t_valid"] < geo["nk"] * geo["tk"]:
        k_chunk = jnp.where(kp < geo["t_valid"], k_chunk, jnp.iinfo(jnp.int32).max)
    return k_chunk <= (qp // CHUNK), qp, kp


def _diagonal_only(geo):
    return (geo["tq"] == geo["tk"] and geo["q_off"] % geo["tk"] == 0
            and geo["t_valid"] == geo["nk"] * geo["tk"])


def _diagonal_visible(tq, tk):
    row = lax.broadcasted_iota(jnp.int32, (tq, tk), 0)
    col = lax.broadcasted_iota(jnp.int32, (tq, tk), 1)
    return (col // CHUNK) <= (row // CHUNK), row, col


def _for_blocks(lo, hi, step):
    lax.fori_loop(lo, hi, lambda ki, carry: (step(ki), carry)[1], 0)


def _online_softmax_step(s, v, m_ref, l_ref, acc_ref, rows):
    m_prev = m_ref[rows, :]
    m_new = jnp.maximum(m_prev, jnp.max(s, axis=-1, keepdims=True))
    alpha = jnp.exp(m_prev - m_new)
    p = jnp.exp(s - m_new)
    l_ref[rows, :] = alpha * l_ref[rows, :] + jnp.sum(p, axis=-1, keepdims=True)
    acc_ref[rows, :] = (alpha * acc_ref[rows, :]
                        + jnp.dot(p.astype(BF16), v, preferred_element_type=F32))
    m_ref[rows, :] = m_new


def _mla_attn_kernel(q_ref, knt_ref, krt_ref, v_ref, o_ref, kcat_t, diag_ref, s_buf,
                     m_ref, l_ref, acc_ref, *, geo, scale):
    qi = pl.program_id(2)
    tk = geo["tk"]
    parts = _row_parts(geo["tq"])
    diagonal_only = _diagonal_only(geo)

    @pl.when(qi == 0)
    def _():
        kcat_t[:MLA_NOPE, :] = knt_ref[...]
        kcat_t[MLA_NOPE:, :] = krt_ref[...]
        if diagonal_only:
            diag_ref[...] = jnp.where(_diagonal_visible(geo["tq"], tk)[0], 0.0, NEG)

    m_ref[...] = jnp.full_like(m_ref, -jnp.inf)
    l_ref[...] = jnp.zeros_like(l_ref)
    acc_ref[...] = jnp.zeros_like(acc_ref)
    n_before, n_vis = _visible_blocks(qi, q_off=geo["q_off"], tq=geo["tq"], tk=tk, nk=geo["nk"])

    def scores(ki, slot):
        r0 = pl.multiple_of(ki * tk, tk)
        k_t = kcat_t[:, pl.ds(r0, tk)]
        for rows in parts:
            s_buf[slot, rows, :] = jnp.dot(q_ref[rows, :], k_t, preferred_element_type=F32) * scale

    def step(ki, masked):
        slot = lax.rem(ki, 2)
        r0 = pl.multiple_of(ki * tk, tk)
        v = v_ref[pl.ds(r0, tk), :]
        for rows in parts:
            s = s_buf[slot, rows, :]
            if masked and diagonal_only:
                s = jnp.where(diag_ref[rows, :] < 0.0, NEG, s)
            elif masked:
                s = jnp.where(_key_mask(qi, ki, rows, geo)[0], s, NEG)
            _online_softmax_step(s, v, m_ref, l_ref, acc_ref, rows)
        if geo["nk"] > 1:
            scores(jnp.minimum(ki + 1, n_vis - 1), 1 - slot)

    scores(0, 0)
    _for_blocks(0, n_before, functools.partial(step, masked=False))
    _for_blocks(n_before, n_vis, functools.partial(step, masked=True))
    o_ref[...] = (acc_ref[...] / l_ref[...]).astype(BF16)


def _mla_attention(q2d, knt, krt, v2d, *, batch, t_q, t_k, t_valid, q_row0, q_off, tq, tk):
    nq, nk = t_q // tq, t_k // tk
    assert q_row0 % tq == 0 and t_k % tk == 0 and tk % LANE == 0
    geo = dict(q_off=q_off, tq=tq, tk=tk, nk=nk, t_valid=t_valid)
    qb0 = q_row0 // tq
    return pl.pallas_call(
        functools.partial(_mla_attn_kernel, geo=geo, scale=(MLA_NOPE + MLA_ROPE) ** -0.5),
        grid=(batch, MLA_HEADS, nq),
        in_specs=[pl.BlockSpec((tq, 2 * LANE), lambda b, h, qi: (qb0 + b * nq + qi, h)),
                  pl.BlockSpec((MLA_NOPE, t_k), lambda b, h, qi: (h, b)),
                  pl.BlockSpec((LANE, t_k), lambda b, h, qi: (0, b)),
                  pl.BlockSpec((t_k, MLA_V), lambda b, h, qi: (b, h))],
        out_specs=pl.BlockSpec((tq, MLA_V), lambda b, h, qi: (b * nq + qi, h)),
        out_shape=jax.ShapeDtypeStruct((batch * t_q, MLA_HEADS * MLA_V), BF16),
        scratch_shapes=[pltpu.VMEM((2 * LANE, t_k), BF16),
                        pltpu.VMEM((tq, tk) if _diagonal_only(geo) else (8, LANE), F32),
                        pltpu.VMEM((2, tq, tk), F32),
                        pltpu.VMEM((tq, 1), F32), pltpu.VMEM((tq, 1), F32),
                        pltpu.VMEM((tq, MLA_V), F32)],
        compiler_params=_params("parallel", "parallel", "arbitrary"),
        name="mla_attn",
    )(q2d, knt, krt, v2d)


def _diff_attn_kernel(slope_ref, lam_ref, gsub_ref, q_ref, k_ref, v_ref, o_ref,
                      b0_ref, diag_ref, s_buf, m1, l1, a1, m2, l2, a2,
                      *, geo, scale, lam_init, k_transposed):
    h, qi = pl.program_id(1), pl.program_id(2)
    tq, tk = geo["tq"], geo["tk"]
    parts = _row_parts(tq)
    diagonal_only = _diagonal_only(geo)
    stats = ((m1, l1, a1), (m2, l2, a2))
    for m, l, a in stats:
        m[...] = jnp.full_like(m, -jnp.inf)
        l[...] = jnp.zeros_like(l)
        a[...] = jnp.zeros_like(a)
    slope = slope_ref[h]
    q_first = geo["q_off"] + qi * tq

    @pl.when(qi == 0)
    def _():
        visible, row, col = _diagonal_visible(tq, tk)
        b0_ref[...] = -slope * (row - col).astype(F32)
        if diagonal_only:
            diag_ref[...] = jnp.where(visible, -slope * jnp.abs(row - col).astype(F32), NEG)

    n_before, n_vis = _visible_blocks(qi, q_off=geo["q_off"], tq=tq, tk=tk, nk=geo["nk"])

    def scores(ki, slot):
        r0 = pl.multiple_of(ki * tk, tk)
        for rows in parts:
            for half in range(2):
                lo = half * DIFF_QK
                q_half = q_ref[rows, lo:lo + DIFF_QK]
                if k_transposed:
                    s = jnp.dot(q_half, k_ref[lo:lo + DIFF_QK, pl.ds(r0, tk)],
                                preferred_element_type=F32)
                else:
                    s = lax.dot_general(q_half, k_ref[pl.ds(r0, tk), lo:lo + DIFF_QK],
                                        (((1,), (1,)), ((), ())), preferred_element_type=F32)
                s_buf[slot, half, rows, :] = s

    def step(ki, masked):
        slot = lax.rem(ki, 2)
        r0 = pl.multiple_of(ki * tk, tk)
        v = v_ref[pl.ds(r0, tk), :]
        for rows in parts:
            if masked and diagonal_only:
                bias = diag_ref[rows, :]
            elif masked:
                mask, qp, kp = _key_mask(qi, ki, rows, geo)
                bias = jnp.where(mask, -slope * jnp.abs(qp - kp).astype(F32), NEG)
            else:
                shift = -slope * jnp.full((1, tk), q_first - r0, jnp.int32).astype(F32)
                bias = b0_ref[rows, :] + shift
            for half, (m, l, a) in enumerate(stats):
                _online_softmax_step(s_buf[slot, half, rows, :] * scale + bias, v, m, l, a, rows)
        if geo["nk"] > 1:
            scores(jnp.minimum(ki + 1, n_vis - 1), 1 - slot)

    scores(0, 0)
    _for_blocks(0, n_before, functools.partial(step, masked=False))
    _for_blocks(n_before, n_vis, functools.partial(step, masked=True))

    lam_v = lam_ref[...]
    lam = (jnp.exp(jnp.sum(lam_v[0:1] * lam_v[1:2], axis=-1, keepdims=True))
           - jnp.exp(jnp.sum(lam_v[2:3] * lam_v[3:4], axis=-1, keepdims=True)) + lam_init)
    o = a1[...] / l1[...] - lam * (a2[...] / l2[...])
    o_ref[...] = (_rms_norm(o, gsub_ref[...]) * (1.0 - lam_init)).astype(BF16)


def _diff_attention(q2d, k, v2d, slopes, lam_vecs, g_sub, *, k_transposed,
                    batch, t_q, t_k, q_row0, q_off, tq, tk, lam_init):
    nq, nk = t_q // tq, t_k // tk
    assert q_row0 % tq == 0 and t_k % tk == 0 and (tk % LANE == 0 or not k_transposed)
    geo = dict(q_off=q_off, tq=tq, tk=tk, nk=nk, t_valid=t_k)
    qb0 = q_row0 // tq
    hd = 2 * DIFF_QK
    k_spec = (pl.BlockSpec((hd, t_k), lambda b, h, qi: (h, b)) if k_transposed
              else pl.BlockSpec((t_k, hd), lambda b, h, qi: (b, h)))
    return pl.pallas_call(
        functools.partial(_diff_attn_kernel, geo=geo, scale=DIFF_QK ** -0.5, lam_init=lam_init,
                          k_transposed=k_transposed),
        grid=(batch, DIFF_HEADS, nq),
        in_specs=[pl.BlockSpec(memory_space=pltpu.SMEM),
                  pl.BlockSpec((4, DIFF_QK), lambda b, h, qi: (0, 0)),
                  pl.BlockSpec((1, DIFF_V), lambda b, h, qi: (0, 0)),
                  pl.BlockSpec((tq, hd), lambda b, h, qi: (qb0 + b * nq + qi, h)),
                  k_spec,
                  pl.BlockSpec((t_k, DIFF_V), lambda b, h, qi: (b, h))],
        out_specs=pl.BlockSpec((tq, DIFF_V), lambda b, h, qi: (b * nq + qi, h)),
        out_shape=jax.ShapeDtypeStruct((batch * t_q, DIFF_HEADS * DIFF_V), BF16),
        scratch_shapes=[pltpu.VMEM((tq, tk), F32),
                        pltpu.VMEM((tq, tk) if _diagonal_only(geo) else (8, LANE), F32),
                        pltpu.VMEM((2, 2, tq, tk), F32),
                        pltpu.VMEM((tq, 1), F32), pltpu.VMEM((tq, 1), F32),
                        pltpu.VMEM((tq, DIFF_V), F32),
                        pltpu.VMEM((tq, 1), F32), pltpu.VMEM((tq, 1), F32),
                        pltpu.VMEM((tq, DIFF_V), F32)],
        compiler_params=_params("parallel", "parallel", "arbitrary"),
        name="diff_attn",
    )(slopes, lam_vecs, g_sub.reshape(1, DIFF_V), q2d, k, v2d)


def _top_rows(s, n_rows, k, payload=None):
    pos = lax.broadcasted_iota(jnp.int32, s.shape, 0).astype(F32)
    vals, idxs, pays = [], [], []
    for _ in range(k):
        m = jnp.max(s, axis=0, keepdims=True)
        sel = jnp.min(jnp.where(s == m, pos, float(n_rows)), axis=0, keepdims=True)
        hit = pos == sel
        vals.append(m)
        idxs.append(sel)
        if payload is not None:
            pays.append(jnp.max(jnp.where(hit, payload, -1.0), axis=0, keepdims=True))
        s = jnp.where(hit, -jnp.inf, s)
    cat = lambda xs: jnp.concatenate(xs, axis=0)
    return cat(vals), cat(idxs), (cat(pays) if payload is not None else None)


F32_SUBLANE = 8


def _batcher_pairs(n):
    pairs = []

    def merge(lo, cnt, r):
        step = r * 2
        if step < cnt:
            merge(lo, cnt, step)
            merge(lo + r, cnt, step)
            pairs.extend((i, i + r) for i in range(lo + r, lo + cnt - r, step))
        else:
            pairs.append((lo, lo + r))

    def sort(lo, cnt):
        if cnt > 1:
            sort(lo, cnt // 2)
            sort(lo + cnt // 2, cnt // 2)
            merge(lo, cnt, 1)

    sort(0, n)
    return pairs


def _top_rows_by_merge(s, k):
    n_rows, cols = s.shape
    depth = n_rows // F32_SUBLANE
    assert depth * F32_SUBLANE == n_rows and depth >= k
    sub = lax.broadcasted_iota(jnp.int32, (F32_SUBLANE, cols), 0).astype(F32)
    vals = [s[r * F32_SUBLANE:(r + 1) * F32_SUBLANE] for r in range(depth)]
    idxs = [sub + float(r * F32_SUBLANE) for r in range(depth)]
    for a, b in _batcher_pairs(depth):
        swap = vals[b] > vals[a]
        vals[a], vals[b] = jnp.maximum(vals[a], vals[b]), jnp.minimum(vals[a], vals[b])
        idxs[a], idxs[b] = jnp.where(swap, idxs[b], idxs[a]), jnp.where(swap, idxs[a], idxs[b])
    tie = jnp.zeros_like(sub)
    for r in range(depth - 1):
        tie = jnp.where(vals[r] == vals[r + 1], 1.0, tie)
    out_v, out_i = [], []
    for j in range(k):
        m = jnp.max(vals[0], axis=0, keepdims=True)
        sel = jnp.min(jnp.where(vals[0] == m, idxs[0], float(n_rows)), axis=0, keepdims=True)
        hit = idxs[0] == sel
        out_v.append(m)
        out_i.append(sel)
        for r in range(depth - 1 - j):
            vals[r] = jnp.where(hit, vals[r + 1], vals[r])
            idxs[r] = jnp.where(hit, idxs[r + 1], idxs[r])
    return jnp.concatenate(out_v, axis=0), jnp.concatenate(out_i, axis=0), jnp.max(tie)


def _peer_topk_kernel(q_ref, sk_ref, e_ref, g_ref):
    half = PEER_DKEY // 2
    q = q_ref[...]
    nt = (((1,), (1,)), ((), ()))
    scores = [lax.dot_general(sk_ref[c], q[:, c * half:(c + 1) * half], nt,
                              preferred_element_type=F32) for c in range(2)]
    fast = [_top_rows_by_merge(s, PEER_TOPK) for s in scores]
    _peer_pick(q.shape[0], [f[0] for f in fast], [f[1] for f in fast], e_ref, g_ref)

    @pl.when(jnp.maximum(fast[0][2], fast[1][2]) > 0.0)
    def _():
        slow = [_top_rows(s, N_KEYS, PEER_TOPK) for s in scores]
        _peer_pick(q.shape[0], [f[0] for f in slow], [f[1] for f in slow], e_ref, g_ref)


def _peer_pick(tokens, sv, si, e_ref, g_ref):
    k = PEER_TOPK
    rows = F32_SUBLANE
    assert k == 2 * rows
    a_pos = lax.broadcasted_iota(jnp.int32, (k, tokens), 0).astype(F32)
    a_low = lax.broadcasted_iota(jnp.int32, (rows, tokens), 0).astype(F32)
    a_high = a_low + float(rows)
    base_e = si[0] * float(N_KEYS)

    def depth_row(b):
        n, ids = (k, a_pos) if b == 0 else (rows, a_low)
        live = ids < float(k // (b + 1))
        return (jnp.where(live, sv[0][:n] + sv[1][b:b + 1], -jnp.inf),
                jnp.where(live, base_e[:n] + si[1][b:b + 1], -1.0))

    vals, pays = map(list, zip(*[depth_row(b) for b in range(k)]))
    out_v, out_e = [], []
    for j in range(k):
        m = jnp.max(vals[0], axis=0, keepdims=True)
        sel = jnp.min(jnp.where(vals[0] == m, a_pos, float(k)), axis=0, keepdims=True)
        hit = a_pos == sel
        out_v.append(m)
        out_e.append(jnp.max(jnp.where(hit, pays[0], -1.0), axis=0, keepdims=True))
        if j == k - 1:
            break
        low = a_low == sel
        vals[0] = jnp.concatenate([jnp.where(low, vals[1], vals[0][:rows]),
                                   jnp.where(a_high == sel, -jnp.inf, vals[0][rows:])], axis=0)
        pays[0] = jnp.concatenate([jnp.where(low, pays[1], pays[0][:rows]), pays[0][rows:]], axis=0)
        for r in range(1, k - 1 - j):
            vals[r] = jnp.where(low, vals[r + 1], vals[r])
            pays[r] = jnp.where(low, pays[r + 1], pays[r])
    fv, fe = jnp.concatenate(out_v, axis=0), jnp.concatenate(out_e, axis=0)
    p = jnp.exp(fv - fv[0:1])
    g_ref[...] = p / jnp.sum(p, axis=0, keepdims=True)
    e_ref[...] = fe.astype(jnp.int32)


def _peer_topk(qb, sub_keys_b, *, tn=256):
    m = qb.shape[0]
    tn = _tile(m, tn, LANE)
    half = PEER_DKEY // 2
    out_spec = pl.BlockSpec((None, PEER_TOPK, tn), lambda i, h: (h, 0, i))
    return pl.pallas_call(
        _peer_topk_kernel,
        grid=(m // tn, PEER_HEADS),
        in_specs=[pl.BlockSpec((tn, PEER_DKEY), lambda i, h: (i, h)),
                  pl.BlockSpec((None, 2, N_KEYS, half), lambda i, h: (h, 0, 0, 0))],
        out_specs=[out_spec, out_spec],
        out_shape=[jax.ShapeDtypeStruct((PEER_HEADS, PEER_TOPK, m), jnp.int32),
                   jax.ShapeDtypeStruct((PEER_HEADS, PEER_TOPK, m), F32)],
        compiler_params=_params("parallel", "parallel"),
        name="peer_topk",
    )(qb, sub_keys_b)


GATE_ROWS = 8
GATE_UNROLL = 64


def _peer_gate_kernel(e_ref, g_ref, o_ref, *, rows):
    npick = PEER_HEADS * PEER_TOPK
    key = lax.broadcasted_iota(jnp.int32, (N_KEYS, npick), 0).astype(F32).astype(BF16)
    one, zero = jnp.ones((), BF16), jnp.zeros((), BF16)
    nt = (((1,), (1,)), ((), ()))
    shift = N_KEYS.bit_length() - 1

    def body(blk, carry):
        for u in range(GATE_UNROLL):
            n = blk * GATE_UNROLL + u
            e = e_ref[pl.ds(n, 1), :]
            g = g_ref[pl.ds(n, 1), :].astype(BF16)
            e_first = (e >> shift).astype(F32).astype(BF16)
            e_second = (e & (N_KEYS - 1)).astype(F32).astype(BF16)
            first = jnp.where(key == e_first, one, zero)
            second = jnp.where(key == e_second, g, zero)
            w = lax.dot_general(first, second, nt, preferred_element_type=F32)
            o_ref[:, n] = w.reshape(N_KEYS // GATE_ROWS, GATE_ROWS, N_KEYS)
        return carry

    lax.fori_loop(0, rows // GATE_UNROLL, body, 0)


def _peer_gate(e_t, g_t, *, tb=64):
    m, npick = e_t.shape
    assert N_KEYS & (N_KEYS - 1) == 0
    tb = _tile(m, tb, GATE_UNROLL)
    nib = N_KEYS // GATE_ROWS
    return pl.pallas_call(
        functools.partial(_peer_gate_kernel, rows=tb),
        grid=(m // tb,),
        in_specs=[pl.BlockSpec((tb, npick), lambda i: (i, 0)),
                  pl.BlockSpec((tb, npick), lambda i: (i, 0))],
        out_specs=pl.BlockSpec((nib, tb, GATE_ROWS, N_KEYS), lambda i: (0, i, 0, 0)),
        out_shape=jax.ShapeDtypeStruct((nib, m, GATE_ROWS, N_KEYS), F32),
        compiler_params=_params("parallel"),
        name="peer_gate",
    )(e_t, g_t)


def _gelu_tanh(x):
    c = np.sqrt(2 / np.pi).astype(np.float32)
    return x * (0.5 * (1.0 + jnp.tanh(c * (x + 0.044715 * (x * x * x)))))


def _peer_dense_kernel(xb_ref, xf_ref, wd_ref, u_ref, v_ref, g_ref, b_ref,
                       of_ref, ob_ref, w_ref, *, ne, tn):
    e = pl.program_id(1)

    @pl.when(e == 0)
    def _():
        of_ref[...] = jnp.zeros_like(of_ref)

    h = lax.dot_general(xb_ref[...], u_ref[...], (((1,), (1,)), ((), ())),
                        preferred_element_type=F32)
    for r in range(GATE_ROWS):
        lanes = slice(r * N_KEYS, (r + 1) * N_KEYS)
        gate = wd_ref[pl.ds(r, tn, stride=GATE_ROWS), :]
        w_ref[:, lanes] = (gate * _gelu_tanh(h[:, lanes])).astype(BF16)
    of_ref[...] += jnp.dot(w_ref[...], v_ref[...], preferred_element_type=F32)

    @pl.when(e == ne - 1)
    def _():
        _write_ln(ALPHA * xf_ref[...] + of_ref[...], g_ref, b_ref, of_ref, ob_ref)


def _peer_dense(xb, xf, wd, u_t, v, layer, g, b, *, tn=528):
    m, d = xb.shape
    ne = wd.shape[0]
    te = GATE_ROWS * N_KEYS
    assert v.shape[1] == ne * te
    tn = _tile(m, tn)
    row = lambda i, e: (i, 0)
    return pl.pallas_call(
        functools.partial(_peer_dense_kernel, ne=ne, tn=tn),
        grid=(m // tn, ne),
        in_specs=[pl.BlockSpec((tn, d), row), pl.BlockSpec((tn, d), row),
                  pl.BlockSpec((None, tn * GATE_ROWS, N_KEYS), lambda i, e: (e, i, 0)),
                  pl.BlockSpec((None, te, d), lambda i, e: (layer, e, 0)),
                  pl.BlockSpec((None, te, d), lambda i, e: (layer, e, 0)),
                  _resident((1, d)), _resident((1, d))],
        out_specs=[pl.BlockSpec((tn, d), row), pl.BlockSpec((tn, d), row)],
        out_shape=[jax.ShapeDtypeStruct((m, d), F32), jax.ShapeDtypeStruct((m, d), BF16)],
        scratch_shapes=[pltpu.VMEM((tn, te), BF16)],
        compiler_params=_params("parallel", "arbitrary"),
        name="peer_dense",
    )(xb, xf, wd, u_t, v, g.reshape(1, d), b.reshape(1, d))


def _peer_layer(xf, xb, w_query, sub_keys, u_t_all, v_all, layer, ln_g, ln_b):
    m = xf.shape[0]
    npick = PEER_HEADS * PEER_TOPK
    (qb,) = _mm(xb, w_query.astype(BF16), [BF16])
    eidx, gates = _peer_topk(qb, sub_keys.astype(BF16))
    e_t = eidx.reshape(npick, m).T
    g_t = gates.reshape(npick, m).T
    wd = _peer_gate(e_t, g_t).reshape(N_KEYS // GATE_ROWS, m * GATE_ROWS, N_KEYS)
    return _peer_dense(xb, xf, wd, u_t_all, v_all, layer, ln_g, ln_b)


def _rope_tables(pos):
    half = MLA_ROPE // 2
    inv = ROPE_THETA ** (-jnp.arange(half, dtype=jnp.float32) / half)
    ang = pos.astype(jnp.float32)[:, None] * inv
    cos, sin = jnp.cos(ang), jnp.sin(ang)
    z = jnp.zeros_like(cos)
    return (jnp.concatenate([cos, cos, z, z], -1), jnp.concatenate([-sin, z, z, z], -1),
            jnp.concatenate([z, sin, z, z], -1))


def kernel(x_prompt, x_sample, cache_mla_ckv, cache_mla_krope, cache_diff_k, cache_diff_v,
           mla_w_dqkv, mla_g_q, mla_w_uq, mla_g_kv, mla_w_ukv, mla_w_o,
           diff_w_qkv, diff_lam_q1, diff_lam_k1, diff_lam_q2, diff_lam_k2, diff_g_sub, diff_w_o,
           peer_w_query, peer_sub_keys, peer_u, peer_v,
           ln_mix_g, ln_mix_b, ln_ffn_g, ln_ffn_b):
    bp, tp, d = x_prompt.shape
    bs, ts, _ = x_sample.shape
    past = cache_mla_ckv.shape[2]
    mp, ms = bp * tp, bs * ts
    tks = past + ts
    tkp = -(-tks // LANE) * LANE
    tq_p = _tile(tp, 512, LANE)

    def pad_keys(a):
        return jnp.pad(a, ((0, 0), (0, tkp - tks), (0, 0))).reshape(bs * tkp, a.shape[-1])

    u_t_all = peer_u.astype(BF16)
    v_all = peer_v.astype(BF16)
    xf = jnp.concatenate([x_prompt.reshape(mp, d), x_sample.reshape(ms, d)], 0)
    pos = jnp.concatenate([jnp.tile(jnp.arange(tp), bp), jnp.tile(past + jnp.arange(ts), bs)])
    cos, sin_a, sin_b = _rope_tables(pos)

    j = 0
    w_dqkv = jnp.pad(mla_w_dqkv[j], ((0, 0), (0, LANE - MLA_ROPE))).astype(BF16)
    cq, ckv_f, ckv_b, kr_f, kr_b = _mla_proj(xf, w_dqkv, mla_g_q[j], mla_g_kv[j], cos, sin_a, sin_b)
    hq = MLA_NOPE + MLA_ROPE
    w_uq = jnp.pad(mla_w_uq[j].reshape(MLA_Q_LORA, MLA_HEADS, hq),
                   ((0, 0), (0, 0), (0, 2 * LANE - hq))).reshape(MLA_Q_LORA, MLA_HEADS * 2 * LANE)
    q = _q_proj(cq, w_uq.astype(BF16), cos, sin_a, sin_b)
    w_ukv = mla_w_ukv[j].reshape(MLA_KV_LORA, MLA_HEADS, MLA_NOPE + MLA_V)
    w_uk_t = w_ukv[:, :, :MLA_NOPE].reshape(MLA_KV_LORA, -1).T.astype(BF16)
    w_uv = w_ukv[:, :, MLA_NOPE:].reshape(MLA_KV_LORA, -1).astype(BF16)
    lora_tiles = dict(tm=1024, tn=2048)
    (knt_p,) = _mm(w_uk_t, ckv_b[:mp].T, [BF16], **lora_tiles)
    (v_p,) = _mm(ckv_b[:mp], w_uv, [BF16], **lora_tiles)
    o_p = _mla_attention(q, knt_p, kr_b[:mp].T, v_p, batch=bp, t_q=tp, t_k=tp,
                         t_valid=tp, q_row0=0, q_off=0, tq=tq_p, tk=tq_p)
    ckv_cat = pad_keys(jnp.concatenate([cache_mla_ckv[j].astype(BF16),
                                        ckv_b[mp:].reshape(bs, ts, MLA_KV_LORA)], 1))
    kr_cache = jnp.pad(cache_mla_krope[j], ((0, 0), (0, 0), (0, LANE - MLA_ROPE))).astype(BF16)
    kr_cat = pad_keys(jnp.concatenate([kr_cache, kr_b[mp:].reshape(bs, ts, LANE)], 1))
    (knt_s,) = _mm(w_uk_t, ckv_cat.T, [BF16], **lora_tiles)
    (v_s,) = _mm(ckv_cat, w_uv, [BF16], **lora_tiles)
    o_s = _mla_attention(q, knt_s, kr_cat.T, v_s, batch=bs, t_q=ts, t_k=tkp,
                         t_valid=tks, q_row0=mp, q_off=past, tq=ts, tk=tkp)
    o = jnp.concatenate([o_p, o_s], 0)
    xf, xb = _mm_ln(o, mla_w_o[j].astype(BF16), xf, ln_mix_g[0], ln_mix_b[0])
    xf, xb = _peer_layer(xf, xb, peer_w_query[0], peer_sub_keys[0], u_t_all, v_all, 0,
                         ln_ffn_g[0], ln_ffn_b[0])

    i = 1
    lam_init = 0.8 - 0.6 * math.exp(-0.3 * i)
    nqk = DIFF_HEADS * 2 * DIFF_QK
    w_qkv = diff_w_qkv[j].astype(BF16)
    (dq,) = _mm(xb, w_qkv[:, :nqk], [BF16])
    dk_f, dk_b = _mm(xb, w_qkv[:, nqk:2 * nqk], [F32, BF16], tm=1056)
    dv_f, dv_b = _mm(xb, w_qkv[:, 2 * nqk:], [F32, BF16], tm=1056)
    slopes = 2.0 ** (-8.0 * jnp.arange(1, DIFF_HEADS + 1, dtype=jnp.float32) / DIFF_HEADS)
    lam_vecs = jnp.stack([diff_lam_q1[j], diff_lam_k1[j], diff_lam_q2[j], diff_lam_k2[j]])
    diff_args = (slopes, lam_vecs, diff_g_sub[j])
    o_p = _diff_attention(dq, dk_b[:mp].T, dv_b, *diff_args, k_transposed=True, batch=bp,
                          t_q=tp, t_k=tp, q_row0=0, q_off=0, tq=tq_p, tk=tq_p, lam_init=lam_init)
    k_cat = jnp.concatenate([cache_diff_k[j].reshape(bs, past, nqk).astype(BF16),
                             dk_b[mp:].reshape(bs, ts, nqk)], 1).reshape(bs * tks, nqk)
    v_cat = jnp.concatenate([cache_diff_v[j].reshape(bs, past, -1).astype(BF16),
                             dv_b[mp:].reshape(bs, ts, -1)], 1).reshape(bs * tks, -1)
    o_s = _diff_attention(dq, k_cat, v_cat, *diff_args, k_transposed=False, batch=bs,
                          t_q=ts, t_k=tks, q_row0=mp, q_off=past, tq=ts, tk=tks, lam_init=lam_init)
    o = jnp.concatenate([o_p, o_s], 0)
    xf, xb = _mm_ln(o, diff_w_o[j].astype(BF16), xf, ln_mix_g[1], ln_mix_b[1])
    xf, xb = _peer_layer(xf, xb, peer_w_query[1], peer_sub_keys[1], u_t_all, v_all, 1,
                         ln_ffn_g[1], ln_ffn_b[1])

    kr_f = kr_f[:, :MLA_ROPE]
    return (xf[:mp].reshape(bp, tp, d), xf[mp:].reshape(bs, ts, d),
            ckv_f[:mp].reshape(1, bp, tp, -1), kr_f[:mp].reshape(1, bp, tp, -1),
            dk_f[:mp].reshape(1, bp, tp, DIFF_HEADS, -1), dv_f[:mp].reshape(1, bp, tp, DIFF_HEADS, -1),
            ckv_f[mp:].reshape(1, bs, ts, -1), kr_f[mp:].reshape(1, bs, ts, -1),
            dk_f[mp:].reshape(1, bs, ts, DIFF_HEADS, -1), dv_f[mp:].reshape(1, bs, ts, DIFF_HEADS, -1))
```

```python
import functools
import math

import jax
import jax.numpy as jnp
import numpy as np
from jax import lax
from jax.experimental import pallas as pl
from jax.experimental.pallas import tpu as pltpu

F32 = jnp.float32
BF16 = jnp.bfloat16

DEPTH = 2
CHUNK = 64
ALPHA = (2 * DEPTH) ** 0.25
LN_EPS = 1e-5
RMS_EPS = 1e-6
NEG = -1e30
ROPE_THETA = 10000.0
MLA_HEADS = 16
MLA_Q_LORA = 512
MLA_KV_LORA = 512
MLA_NOPE = 128
MLA_ROPE = 64
MLA_V = 128
DIFF_HEADS = 8
DIFF_QK = 128
DIFF_V = 256
PEER_HEADS = 8
PEER_TOPK = 16
N_KEYS = 128
PEER_DKEY = 256

LANE = 128
BF16_SUBLANE = 16
VMEM_LIMIT = 56 << 20


def _tile(n, target, mult=BF16_SUBLANE):
    for t in range(min(n, target), 0, -1):
        if n % t == 0 and t % mult == 0:
            return t
    raise ValueError(f"no tile for {n} (target {target}, multiple of {mult})")


def _params(*sem):
    return pltpu.CompilerParams(dimension_semantics=sem, vmem_limit_bytes=VMEM_LIMIT)


def _layer_norm(y, g, b):
    mu = jnp.mean(y, axis=-1, keepdims=True)
    d = y - mu
    var = jnp.mean(d * d, axis=-1, keepdims=True)
    return d * lax.rsqrt(var + LN_EPS) * g + b


def _rms_norm(y, g):
    return y * lax.rsqrt(jnp.mean(y * y, axis=-1, keepdims=True) + RMS_EPS) * g


def _rope128(r, cos, sin_a, sin_b):
    return r * cos + pltpu.roll(r, 96, 1) * sin_a + pltpu.roll(r, 32, 1) * sin_b


def _mm_kernel(x_ref, w_ref, *outs):
    y = jnp.dot(x_ref[...], w_ref[...], preferred_element_type=F32)
    for o in outs:
        o[...] = y.astype(o.dtype)


def _mm(x, w, out_dtypes, *, tm=528, tn=1024):
    m, kdim = x.shape
    n = w.shape[1]
    tm, tn = _tile(m, tm), _tile(n, tn, LANE)
    outs = pl.pallas_call(
        _mm_kernel,
        grid=(m // tm, n // tn),
        in_specs=[pl.BlockSpec((tm, kdim), lambda i, j: (i, 0)),
                  pl.BlockSpec((kdim, tn), lambda i, j: (0, j))],
        out_specs=[pl.BlockSpec((tm, tn), lambda i, j: (i, j)) for _ in out_dtypes],
        out_shape=[jax.ShapeDtypeStruct((m, n), dt) for dt in out_dtypes],
        compiler_params=_params("parallel", "parallel"),
        name="mm",
    )(x, w)
    return outs


def _resident(shape):
    return pl.BlockSpec(shape, lambda *_: (0,) * len(shape), pipeline_mode=pl.Buffered(1))


def _write_ln(y, g_ref, b_ref, of_ref, ob_ref):
    y = _layer_norm(y, g_ref[...], b_ref[...])
    of_ref[...] = y
    ob_ref[...] = y.astype(BF16)


def _mm_ln_kernel(a_ref, w_ref, r_ref, g_ref, b_ref, of_ref, ob_ref):
    mix = jnp.dot(a_ref[...], w_ref[...], preferred_element_type=F32)
    _write_ln(ALPHA * r_ref[...] + mix, g_ref, b_ref, of_ref, ob_ref)


def _mm_ln(a, w, resid, g, b, *, tm=528):
    m, kdim = a.shape
    d = w.shape[1]
    tm = _tile(m, tm)
    row = lambda i: (i, 0)
    return pl.pallas_call(
        _mm_ln_kernel,
        grid=(m // tm,),
        in_specs=[pl.BlockSpec((tm, kdim), row), _resident((kdim, d)),
                  pl.BlockSpec((tm, d), row), _resident((1, d)), _resident((1, d))],
        out_specs=[pl.BlockSpec((tm, d), row), pl.BlockSpec((tm, d), row)],
        out_shape=[jax.ShapeDtypeStruct((m, d), F32), jax.ShapeDtypeStruct((m, d), BF16)],
        compiler_params=_params("parallel"),
        name="mm_ln",
    )(a, w, resid, g.reshape(1, d), b.reshape(1, d))


def _mla_proj_kernel(x_ref, w_ref, gq_ref, gkv_ref, cos_ref, sa_ref, sb_ref,
                     cq_ref, ckvf_ref, ckvb_ref, krf_ref, krb_ref, *, nq, nkv):
    lat = jnp.dot(x_ref[...].astype(BF16), w_ref[...], preferred_element_type=F32)
    cq_ref[...] = _rms_norm(lat[:, :nq], gq_ref[...]).astype(BF16)
    ckv = _rms_norm(lat[:, nq:nq + nkv], gkv_ref[...])
    ckvf_ref[...] = ckv
    ckvb_ref[...] = ckv.astype(BF16)
    kr = _rope128(lat[:, nq + nkv:], cos_ref[...], sa_ref[...], sb_ref[...])
    krf_ref[...] = kr
    krb_ref[...] = kr.astype(BF16)


def _mla_proj(x, w_pad, g_q, g_kv, cos, sin_a, sin_b, *, tm=528):
    m, kdim = x.shape
    nq, nkv = g_q.shape[0], g_kv.shape[0]
    n = w_pad.shape[1]
    assert n == nq + nkv + LANE
    tm = _tile(m, tm)
    row = lambda i: (i, 0)
    return pl.pallas_call(
        functools.partial(_mla_proj_kernel, nq=nq, nkv=nkv),
        grid=(m // tm,),
        in_specs=[pl.BlockSpec((tm, kdim), row), _resident((kdim, n)),
                  _resident((1, nq)), _resident((1, nkv)),
                  pl.BlockSpec((tm, LANE), row), pl.BlockSpec((tm, LANE), row),
                  pl.BlockSpec((tm, LANE), row)],
        out_specs=[pl.BlockSpec((tm, nq), row), pl.BlockSpec((tm, nkv), row),
                   pl.BlockSpec((tm, nkv), row), pl.BlockSpec((tm, LANE), row),
                   pl.BlockSpec((tm, LANE), row)],
        out_shape=[jax.ShapeDtypeStruct((m, nq), BF16), jax.ShapeDtypeStruct((m, nkv), F32),
                   jax.ShapeDtypeStruct((m, nkv), BF16), jax.ShapeDtypeStruct((m, LANE), F32),
                   jax.ShapeDtypeStruct((m, LANE), BF16)],
        compiler_params=_params("parallel"),
        name="mla_proj",
    )(x, w_pad, g_q.reshape(1, nq), g_kv.reshape(1, nkv), cos, sin_a, sin_b)


def _q_proj_kernel(c_ref, w_ref, cos_ref, sa_ref, sb_ref, o_ref, *, heads):
    y = jnp.dot(c_ref[...], w_ref[...], preferred_element_type=F32)
    for h in range(heads):
        lo = h * 2 * LANE
        o_ref[:, lo:lo + LANE] = y[:, lo:lo + LANE].astype(BF16)
        r = _rope128(y[:, lo + LANE:lo + 2 * LANE], cos_ref[...], sa_ref[...], sb_ref[...])
        o_ref[:, lo + LANE:lo + 2 * LANE] = r.astype(BF16)


def _q_proj(cq, w_arr, cos, sin_a, sin_b, *, tm=528, heads_per_step=8):
    m, kdim = cq.shape
    n = w_arr.shape[1]
    tm = _tile(m, tm)
    tn = heads_per_step * 2 * LANE
    row = lambda i, j: (i, 0)
    return pl.pallas_call(
        functools.partial(_q_proj_kernel, heads=heads_per_step),
        grid=(m // tm, n // tn),
        in_specs=[pl.BlockSpec((tm, kdim), row),
                  pl.BlockSpec((kdim, tn), lambda i, j: (0, j)),
                  pl.BlockSpec((tm, LANE), row), pl.BlockSpec((tm, LANE), row),
                  pl.BlockSpec((tm, LANE), row)],
        out_specs=pl.BlockSpec((tm, tn), lambda i, j: (i, j)),
        out_shape=jax.ShapeDtypeStruct((m, n), BF16),
        compiler_params=_params("parallel", "parallel"),
        name="q_proj",
    )(cq, w_arr, cos, sin_a, sin_b)


Q_CHAIN_ROWS = 256


def _row_parts(tq):
    n = max(1, tq // Q_CHAIN_ROWS)
    assert tq % n == 0
    return [slice(p * (tq // n), (p + 1) * (tq // n)) for p in range(n)]


def _visible_blocks(qi, *, q_off, tq, tk, nk):
    q_first = q_off + qi * tq
    q_last = q_first + tq - 1
    n_before = jnp.minimum(q_first // tk, nk)
    n_vis = jnp.minimum(((q_last // CHUNK + 1) * CHUNK + tk - 1) // tk, nk)
    return n_before, n_vis


def _positions(qi, ki, rows, *, q_off, tq, tk):
    shape = (rows.stop - rows.start, tk)
    qp = q_off + qi * tq + rows.start + lax.broadcasted_iota(jnp.int32, shape, 0)
    kp = ki * tk + lax.broadcasted_iota(jnp.int32, shape, 1)
    return qp, kp


def _key_mask(qi, ki, rows, geo):
    qp, kp = _positions(qi, ki, rows, q_off=geo["q_off"], tq=geo["tq"], tk=geo["tk"])
    k_chunk = kp // CHUNK
    if geo["t_valid"] < geo["nk"] * geo["tk"]:
        k_chunk = jnp.where(kp < geo["t_valid"], k_chunk, jnp.iinfo(jnp.int32).max)
    return k_chunk <= (qp // CHUNK), qp, kp


def _diagonal_only(geo):
    return (geo["tq"] == geo["tk"] and geo["q_off"] % geo["tk"] == 0
            and geo["t_valid"] == geo["nk"] * geo["tk"])


def _diagonal_visible(tq, tk):
    row = lax.broadcasted_iota(jnp.int32, (tq, tk), 0)
    col = lax.broadcasted_iota(jnp.int32, (tq, tk), 1)
    return (col // CHUNK) <= (row // CHUNK), row, col


def _for_blocks(lo, hi, step):
    lax.fori_loop(lo, hi, lambda ki, carry: (step(ki), carry)[1], 0)


def _online_softmax_step(s, v, m_ref, l_ref, acc_ref, rows):
    m_prev = m_ref[rows, :]
    m_new = jnp.maximum(m_prev, jnp.max(s, axis=-1, keepdims=True))
    alpha = jnp.exp(m_prev - m_new)
    p = jnp.exp(s - m_new)
    l_ref[rows, :] = alpha * l_ref[rows, :] + jnp.sum(p, axis=-1, keepdims=True)
    acc_ref[rows, :] = (alpha * acc_ref[rows, :]
                        + jnp.dot(p.astype(BF16), v, preferred_element_type=F32))
    m_ref[rows, :] = m_new


def _mla_attn_kernel(q_ref, knt_ref, krt_ref, v_ref, o_ref, kcat_t, diag_ref, s_buf,
                     m_ref, l_ref, acc_ref, *, geo, scale):
    qi = pl.program_id(2)
    tk = geo["tk"]
    parts = _row_parts(geo["tq"])
    diagonal_only = _diagonal_only(geo)

    @pl.when(qi == 0)
    def _():
        kcat_t[:MLA_NOPE, :] = knt_ref[...]
        kcat_t[MLA_NOPE:, :] = krt_ref[...]
        if diagonal_only:
            diag_ref[...] = jnp.where(_diagonal_visible(geo["tq"], tk)[0], 0.0, NEG)

    m_ref[...] = jnp.full_like(m_ref, -jnp.inf)
    l_ref[...] = jnp.zeros_like(l_ref)
    acc_ref[...] = jnp.zeros_like(acc_ref)
    n_before, n_vis = _visible_blocks(qi, q_off=geo["q_off"], tq=geo["tq"], tk=tk, nk=geo["nk"])

    def scores(ki, slot):
        r0 = pl.multiple_of(ki * tk, tk)
        k_t = kcat_t[:, pl.ds(r0, tk)]
        for rows in parts:
            s_buf[slot, rows, :] = jnp.dot(q_ref[rows, :], k_t, preferred_element_type=F32) * scale

    def step(ki, masked):
        slot = lax.rem(ki, 2)
        r0 = pl.multiple_of(ki * tk, tk)
        v = v_ref[pl.ds(r0, tk), :]
        for rows in parts:
            s = s_buf[slot, rows, :]
            if masked and diagonal_only:
                s = jnp.where(diag_ref[rows, :] < 0.0, NEG, s)
            elif masked:
                s = jnp.where(_key_mask(qi, ki, rows, geo)[0], s, NEG)
            _online_softmax_step(s, v, m_ref, l_ref, acc_ref, rows)
        if geo["nk"] > 1:
            scores(jnp.minimum(ki + 1, n_vis - 1), 1 - slot)

    scores(0, 0)
    _for_blocks(0, n_before, functools.partial(step, masked=False))
    _for_blocks(n_before, n_vis, functools.partial(step, masked=True))
    o_ref[...] = (acc_ref[...] / l_ref[...]).astype(BF16)


def _mla_attention(q2d, knt, krt, v2d, *, batch, t_q, t_k, t_valid, q_row0, q_off, tq, tk):
    nq, nk = t_q // tq, t_k // tk
    assert q_row0 % tq == 0 and t_k % tk == 0 and tk % LANE == 0
    geo = dict(q_off=q_off, tq=tq, tk=tk, nk=nk, t_valid=t_valid)
    qb0 = q_row0 // tq
    return pl.pallas_call(
        functools.partial(_mla_attn_kernel, geo=geo, scale=(MLA_NOPE + MLA_ROPE) ** -0.5),
        grid=(batch, MLA_HEADS, nq),
        in_specs=[pl.BlockSpec((tq, 2 * LANE), lambda b, h, qi: (qb0 + b * nq + qi, h)),
                  pl.BlockSpec((MLA_NOPE, t_k), lambda b, h, qi: (h, b)),
                  pl.BlockSpec((LANE, t_k), lambda b, h, qi: (0, b)),
                  pl.BlockSpec((t_k, MLA_V), lambda b, h, qi: (b, h))],
        out_specs=pl.BlockSpec((tq, MLA_V), lambda b, h, qi: (b * nq + qi, h)),
        out_shape=jax.ShapeDtypeStruct((batch * t_q, MLA_HEADS * MLA_V), BF16),
        scratch_shapes=[pltpu.VMEM((2 * LANE, t_k), BF16),
                        pltpu.VMEM((tq, tk) if _diagonal_only(geo) else (8, LANE), F32),
                        pltpu.VMEM((2, tq, tk), F32),
                        pltpu.VMEM((tq, 1), F32), pltpu.VMEM((tq, 1), F32),
                        pltpu.VMEM((tq, MLA_V), F32)],
        compiler_params=_params("parallel", "parallel", "arbitrary"),
        name="mla_attn",
    )(q2d, knt, krt, v2d)


def _diff_attn_kernel(slope_ref, lam_ref, gsub_ref, q_ref, k_ref, v_ref, o_ref,
                      b0_ref, diag_ref, s_buf, m1, l1, a1, m2, l2, a2,
                      *, geo, scale, lam_init, k_transposed):
    h, qi = pl.program_id(1), pl.program_id(2)
    tq, tk = geo["tq"], geo["tk"]
    parts = _row_parts(tq)
    diagonal_only = _diagonal_only(geo)
    stats = ((m1, l1, a1), (m2, l2, a2))
    for m, l, a in stats:
        m[...] = jnp.full_like(m, -jnp.inf)
        l[...] = jnp.zeros_like(l)
        a[...] = jnp.zeros_like(a)
    slope = slope_ref[h]
    q_first = geo["q_off"] + qi * tq

    @pl.when(qi == 0)
    def _():
        visible, row, col = _diagonal_visible(tq, tk)
        b0_ref[...] = -slope * (row - col).astype(F32)
        if diagonal_only:
            diag_ref[...] = jnp.where(visible, -slope * jnp.abs(row - col).astype(F32), NEG)

    n_before, n_vis = _visible_blocks(qi, q_off=geo["q_off"], tq=tq, tk=tk, nk=geo["nk"])

    def scores(ki, slot):
        r0 = pl.multiple_of(ki * tk, tk)
        for rows in parts:
            for half in range(2):
                lo = half * DIFF_QK
                q_half = q_ref[rows, lo:lo + DIFF_QK]
                if k_transposed:
                    s = jnp.dot(q_half, k_ref[lo:lo + DIFF_QK, pl.ds(r0, tk)],
                                preferred_element_type=F32)
                else:
                    s = lax.dot_general(q_half, k_ref[pl.ds(r0, tk), lo:lo + DIFF_QK],
                                        (((1,), (1,)), ((), ())), preferred_element_type=F32)
                s_buf[slot, half, rows, :] = s

    def step(ki, masked):
        slot = lax.rem(ki, 2)
        r0 = pl.multiple_of(ki * tk, tk)
        v = v_ref[pl.ds(r0, tk), :]
        for rows in parts:
            if masked and diagonal_only:
                bias = diag_ref[rows, :]
            elif masked:
                mask, qp, kp = _key_mask(qi, ki, rows, geo)
                bias = jnp.where(mask, -slope * jnp.abs(qp - kp).astype(F32), NEG)
            else:
                shift = -slope * jnp.full((1, tk), q_first - r0, jnp.int32).astype(F32)
                bias = b0_ref[rows, :] + shift
            for half, (m, l, a) in enumerate(stats):
                _online_softmax_step(s_buf[slot, half, rows, :] * scale + bias, v, m, l, a, rows)
        if geo["nk"] > 1:
            scores(jnp.minimum(ki + 1, n_vis - 1), 1 - slot)

    scores(0, 0)
    _for_blocks(0, n_before, functools.partial(step, masked=False))
    _for_blocks(n_before, n_vis, functools.partial(step, masked=True))

    lam_v = lam_ref[...]
    lam = (jnp.exp(jnp.sum(lam_v[0:1] * lam_v[1:2], axis=-1, keepdims=True))
           - jnp.exp(jnp.sum(lam_v[2:3] * lam_v[3:4], axis=-1, keepdims=True)) + lam_init)
    o = a1[...] / l1[...] - lam * (a2[...] / l2[...])
    o_ref[...] = (_rms_norm(o, gsub_ref[...]) * (1.0 - lam_init)).astype(BF16)


def _diff_attention(q2d, k, v2d, slopes, lam_vecs, g_sub, *, k_transposed,
                    batch, t_q, t_k, q_row0, q_off, tq, tk, lam_init):
    nq, nk = t_q // tq, t_k // tk
    assert q_row0 % tq == 0 and t_k % tk == 0 and (tk % LANE == 0 or not k_transposed)
    geo = dict(q_off=q_off, tq=tq, tk=tk, nk=nk, t_valid=t_k)
    qb0 = q_row0 // tq
    hd = 2 * DIFF_QK
    k_spec = (pl.BlockSpec((hd, t_k), lambda b, h, qi: (h, b)) if k_transposed
              else pl.BlockSpec((t_k, hd), lambda b, h, qi: (b, h)))
    return pl.pallas_call(
        functools.partial(_diff_attn_kernel, geo=geo, scale=DIFF_QK ** -0.5, lam_init=lam_init,
                          k_transposed=k_transposed),
        grid=(batch, DIFF_HEADS, nq),
        in_specs=[pl.BlockSpec(memory_space=pltpu.SMEM),
                  pl.BlockSpec((4, DIFF_QK), lambda b, h, qi: (0, 0)),
                  pl.BlockSpec((1, DIFF_V), lambda b, h, qi: (0, 0)),
                  pl.BlockSpec((tq, hd), lambda b, h, qi: (qb0 + b * nq + qi, h)),
                  k_spec,
                  pl.BlockSpec((t_k, DIFF_V), lambda b, h, qi: (b, h))],
        out_specs=pl.BlockSpec((tq, DIFF_V), lambda b, h, qi: (b * nq + qi, h)),
        out_shape=jax.ShapeDtypeStruct((batch * t_q, DIFF_HEADS * DIFF_V), BF16),
        scratch_shapes=[pltpu.VMEM((tq, tk), F32),
                        pltpu.VMEM((tq, tk) if _diagonal_only(geo) else (8, LANE), F32),
                        pltpu.VMEM((2, 2, tq, tk), F32),
                        pltpu.VMEM((tq, 1), F32), pltpu.VMEM((tq, 1), F32),
                        pltpu.VMEM((tq, DIFF_V), F32),
                        pltpu.VMEM((tq, 1), F32), pltpu.VMEM((tq, 1), F32),
                        pltpu.VMEM((tq, DIFF_V), F32)],
        compiler_params=_params("parallel", "parallel", "arbitrary"),
        name="diff_attn",
    )(slopes, lam_vecs, g_sub.reshape(1, DIFF_V), q2d, k, v2d)


def _top_rows(s, n_rows, k, payload=None):
    pos = lax.broadcasted_iota(jnp.int32, s.shape, 0).astype(F32)
    vals, idxs, pays = [], [], []
    for _ in range(k):
        m = jnp.max(s, axis=0, keepdims=True)
        sel = jnp.min(jnp.where(s == m, pos, float(n_rows)), axis=0, keepdims=True)
        hit = pos == sel
        vals.append(m)
        idxs.append(sel)
        if payload is not None:
            pays.append(jnp.max(jnp.where(hit, payload, -1.0), axis=0, keepdims=True))
        s = jnp.where(hit, -jnp.inf, s)
    cat = lambda xs: jnp.concatenate(xs, axis=0)
    return cat(vals), cat(idxs), (cat(pays) if payload is not None else None)


F32_SUBLANE = 8


def _batcher_pairs(n):
    pairs = []

    def merge(lo, cnt, r):
        step = r * 2
        if step < cnt:
            merge(lo, cnt, step)
            merge(lo + r, cnt, step)
            pairs.extend((i, i + r) for i in range(lo + r, lo + cnt - r, step))
        else:
            pairs.append((lo, lo + r))

    def sort(lo, cnt):
        if cnt > 1:
            sort(lo, cnt // 2)
            sort(lo + cnt // 2, cnt // 2)
            merge(lo, cnt, 1)

    sort(0, n)
    return pairs


def _top_rows_by_merge(s, k):
    n_rows, cols = s.shape
    depth = n_rows // F32_SUBLANE
    assert depth * F32_SUBLANE == n_rows and depth >= k
    sub = lax.broadcasted_iota(jnp.int32, (F32_SUBLANE, cols), 0).astype(F32)
    vals = [s[r * F32_SUBLANE:(r + 1) * F32_SUBLANE] for r in range(depth)]
    idxs = [sub + float(r * F32_SUBLANE) for r in range(depth)]
    for a, b in _batcher_pairs(depth):
        swap = vals[b] > vals[a]
        vals[a], vals[b] = jnp.maximum(vals[a], vals[b]), jnp.minimum(vals[a], vals[b])
        idxs[a], idxs[b] = jnp.where(swap, idxs[b], idxs[a]), jnp.where(swap, idxs[a], idxs[b])
    tie = jnp.zeros_like(sub)
    for r in range(depth - 1):
        tie = jnp.where(vals[r] == vals[r + 1], 1.0, tie)
    out_v, out_i = [], []
    for j in range(k):
        m = jnp.max(vals[0], axis=0, keepdims=True)
        sel = jnp.min(jnp.where(vals[0] == m, idxs[0], float(n_rows)), axis=0, keepdims=True)
        hit = idxs[0] == sel
        out_v.append(m)
        out_i.append(sel)
        for r in range(depth - 1 - j):
            vals[r] = jnp.where(hit, vals[r + 1], vals[r])
            idxs[r] = jnp.where(hit, idxs[r + 1], idxs[r])
    return jnp.concatenate(out_v, axis=0), jnp.concatenate(out_i, axis=0), jnp.max(tie)


TOPK_HEADS = 2


def _peer_topk_kernel(q_ref, sk_ref, e_ref, g_ref):
    half = PEER_DKEY // 2
    tokens = q_ref.shape[0]
    nt = (((1,), (1,)), ((), ()))
    pending = []
    for hh in range(TOPK_HEADS):
        scores = [lax.dot_general(sk_ref[hh, c],
                                  q_ref[:, (2 * hh + c) * half:(2 * hh + c + 1) * half], nt,
                                  preferred_element_type=F32) for c in range(2)]
        fast = [_top_rows_by_merge(s, PEER_TOPK) for s in scores]
        _peer_pick(tokens, [f[0] for f in fast], [f[1] for f in fast], e_ref.at[hh], g_ref.at[hh])
        pending.append((scores, jnp.maximum(fast[0][2], fast[1][2])))

    for hh, (scores, tie) in enumerate(pending):
        def exact(hh=hh, scores=scores):
            slow = [_top_rows(s, N_KEYS, PEER_TOPK) for s in scores]
            _peer_pick(tokens, [f[0] for f in slow], [f[1] for f in slow],
                       e_ref.at[hh], g_ref.at[hh])

        pl.when(tie > 0.0)(exact)


def _peer_pick(tokens, sv, si, e_ref, g_ref):
    k = PEER_TOPK
    rows = F32_SUBLANE
    assert k == 2 * rows
    a_pos = lax.broadcasted_iota(jnp.int32, (k, tokens), 0).astype(F32)
    a_low = lax.broadcasted_iota(jnp.int32, (rows, tokens), 0).astype(F32)
    a_high = a_low + float(rows)
    base_e = si[0] * float(N_KEYS)

    def depth_row(b):
        n, ids = (k, a_pos) if b == 0 else (rows, a_low)
        live = ids < float(k // (b + 1))
        return (jnp.where(live, sv[0][:n] + sv[1][b:b + 1], -jnp.inf),
                jnp.where(live, base_e[:n] + si[1][b:b + 1], -1.0))

    vals, pays = map(list, zip(*[depth_row(b) for b in range(k)]))
    out_v, out_e = [], []
    for j in range(k):
        m = jnp.max(vals[0], axis=0, keepdims=True)
        sel = jnp.min(jnp.where(vals[0] == m, a_pos, float(k)), axis=0, keepdims=True)
        hit = a_pos == sel
        out_v.append(m)
        out_e.append(jnp.max(jnp.where(hit, pays[0], -1.0), axis=0, keepdims=True))
        if j == k - 1:
            break
        low = a_low == sel
        vals[0] = jnp.concatenate([jnp.where(low, vals[1], vals[0][:rows]),
                                   jnp.where(a_high == sel, -jnp.inf, vals[0][rows:])], axis=0)
        pays[0] = jnp.concatenate([jnp.where(low, pays[1], pays[0][:rows]), pays[0][rows:]], axis=0)
        for r in range(1, k - 1 - j):
            vals[r] = jnp.where(low, vals[r + 1], vals[r])
            pays[r] = jnp.where(low, pays[r + 1], pays[r])
    fv, fe = jnp.concatenate(out_v, axis=0), jnp.concatenate(out_e, axis=0)
    p = jnp.exp(fv - fv[0:1])
    g_ref[...] = p / jnp.sum(p, axis=0, keepdims=True)
    e_ref[...] = fe.astype(jnp.int32)


def _peer_topk(qb, sub_keys_b, *, tn=256):
    m = qb.shape[0]
    tn = _tile(m, tn, LANE)
    half = PEER_DKEY // 2
    out_spec = pl.BlockSpec((TOPK_HEADS, PEER_TOPK, tn), lambda i, h: (h, 0, i))
    return pl.pallas_call(
        _peer_topk_kernel,
        grid=(m // tn, PEER_HEADS // TOPK_HEADS),
        in_specs=[pl.BlockSpec((tn, TOPK_HEADS * PEER_DKEY), lambda i, h: (i, h)),
                  pl.BlockSpec((TOPK_HEADS, 2, N_KEYS, half), lambda i, h: (h, 0, 0, 0))],
        out_specs=[out_spec, out_spec],
        out_shape=[jax.ShapeDtypeStruct((PEER_HEADS, PEER_TOPK, m), jnp.int32),
                   jax.ShapeDtypeStruct((PEER_HEADS, PEER_TOPK, m), F32)],
        compiler_params=_params("parallel", "parallel"),
        name="peer_topk",
    )(qb, sub_keys_b)


GATE_ROWS = 8
GATE_UNROLL = 64


def _peer_gate_kernel(e_ref, g_ref, o_ref, *, rows):
    npick = PEER_HEADS * PEER_TOPK
    key = lax.broadcasted_iota(jnp.int32, (N_KEYS, npick), 0).astype(F32).astype(BF16)
    one, zero = jnp.ones((), BF16), jnp.zeros((), BF16)
    nt = (((1,), (1,)), ((), ()))
    shift = N_KEYS.bit_length() - 1

    def body(blk, carry):
        for u in range(GATE_UNROLL):
            n = blk * GATE_UNROLL + u
            e = e_ref[pl.ds(n, 1), :]
            g = g_ref[pl.ds(n, 1), :].astype(BF16)
            e_first = (e >> shift).astype(F32).astype(BF16)
            e_second = (e & (N_KEYS - 1)).astype(F32).astype(BF16)
            first = jnp.where(key == e_first, one, zero)
            second = jnp.where(key == e_second, g, zero)
            w = lax.dot_general(first, second, nt, preferred_element_type=F32)
            o_ref[:, n] = w.reshape(N_KEYS // GATE_ROWS, GATE_ROWS, N_KEYS)
        return carry

    lax.fori_loop(0, rows // GATE_UNROLL, body, 0)


def _peer_gate(e_t, g_t, *, tb=64):
    m, npick = e_t.shape
    assert N_KEYS & (N_KEYS - 1) == 0
    tb = _tile(m, tb, GATE_UNROLL)
    nib = N_KEYS // GATE_ROWS
    return pl.pallas_call(
        functools.partial(_peer_gate_kernel, rows=tb),
        grid=(m // tb,),
        in_specs=[pl.BlockSpec((tb, npick), lambda i: (i, 0)),
                  pl.BlockSpec((tb, npick), lambda i: (i, 0))],
        out_specs=pl.BlockSpec((nib, tb, GATE_ROWS, N_KEYS), lambda i: (0, i, 0, 0)),
        out_shape=jax.ShapeDtypeStruct((nib, m, GATE_ROWS, N_KEYS), F32),
        compiler_params=_params("parallel"),
        name="peer_gate",
    )(e_t, g_t)


def _gelu_tanh(x):
    c = np.sqrt(2 / np.pi).astype(np.float32)
    return x * (0.5 * (1.0 + jnp.tanh(c * (x + 0.044715 * (x * x * x)))))


def _peer_dense_kernel(xb_ref, xf_ref, wd_ref, ut_ref, v_ref, g_ref, b_ref,
                       of_ref, ob_ref, w_ref, *, ne, tn):
    e = pl.program_id(1)

    @pl.when(e == 0)
    def _():
        of_ref[...] = jnp.zeros_like(of_ref)

    h = jnp.dot(xb_ref[...], ut_ref[...], preferred_element_type=F32)
    for r in range(GATE_ROWS):
        lanes = slice(r * N_KEYS, (r + 1) * N_KEYS)
        gate = wd_ref[pl.ds(r, tn, stride=GATE_ROWS), :]
        w_ref[:, lanes] = (gate * _gelu_tanh(h[:, lanes])).astype(BF16)
    of_ref[...] += jnp.dot(w_ref[...], v_ref[...], preferred_element_type=F32)

    @pl.when(e == ne - 1)
    def _():
        _write_ln(ALPHA * xf_ref[...] + of_ref[...], g_ref, b_ref, of_ref, ob_ref)


def _peer_dense(xb, xf, wd, u_t, v, layer, g, b, *, tn=528):
    m, d = xb.shape
    ne = wd.shape[0]
    te = GATE_ROWS * N_KEYS
    assert v.shape[1] == ne * te
    tn = _tile(m, tn)
    row = lambda i, e: (i, 0)
    return pl.pallas_call(
        functools.partial(_peer_dense_kernel, ne=ne, tn=tn),
        grid=(m // tn, ne),
        in_specs=[pl.BlockSpec((tn, d), row), pl.BlockSpec((tn, d), row),
                  pl.BlockSpec((None, tn * GATE_ROWS, N_KEYS), lambda i, e: (e, i, 0)),
                  pl.BlockSpec((None, d, te), lambda i, e: (layer, 0, e)),
                  pl.BlockSpec((None, te, d), lambda i, e: (layer, e, 0)),
                  _resident((1, d)), _resident((1, d))],
        out_specs=[pl.BlockSpec((tn, d), row), pl.BlockSpec((tn, d), row)],
        out_shape=[jax.ShapeDtypeStruct((m, d), F32), jax.ShapeDtypeStruct((m, d), BF16)],
        scratch_shapes=[pltpu.VMEM((tn, te), BF16)],
        compiler_params=_params("parallel", "arbitrary"),
        name="peer_dense",
    )(xb, xf, wd, u_t, v, g.reshape(1, d), b.reshape(1, d))


def _peer_layer(xf, xb, w_query, sub_keys, u_t_all, v_all, layer, ln_g, ln_b):
    m = xf.shape[0]
    npick = PEER_HEADS * PEER_TOPK
    (qb,) = _mm(xb, w_query.astype(BF16), [BF16])
    eidx, gates = _peer_topk(qb, sub_keys.astype(BF16))
    e_t = eidx.reshape(npick, m).T
    g_t = gates.reshape(npick, m).T
    wd = _peer_gate(e_t, g_t).reshape(N_KEYS // GATE_ROWS, m * GATE_ROWS, N_KEYS)
    return _peer_dense(xb, xf, wd, u_t_all, v_all, layer, ln_g, ln_b)


def _rope_tables(pos):
    half = MLA_ROPE // 2
    inv = ROPE_THETA ** (-jnp.arange(half, dtype=jnp.float32) / half)
    ang = pos.astype(jnp.float32)[:, None] * inv
    cos, sin = jnp.cos(ang), jnp.sin(ang)
    z = jnp.zeros_like(cos)
    return (jnp.concatenate([cos, cos, z, z], -1), jnp.concatenate([-sin, z, z, z], -1),
            jnp.concatenate([z, sin, z, z], -1))


def kernel(x_prompt, x_sample, cache_mla_ckv, cache_mla_krope, cache_diff_k, cache_diff_v,
           mla_w_dqkv, mla_g_q, mla_w_uq, mla_g_kv, mla_w_ukv, mla_w_o,
           diff_w_qkv, diff_lam_q1, diff_lam_k1, diff_lam_q2, diff_lam_k2, diff_g_sub, diff_w_o,
           peer_w_query, peer_sub_keys, peer_u, peer_v,
           ln_mix_g, ln_mix_b, ln_ffn_g, ln_ffn_b):
    bp, tp, d = x_prompt.shape
    bs, ts, _ = x_sample.shape
    past = cache_mla_ckv.shape[2]
    mp, ms = bp * tp, bs * ts
    tks = past + ts
    tkp = -(-tks // LANE) * LANE
    tq_p = _tile(tp, 512, LANE)

    def pad_keys(a):
        return jnp.pad(a, ((0, 0), (0, tkp - tks), (0, 0))).reshape(bs * tkp, a.shape[-1])

    u_t_all = peer_u.astype(BF16).swapaxes(1, 2)
    v_all = peer_v.astype(BF16)
    xf = jnp.concatenate([x_prompt.reshape(mp, d), x_sample.reshape(ms, d)], 0)
    pos = jnp.concatenate([jnp.tile(jnp.arange(tp), bp), jnp.tile(past + jnp.arange(ts), bs)])
    cos, sin_a, sin_b = _rope_tables(pos)

    j = 0
    w_dqkv = jnp.pad(mla_w_dqkv[j], ((0, 0), (0, LANE - MLA_ROPE))).astype(BF16)
    cq, ckv_f, ckv_b, kr_f, kr_b = _mla_proj(xf, w_dqkv, mla_g_q[j], mla_g_kv[j], cos, sin_a, sin_b)
    hq = MLA_NOPE + MLA_ROPE
    w_uq = jnp.pad(mla_w_uq[j].reshape(MLA_Q_LORA, MLA_HEADS, hq),
                   ((0, 0), (0, 0), (0, 2 * LANE - hq))).reshape(MLA_Q_LORA, MLA_HEADS * 2 * LANE)
    q = _q_proj(cq, w_uq.astype(BF16), cos, sin_a, sin_b)
    w_ukv = mla_w_ukv[j].reshape(MLA_KV_LORA, MLA_HEADS, MLA_NOPE + MLA_V)
    w_uk_t = w_ukv[:, :, :MLA_NOPE].reshape(MLA_KV_LORA, -1).T.astype(BF16)
    w_uv = w_ukv[:, :, MLA_NOPE:].reshape(MLA_KV_LORA, -1).astype(BF16)
    lora_tiles = dict(tm=1024, tn=2048)
    (knt_p,) = _mm(w_uk_t, ckv_b[:mp].T, [BF16], **lora_tiles)
    (v_p,) = _mm(ckv_b[:mp], w_uv, [BF16], **lora_tiles)
    o_p = _mla_attention(q, knt_p, kr_b[:mp].T, v_p, batch=bp, t_q=tp, t_k=tp,
                         t_valid=tp, q_row0=0, q_off=0, tq=tq_p, tk=tq_p)
    ckv_cat = pad_keys(jnp.concatenate([cache_mla_ckv[j].astype(BF16),
                                        ckv_b[mp:].reshape(bs, ts, MLA_KV_LORA)], 1))
    kr_cache = jnp.pad(cache_mla_krope[j], ((0, 0), (0, 0), (0, LANE - MLA_ROPE))).astype(BF16)
    kr_cat = pad_keys(jnp.concatenate([kr_cache, kr_b[mp:].reshape(bs, ts, LANE)], 1))
    (knt_s,) = _mm(w_uk_t, ckv_cat.T, [BF16], **lora_tiles)
    (v_s,) = _mm(ckv_cat, w_uv, [BF16], **lora_tiles)
    o_s = _mla_attention(q, knt_s, kr_cat.T, v_s, batch=bs, t_q=ts, t_k=tkp,
                         t_valid=tks, q_row0=mp, q_off=past, tq=ts, tk=tkp)
    o = jnp.concatenate([o_p, o_s], 0)
    xf, xb = _mm_ln(o, mla_w_o[j].astype(BF16), xf, ln_mix_g[0], ln_mix_b[0])
    xf, xb = _peer_layer(xf, xb, peer_w_query[0], peer_sub_keys[0], u_t_all, v_all, 0,
                         ln_ffn_g[0], ln_ffn_b[0])

    i = 1
    lam_init = 0.8 - 0.6 * math.exp(-0.3 * i)
    nqk = DIFF_HEADS * 2 * DIFF_QK
    w_qkv = diff_w_qkv[j].astype(BF16)
    (dq,) = _mm(xb, w_qkv[:, :nqk], [BF16])
    dk_f, dk_b = _mm(xb, w_qkv[:, nqk:2 * nqk], [F32, BF16], tm=1056)
    dv_f, dv_b = _mm(xb, w_qkv[:, 2 * nqk:], [F32, BF16], tm=1056)
    slopes = 2.0 ** (-8.0 * jnp.arange(1, DIFF_HEADS + 1, dtype=jnp.float32) / DIFF_HEADS)
    lam_vecs = jnp.stack([diff_lam_q1[j], diff_lam_k1[j], diff_lam_q2[j], diff_lam_k2[j]])
    diff_args = (slopes, lam_vecs, diff_g_sub[j])
    o_p = _diff_attention(dq, dk_b[:mp].T, dv_b, *diff_args, k_transposed=True, batch=bp,
                          t_q=tp, t_k=tp, q_row0=0, q_off=0, tq=tq_p, tk=tq_p, lam_init=lam_init)
    k_cat = jnp.concatenate([cache_diff_k[j].reshape(bs, past, nqk).astype(BF16),
                             dk_b[mp:].reshape(bs, ts, nqk)], 1).reshape(bs * tks, nqk)
    v_cat = jnp.concatenate([cache_diff_v[j].reshape(bs, past, -1).astype(BF16),
                             dv_b[mp:].reshape(bs, ts, -1)], 1).reshape(bs * tks, -1)
    o_s = _diff_attention(dq, k_cat, v_cat, *diff_args, k_transposed=False, batch=bs,
                          t_q=ts, t_k=tks, q_row0=mp, q_off=past, tq=ts, tk=tks, lam_init=lam_init)
    o = jnp.concatenate([o_p, o_s], 0)
    xf, xb = _mm_ln(o, diff_w_o[j].astype(BF16), xf, ln_mix_g[1], ln_mix_b[1])
    xf, xb = _peer_layer(xf, xb, peer_w_query[1], peer_sub_keys[1], u_t_all, v_all, 1,
                         ln_ffn_g[1], ln_ffn_b[1])

    kr_f = kr_f[:, :MLA_ROPE]
    return (xf[:mp].reshape(bp, tp, d), xf[mp:].reshape(bs, ts, d),
            ckv_f[:mp].reshape(1, bp, tp, -1), kr_f[:mp].reshape(1, bp, tp, -1),
            dk_f[:mp].reshape(1, bp, tp, DIFF_HEADS, -1), dv_f[:mp].reshape(1, bp, tp, DIFF_HEADS, -1),
            ckv_f[mp:].reshape(1, bs, ts, -1), kr_f[mp:].reshape(1, bs, ts, -1),
            dk_f[mp:].reshape(1, bs, ts, DIFF_HEADS, -1), dv_f[mp:].reshape(1, bs, ts, DIFF_HEADS, -1))
```
